```python
import jax
import jax.numpy as jnp
from jax import lax
import numpy as np

D_MODEL = 2048
BATCH = 2
SEQ = 4096
DEPTH = 4
DEC_BATCH = 8
DEC_SEQ = 8
PAST_LEN = 16384
PAGE_SIZE = 128

N_POOL_LAYERS = DEPTH // 2
N_ATTN_LAYERS = DEPTH - N_POOL_LAYERS
POOL_WINDOWS = (2, 4, 8, 16)
N_POOL_GROUPS = len(POOL_WINDOWS)
POOL_GROUP_DIM = D_MODEL // N_POOL_GROUPS
POOL_STATE_ROWS = max(POOL_WINDOWS) - 1
HEAD_DIM = 128
N_HEADS = D_MODEL // HEAD_DIM
MOBA_BLOCK = 256
MOBA_TOP_K = 3
PAGES_PER_BLOCK = MOBA_BLOCK // PAGE_SIZE
Q_CHUNK = 16
D_FF = -(-8 * D_MODEL // (3 * 256)) * 256
ROPE_THETA = 10000.0
RMS_EPS = 1e-6
NEG_INF = -1e30

kernel_name = 'yoco_pool_moba_decoder_step'


def rmsnorm(x, g):
    xf = x.astype(jnp.float32)
    r = lax.rsqrt(jnp.mean(xf * xf, axis=-1, keepdims=True) + RMS_EPS)
    return (xf * r * g.astype(jnp.float32)).astype(x.dtype)


def rope(x, pos):
    dh = x.shape[-1]
    half = dh // 2
    inv = 1.0 / (ROPE_THETA ** (jnp.arange(half, dtype=jnp.float32) * (2.0 / dh)))
    ang = pos.astype(jnp.float32)[:, None] * inv[None, :]
    cos = jnp.cos(ang)[None, :, None, :]
    sin = jnp.sin(ang)[None, :, None, :]
    xf = x.astype(jnp.float32)
    x1, x2 = xf[..., :half], xf[..., half:]
    return jnp.concatenate([x1 * cos - x2 * sin, x2 * cos + x1 * sin], axis=-1).astype(x.dtype)


def swiglu(x, w_gu, w_dn):
    g, u = jnp.split(x @ w_gu, 2, axis=-1)
    return (jax.nn.silu(g) * u) @ w_dn


def pool_mixer(z, n_prefix, pos0, w_grp, scale):
    B, L, D = z.shape
    T = L - n_prefix
    csum = jnp.cumsum(z.astype(jnp.float32), axis=1)
    csum = jnp.concatenate([jnp.zeros((B, 1, D), jnp.float32), csum], axis=1)
    end = n_prefix + jnp.arange(T) + 1
    pos = pos0 + jnp.arange(T)
    z_t = z[:, n_prefix:].astype(jnp.float32)
    hi = csum[:, end]
    outs = []
    for g, w in enumerate(POOL_WINDOWS):
        sl = slice(g * POOL_GROUP_DIM, (g + 1) * POOL_GROUP_DIM)
        start = jnp.maximum(end - w, 0)
        cnt = jnp.minimum(pos + 1, w).astype(jnp.float32)
        mean = (hi[..., sl] - csum[:, start][..., sl]) / cnt[None, :, None]
        outs.append(mean - z_t[..., sl])
    pooled = jnp.stack(outs, axis=2).astype(z.dtype)
    mixed = jnp.einsum('btgc,gce->btge', pooled, w_grp).reshape(B, T, D)
    return mixed * scale


def shared_kv(h, g_kv, w_kv, pos):
    B, T, _ = h.shape
    kv = (rmsnorm(h, g_kv) @ w_kv).reshape(B, T, 2, N_HEADS, HEAD_DIM)
    return rope(kv[:, :, 0], pos), kv[:, :, 1]


def moba_prompt(q, k, v):
    B, S, H, Dh = q.shape
    scale = Dh ** -0.5
    n_blk = -(-S // MOBA_BLOCK)
    s_pad = n_blk * MOBA_BLOCK
    pad = ((0, 0), (0, s_pad - S), (0, 0), (0, 0))
    qh = jnp.pad(q, pad).transpose(0, 2, 1, 3)
    kb = jnp.pad(k, pad).transpose(0, 2, 1, 3).reshape(B, H, n_blk, MOBA_BLOCK, Dh)
    vb = jnp.pad(v, pad).transpose(0, 2, 1, 3).reshape(B, H, n_blk, MOBA_BLOCK, Dh)
    k_mean = jnp.mean(kb, axis=3, dtype=jnp.float32)
    own = jnp.arange(s_pad) // MOBA_BLOCK
    gate = jnp.einsum('bhsd,bhnd->bhsn', qh.astype(jnp.float32), k_mean)
    fully_past = jnp.arange(n_blk)[None, :] < own[:, None]
    gate = jnp.where(fully_past[None, None], gate, NEG_INF)
    top_k = min(MOBA_TOP_K, n_blk)
    _, sel = lax.top_k(gate, top_k)
    sel_ok = jnp.arange(top_k)[None, :] < own[:, None]
    b_ix = jnp.arange(B)[:, None, None, None]
    h_ix = jnp.arange(H)[None, :, None, None]
    n_sel = top_k * MOBA_BLOCK

    def chunk(c):
        q0 = c * Q_CHUNK
        qc = lax.dynamic_slice_in_dim(qh, q0, Q_CHUNK, axis=2)
        sc = lax.dynamic_slice_in_dim(sel, q0, Q_CHUNK, axis=2)
        okc = lax.dynamic_slice_in_dim(sel_ok, q0, Q_CHUNK, axis=0)
        qpos = q0 + jnp.arange(Q_CHUNK)
        ob = q0 // MOBA_BLOCK
        k_sel = kb[b_ix, h_ix, sc]
        v_sel = vb[b_ix, h_ix, sc].reshape(B, H, Q_CHUNK, n_sel, Dh)
        k_own = lax.dynamic_index_in_dim(kb, ob, axis=2, keepdims=False)
        v_own = lax.dynamic_index_in_dim(vb, ob, axis=2, keepdims=False)
        s_sel = jnp.einsum('bhqd,bhqknd->bhqkn', qc, k_sel, preferred_element_type=jnp.float32) * scale
        s_sel = jnp.where(okc[None, None, :, :, None], s_sel, NEG_INF).reshape(B, H, Q_CHUNK, n_sel)
        s_own = jnp.einsum('bhqd,bhnd->bhqn', qc, k_own, preferred_element_type=jnp.float32) * scale
        kpos = ob * MOBA_BLOCK + jnp.arange(MOBA_BLOCK)
        s_own = jnp.where((kpos[None, :] <= qpos[:, None])[None, None], s_own, NEG_INF)
        p = jax.nn.softmax(jnp.concatenate([s_sel, s_own], axis=-1), axis=-1).astype(v.dtype)
        return (jnp.einsum('bhqn,bhqnd->bhqd', p[..., :n_sel], v_sel)
                + jnp.einsum('bhqn,bhnd->bhqd', p[..., n_sel:], v_own))

    out = lax.map(chunk, jnp.arange(s_pad // Q_CHUNK))
    out = out.transpose(1, 0, 3, 2, 4).reshape(B, s_pad, H, Dh)
    return out[:, :S]


def moba_sample(q, k_new, v_new, cache_k, cache_v, page_table):
    DB, T, H, Dh = q.shape
    scale = Dh ** -0.5
    n_pages = page_table.shape[1]
    n_past_blk = PAST_LEN // MOBA_BLOCK
    own_first_page = n_past_blk * PAGES_PER_BLOCK
    n_own_pages = n_pages - own_first_page
    qh = q.transpose(0, 2, 1, 3)
    qpos = PAST_LEN + jnp.arange(T)
    own_pages = page_table[:, own_first_page:]
    k_own = jnp.concatenate([cache_k[own_pages].reshape(DB, n_own_pages * PAGE_SIZE, H, Dh).astype(k_new.dtype), k_new], axis=1)
    v_own = jnp.concatenate([cache_v[own_pages].reshape(DB, n_own_pages * PAGE_SIZE, H, Dh).astype(v_new.dtype), v_new], axis=1)
    kpos = n_past_blk * MOBA_BLOCK + jnp.arange(k_own.shape[1])
    s_own = jnp.einsum('bhtd,bnhd->bhtn', qh, k_own, preferred_element_type=jnp.float32) * scale
    s_own = jnp.where((kpos[None, :] <= qpos[:, None])[None, None], s_own, NEG_INF)
    if n_past_blk == 0:
        p = jax.nn.softmax(s_own, axis=-1).astype(v_new.dtype)
        return jnp.einsum('bhtn,bnhd->bthd', p, v_own)
    page_mean = jnp.mean(cache_k, axis=1, dtype=jnp.float32)
    blk_mean = page_mean[page_table[:, :own_first_page]].reshape(DB, n_past_blk, PAGES_PER_BLOCK, H, Dh).mean(axis=2)
    gate = jnp.einsum('bhtd,bnhd->bhtn', qh.astype(jnp.float32), blk_mean)
    top_k = min(MOBA_TOP_K, n_past_blk)
    _, sel = lax.top_k(gate, top_k)
    logical = sel[..., None] * PAGES_PER_BLOCK + jnp.arange(PAGES_PER_BLOCK)
    phys = page_table[jnp.arange(DB)[:, None, None, None, None], logical]
    row = jnp.arange(PAGE_SIZE)
    h_ix = jnp.arange(H)[None, :, None, None, None, None]
    n_sel = top_k * MOBA_BLOCK
    k_sel = cache_k[phys[..., None], row, h_ix].reshape(DB, H, T, n_sel, Dh).astype(k_new.dtype)
    v_sel = cache_v[phys[..., None], row, h_ix].reshape(DB, H, T, n_sel, Dh).astype(v_new.dtype)
    s_sel = jnp.einsum('bhtd,bhtnd->bhtn', qh, k_sel, preferred_element_type=jnp.float32) * scale
    p = jax.nn.softmax(jnp.concatenate([s_sel, s_own], axis=-1), axis=-1).astype(v_new.dtype)
    return (jnp.einsum('bhtn,bhtnd->bthd', p[..., :n_sel], v_sel)
            + jnp.einsum('bhtn,bnhd->bthd', p[..., n_sel:], v_own))


def setup_inputs(seed: int = 0) -> dict:
    key = jax.random.key(seed)
    ks = jax.random.split(key, 20)
    f32 = jnp.float32
    n_pages = PAST_LEN // PAGE_SIZE
    n_phys = (5 * DEC_BATCH * n_pages + 3) // 4

    def nrm(k, shape, s):
        return jax.random.normal(k, shape, f32) * s

    def gain(k, shape):
        return 1.0 + nrm(k, shape, 0.05)

    page_table = jax.random.permutation(ks[5], n_phys)[:DEC_BATCH * n_pages].reshape(DEC_BATCH, n_pages).astype(jnp.int32)
    return {
        'x_prompt': nrm(ks[0], (BATCH, SEQ, D_MODEL), 1.0),
        'x_sample': nrm(ks[1], (DEC_BATCH, DEC_SEQ, D_MODEL), 1.0),
        'state_pool': nrm(ks[2], (N_POOL_LAYERS, DEC_BATCH, POOL_STATE_ROWS, D_MODEL), 1.0),
        'cache_k': nrm(ks[3], (n_phys, PAGE_SIZE, N_HEADS, HEAD_DIM), 1.0),
        'cache_v': nrm(ks[4], (n_phys, PAGE_SIZE, N_HEADS, HEAD_DIM), 1.0),
        'page_table': page_table,
        'g_pool': gain(ks[6], (N_POOL_LAYERS, D_MODEL)),
        'w_pool': nrm(ks[7], (N_POOL_LAYERS, N_POOL_GROUPS, POOL_GROUP_DIM, POOL_GROUP_DIM), POOL_GROUP_DIM ** -0.5),
        's_pool': 0.5 + nrm(ks[8], (N_POOL_LAYERS, D_MODEL), 0.05),
        'g_ffn': gain(ks[9], (DEPTH, D_MODEL)),
        'w_gate_up': nrm(ks[10], (DEPTH, D_MODEL, 2 * D_FF), D_MODEL ** -0.5),
        'w_down': nrm(ks[11], (DEPTH, D_FF, D_MODEL), D_FF ** -0.5),
        'g_kv': gain(ks[12], (D_MODEL,)),
        'w_kv': nrm(ks[13], (D_MODEL, 2 * D_MODEL), D_MODEL ** -0.5),
        'g_attn': gain(ks[14], (N_ATTN_LAYERS, D_MODEL)),
        'w_q': nrm(ks[15], (N_ATTN_LAYERS, D_MODEL, D_MODEL), D_MODEL ** -0.5),
        'w_o': nrm(ks[16], (N_ATTN_LAYERS, D_MODEL, D_MODEL), D_MODEL ** -0.5),
        'g_final': gain(ks[17], (D_MODEL,)),
    }


def reference(x_prompt, x_sample, state_pool, cache_k, cache_v, page_table, g_pool, w_pool, s_pool,
              g_ffn, w_gate_up, w_down, g_kv, w_kv, g_attn, w_q, w_o, g_final):
    B, S, D = x_prompt.shape
    DB, T, _ = x_sample.shape
    pos_p = jnp.arange(S)
    pos_s = PAST_LEN + jnp.arange(T)
    hp, hs = x_prompt, x_sample
    pool_p, pool_s = [], []
    for layer in range(DEPTH):
        if layer < N_POOL_LAYERS:
            i = layer
            zp = rmsnorm(hp, g_pool[i])
            zs = jnp.concatenate([state_pool[i].astype(hs.dtype), rmsnorm(hs, g_pool[i])], axis=1)
            hp = hp + pool_mixer(zp, 0, 0, w_pool[i], s_pool[i])
            hs = hs + pool_mixer(zs, POOL_STATE_ROWS, PAST_LEN, w_pool[i], s_pool[i])
            pool_p.append(zp[:, -POOL_STATE_ROWS:])
            pool_s.append(zs[:, -POOL_STATE_ROWS:])
        else:
            i = layer - N_POOL_LAYERS
            if i == 0:
                k_p, v_p = shared_kv(hp, g_kv, w_kv, pos_p)
                k_s, v_s = shared_kv(hs, g_kv, w_kv, pos_s)
            q_p = rope((rmsnorm(hp, g_attn[i]) @ w_q[i]).reshape(B, S, N_HEADS, HEAD_DIM), pos_p)
            q_s = rope((rmsnorm(hs, g_attn[i]) @ w_q[i]).reshape(DB, T, N_HEADS, HEAD_DIM), pos_s)
            hp = hp + moba_prompt(q_p, k_p, v_p).reshape(B, S, D) @ w_o[i]
            hs = hs + moba_sample(q_s, k_s, v_s, cache_k, cache_v, page_table).reshape(DB, T, D) @ w_o[i]
        hp = hp + swiglu(rmsnorm(hp, g_ffn[layer]), w_gate_up[layer], w_down[layer])
        hs = hs + swiglu(rmsnorm(hs, g_ffn[layer]), w_gate_up[layer], w_down[layer])
    y_prompt = rmsnorm(hp, g_final)
    y_sample = rmsnorm(hs, g_final)
    new_pool_prompt = jnp.stack(pool_p)
    new_pool_sample = jnp.stack(pool_s)
    return (y_prompt, y_sample, new_pool_prompt, new_pool_sample, k_p, v_p, k_s, v_s)
```

```python
import functools

import jax
import jax.numpy as jnp
from jax import lax
from jax.experimental import pallas as pl
from jax.experimental.pallas import tpu as pltpu

F32 = jnp.float32
BF16 = jnp.bfloat16

POOL_WINDOWS = (2, 4, 8, 16)
POOL_HALO = 16
HEAD_DIM = 128
MOBA_BLOCK = 256
MOBA_TOP_K = 3
PAGE_SIZE = 128
PAGES_PER_BLOCK = MOBA_BLOCK // PAGE_SIZE
PAST_LEN = 16384
ROPE_THETA = 10000.0
RMS_EPS = 1e-6
NEG_INF = -1e30
GATE_FLOOR = -3.0e38

VMEM_LIMIT_BYTES = 56 * 1024 * 1024


def _params(n_axes):
    return pltpu.CompilerParams(dimension_semantics=("arbitrary",) * n_axes, vmem_limit_bytes=VMEM_LIMIT_BYTES)


def _rmsnorm(x, g):
    ms = jnp.mean(x * x, axis=-1, keepdims=True)
    return x * lax.rsqrt(ms + RMS_EPS) * g


def _pool_kernel(x_ref, halo_ref, g_ref, w_ref, s_ref, h_ref, zt_ref, zs_ref, *, tm, halo_normalized, pos0):
    i = pl.program_id(1)
    d = x_ref.shape[-1]
    c = d // len(POOL_WINDOWS)
    x = x_ref[...]
    g = g_ref[...]
    z = _rmsnorm(x, g)
    if halo_normalized:
        zh = halo_ref[...]
    else:
        zh = _rmsnorm(halo_ref[...], g) * (i > 0).astype(F32)
    zs_ref[0:POOL_HALO, :] = zh
    zs_ref[POOL_HALO:POOL_HALO + tm, :] = z
    pos = lax.broadcasted_iota(jnp.int32, (tm, c), 0) + (i * tm + pos0)
    for gi, w in enumerate(POOL_WINDOWS):
        cols = slice(gi * c, (gi + 1) * c)
        zc = z[:, cols]
        tot = zc
        for j in range(1, w):
            tot = tot + zs_ref[POOL_HALO - j:POOL_HALO - j + tm, cols]
        cnt = jnp.minimum(pos + 1, w).astype(F32)
        pooled = tot / cnt - zc
        mixed = jnp.dot(pooled.astype(BF16), w_ref[gi], preferred_element_type=F32)
        h_ref[:, cols] = x[:, cols] + mixed * s_ref[:, cols]

    nt = zt_ref.shape[0]

    @pl.when(i == pl.num_programs(1) - 1)
    def _():
        zt_ref[...] = z[tm - nt:, :]


def _pool_layer(x3, halo3, g, w_b, s, *, tm, halo_normalized, pos0):
    nb, length, d = x3.shape
    nt = min(POOL_HALO, tm)
    n_i = length // tm
    hb = tm // POOL_HALO
    if halo_normalized:
        halo_map = lambda b, i: (b, 0, 0)
    else:
        halo_map = lambda b, i: (b, jnp.maximum(i * hb - 1, 0), 0)
    kern = functools.partial(_pool_kernel, tm=tm, halo_normalized=halo_normalized, pos0=pos0)
    return pl.pallas_call(
        kern,
        grid=(nb, n_i),
        in_specs=[
            pl.BlockSpec((None, tm, d), lambda b, i: (b, i, 0)),
            pl.BlockSpec((None, POOL_HALO, d), halo_map),
            pl.BlockSpec((1, d), lambda b, i: (0, 0)),
            pl.BlockSpec(w_b.shape, lambda b, i: (0, 0, 0)),
            pl.BlockSpec((1, d), lambda b, i: (0, 0)),
        ],
        out_specs=[
            pl.BlockSpec((None, tm, d), lambda b, i: (b, i, 0)),
            pl.BlockSpec((None, nt, d), lambda b, i: (b, 0, 0)),
        ],
        out_shape=[jax.ShapeDtypeStruct((nb, length, d), F32), jax.ShapeDtypeStruct((nb, nt, d), F32)],
        scratch_shapes=[pltpu.VMEM((POOL_HALO + tm, d), F32)],
        compiler_params=_params(2),
        name="pool_layer",
    )(x3, halo3, g.reshape(1, d), w_b, s.reshape(1, d))


def _ffn_kernel(x_ref, g_ref, wg_ref, wu_ref, wd_ref, go_ref, o_ref, xn_ref, *, final_norm):
    j = pl.program_id(1)

    @pl.when(j == 0)
    def _():
        x = x_ref[...]
        xn_ref[...] = _rmsnorm(x, g_ref[...]).astype(BF16)
        o_ref[...] = x

    xn = xn_ref[...]
    gate = jnp.dot(xn, wg_ref[...], preferred_element_type=F32)
    up = jnp.dot(xn, wu_ref[...], preferred_element_type=F32)
    act = (gate * jax.nn.sigmoid(gate) * up).astype(BF16)
    o_ref[...] += jnp.dot(act, wd_ref[...], preferred_element_type=F32)

    if final_norm:
        @pl.when(j == pl.num_programs(1) - 1)
        def _():
            o_ref[...] = _rmsnorm(o_ref[...], go_ref[...])


def _ffn(x, g, w_gu_b, w_dn_b, layer, g_out, *, tm, tf, final_norm):
    m, d = x.shape
    f = w_dn_b.shape[1]
    n_f = f // tf
    kern = functools.partial(_ffn_kernel, final_norm=final_norm)
    return pl.pallas_call(
        kern,
        grid=(m // tm, n_f),
        in_specs=[
            pl.BlockSpec((tm, d), lambda i, j: (i, 0)),
            pl.BlockSpec((1, d), lambda i, j: (0, 0)),
            pl.BlockSpec((None, d, tf), lambda i, j: (layer, 0, j)),
            pl.BlockSpec((None, d, tf), lambda i, j: (layer, 0, j + n_f)),
            pl.BlockSpec((None, tf, d), lambda i, j: (layer, j, 0)),
            pl.BlockSpec((1, d), lambda i, j: (0, 0)),
        ],
        out_specs=pl.BlockSpec((tm, d), lambda i, j: (i, 0)),
        out_shape=jax.ShapeDtypeStruct((m, d), F32),
        scratch_shapes=[pltpu.VMEM((tm, d), BF16)],
        compiler_params=_params(2),
        name="swiglu",
    )(x, g.reshape(1, d), w_gu_b, w_gu_b, w_dn_b, g_out.reshape(1, d))


def _rope(xh, cos, sin_signed):
    return xh * cos + pltpu.roll(xh, HEAD_DIM // 2, axis=1) * sin_signed


def _q_kernel(x_ref, g_ref, w_ref, cos_ref, sin_ref, q_ref):
    xn = _rmsnorm(x_ref[...], g_ref[...]).astype(BF16)
    acc = jnp.dot(xn, w_ref[...], preferred_element_type=F32)
    cos = cos_ref[...]
    sin = sin_ref[...]
    for h in range(acc.shape[1] // HEAD_DIM):
        cols = slice(h * HEAD_DIM, (h + 1) * HEAD_DIM)
        q_ref[:, cols] = _rope(acc[:, cols], cos, sin)


def _q_proj(x, g, w_b, layer, cos, sin, *, tm):
    m, d = x.shape
    n_pos = cos.shape[0] // tm
    return pl.pallas_call(
        _q_kernel,
        grid=(m // tm,),
        in_specs=[
            pl.BlockSpec((tm, d), lambda i: (i, 0)),
            pl.BlockSpec((1, d), lambda i: (0, 0)),
            pl.BlockSpec((None, d, d), lambda i: (layer, 0, 0)),
            pl.BlockSpec((tm, HEAD_DIM), lambda i: (i % n_pos, 0)),
            pl.BlockSpec((tm, HEAD_DIM), lambda i: (i % n_pos, 0)),
        ],
        out_specs=pl.BlockSpec((tm, d), lambda i: (i, 0)),
        out_shape=jax.ShapeDtypeStruct((m, d), F32),
        compiler_params=_params(1),
        name="q_proj",
    )(x, g.reshape(1, d), w_b, cos, sin)


def _k_kernel(x_ref, g_ref, w_ref, cos_ref, sin_ref, k_ref, *extra, with_blocks):
    xn = _rmsnorm(x_ref[...], g_ref[...]).astype(BF16)
    acc = jnp.dot(xn, w_ref[...], preferred_element_type=F32)
    cos = cos_ref[...]
    sin = sin_ref[...]
    for h in range(acc.shape[1] // HEAD_DIM):
        cols = slice(h * HEAD_DIM, (h + 1) * HEAD_DIM)
        k_ref[:, cols] = _rope(acc[:, cols], cos, sin)
    if with_blocks:
        kb_ref, km_ref = extra
        k = k_ref[...]
        kb_ref[...] = k.astype(BF16)
        for r in range(k.shape[0] // MOBA_BLOCK):
            km_ref[r] = jnp.mean(k[r * MOBA_BLOCK:(r + 1) * MOBA_BLOCK, :], axis=0, keepdims=True)


def _k_proj(x, g, w_kv_b, cos, sin, *, tm, with_blocks):
    m, d = x.shape
    n_pos = cos.shape[0] // tm
    out_specs = [pl.BlockSpec((tm, d), lambda i: (i, 0))]
    out_shape = [jax.ShapeDtypeStruct((m, d), F32)]
    if with_blocks:
        r = tm // MOBA_BLOCK
        out_specs += [pl.BlockSpec((tm, d), lambda i: (i, 0)), pl.BlockSpec((r, 1, d), lambda i: (i, 0, 0))]
        out_shape += [jax.ShapeDtypeStruct((m, d), BF16), jax.ShapeDtypeStruct((m // MOBA_BLOCK, 1, d), F32)]
    return pl.pallas_call(
        functools.partial(_k_kernel, with_blocks=with_blocks),
        grid=(m // tm,),
        in_specs=[
            pl.BlockSpec((tm, d), lambda i: (i, 0)),
            pl.BlockSpec((1, d), lambda i: (0, 0)),
            pl.BlockSpec((d, d), lambda i: (0, 0)),
            pl.BlockSpec((tm, HEAD_DIM), lambda i: (i % n_pos, 0)),
            pl.BlockSpec((tm, HEAD_DIM), lambda i: (i % n_pos, 0)),
        ],
        out_specs=out_specs,
        out_shape=out_shape,
        compiler_params=_params(1),
        name="k_proj",
    )(x, g.reshape(1, d), w_kv_b, cos, sin)


def _v_kernel(x_ref, g_ref, w_ref, v_ref, *extra, with_blocks):
    xn = _rmsnorm(x_ref[...], g_ref[...]).astype(BF16)
    v = jnp.dot(xn, w_ref[...], preferred_element_type=F32)
    v_ref[...] = v
    if with_blocks:
        (vt_ref,) = extra
        for h in range(v.shape[1] // HEAD_DIM):
            for r in range(v.shape[0] // MOBA_BLOCK):
                blk = v[r * MOBA_BLOCK:(r + 1) * MOBA_BLOCK, h * HEAD_DIM:(h + 1) * HEAD_DIM]
                vt_ref[h, r] = blk.T.astype(BF16)


def _v_proj(x, g, w_kv_b, *, tm, with_blocks, seq):
    m, d = x.shape
    n_h = d // HEAD_DIM
    out_specs = [pl.BlockSpec((tm, d), lambda i: (i, 0))]
    out_shape = [jax.ShapeDtypeStruct((m, d), F32)]
    if with_blocks:
        r = tm // MOBA_BLOCK
        tiles_per_seq = seq // tm
        out_specs.append(pl.BlockSpec((None, n_h, r, HEAD_DIM, MOBA_BLOCK),
                                      lambda i: (i // tiles_per_seq, 0, i % tiles_per_seq, 0, 0)))
        out_shape.append(jax.ShapeDtypeStruct((m // seq, n_h, seq // MOBA_BLOCK, HEAD_DIM, MOBA_BLOCK), BF16))
    return pl.pallas_call(
        functools.partial(_v_kernel, with_blocks=with_blocks),
        grid=(m // tm,),
        in_specs=[
            pl.BlockSpec((tm, d), lambda i: (i, 0)),
            pl.BlockSpec((1, d), lambda i: (0, 0)),
            pl.BlockSpec((d, d), lambda i: (0, 1)),
        ],
        out_specs=out_specs,
        out_shape=out_shape,
        compiler_params=_params(1),
        name="v_proj",
    )(x, g.reshape(1, d), w_kv_b)


def _o_kernel(a_ref, w_ref, h_ref, o_ref):
    o_ref[...] = h_ref[...] + jnp.dot(a_ref[...].astype(BF16), w_ref[...], preferred_element_type=F32)


def _o_proj(a, w_b, layer, h, *, tm):
    m, d = h.shape
    return pl.pallas_call(
        _o_kernel,
        grid=(m // tm,),
        in_specs=[
            pl.BlockSpec((tm, d), lambda i: (i, 0)),
            pl.BlockSpec((None, d, d), lambda i: (layer, 0, 0)),
            pl.BlockSpec((tm, d), lambda i: (i, 0)),
        ],
        out_specs=pl.BlockSpec((tm, d), lambda i: (i, 0)),
        out_shape=jax.ShapeDtypeStruct((m, d), F32),
        compiler_params=_params(1),
        name="o_proj",
    )(a, w_b, h)


def _top_k_picks(gate, candidate, blk_f, axis):
    remaining = candidate
    for _ in range(MOBA_TOP_K):
        gm = jnp.where(remaining, gate, GATE_FLOOR)
        top = jnp.max(gm, axis=axis, keepdims=True)
        hit = jnp.logical_and(remaining, gm == top)
        idx = jnp.min(jnp.where(hit, blk_f, 1e9), axis=axis, keepdims=True)
        pick = blk_f == idx
        remaining = jnp.logical_and(remaining, jnp.logical_not(pick))
        yield pick, idx


def _attn_kernel(q_ref, k_ref, vt_ref, km_ref, o_ref, bias_ref):
    i = pl.program_id(2)
    tq = q_ref.shape[0]
    n_blk = km_ref.shape[0]
    scale = HEAD_DIM ** -0.5
    q_t = q_ref[...].T
    gate_t = jnp.dot(km_ref[...], q_t, preferred_element_type=F32, precision=lax.Precision.HIGHEST)
    blk_f = lax.broadcasted_iota(jnp.int32, (n_blk, tq), 0).astype(F32)
    selected = jnp.zeros((n_blk, tq), jnp.bool_)
    for pick, _ in _top_k_picks(gate_t, blk_f < i.astype(F32), blk_f, 0):
        selected = jnp.logical_or(selected, pick)
    bias_ref[...] = jnp.where(selected, 0.0, NEG_INF)
    q_tb = q_t.astype(BF16)

    start = pl.multiple_of(i * MOBA_BLOCK, MOBA_BLOCK)
    s = jnp.dot(k_ref[pl.ds(start, MOBA_BLOCK), :], q_tb, preferred_element_type=F32) * scale
    kpos = lax.broadcasted_iota(jnp.int32, s.shape, 0)
    qpos = lax.broadcasted_iota(jnp.int32, s.shape, 1)
    s = jnp.where(kpos <= qpos, s, NEG_INF)
    m0 = jnp.max(s, axis=0, keepdims=True)
    p = jnp.exp(s - m0)
    l0 = jnp.sum(p, axis=0, keepdims=True)
    acc0 = jnp.dot(vt_ref[i], p.astype(BF16), preferred_element_type=F32)

    def body(j, carry):
        m, l, acc = carry
        st = pl.multiple_of(j * MOBA_BLOCK, MOBA_BLOCK)
        sj = jnp.dot(k_ref[pl.ds(st, MOBA_BLOCK), :], q_tb, preferred_element_type=F32) * scale
        sj = sj + bias_ref[pl.ds(j, 1), :]
        m_new = jnp.maximum(m, jnp.max(sj, axis=0, keepdims=True))
        alpha = jnp.exp(m - m_new)
        pj = jnp.exp(sj - m_new)
        l = alpha * l + jnp.sum(pj, axis=0, keepdims=True)
        acc = alpha * acc + jnp.dot(vt_ref[j], pj.astype(BF16), preferred_element_type=F32)
        return m_new, l, acc

    _, l, acc = lax.fori_loop(0, i, body, (m0, l0, acc0))
    o_ref[...] = (acc / l).T.astype(o_ref.dtype)


def _moba_prompt(q, k_b, vt_b, kmean, *, batch, seq):
    m, d = q.shape
    n_h = d // HEAD_DIM
    n_blk = seq // MOBA_BLOCK
    tq = MOBA_BLOCK
    return pl.pallas_call(
        _attn_kernel,
        grid=(batch, n_h, n_blk),
        in_specs=[
            pl.BlockSpec((tq, HEAD_DIM), lambda b, h, i: (b * n_blk + i, h)),
            pl.BlockSpec((seq, HEAD_DIM), lambda b, h, i: (b, h)),
            pl.BlockSpec((None, None, n_blk, HEAD_DIM, MOBA_BLOCK), lambda b, h, i: (b, h, 0, 0, 0)),
            pl.BlockSpec((None, n_blk, HEAD_DIM), lambda b, h, i: (b, 0, h)),
        ],
        out_specs=pl.BlockSpec((tq, HEAD_DIM), lambda b, h, i: (b * n_blk + i, h)),
        out_shape=jax.ShapeDtypeStruct((m, d), BF16),
        scratch_shapes=[pltpu.VMEM((n_blk, tq), F32)],
        compiler_params=_params(3),
        name="moba_prompt",
    )(q, k_b, vt_b, kmean)


def _block_mean_kernel(pt_ref, p0_ref, p1_ref, o_ref):
    del pt_ref
    m0 = jnp.mean(p0_ref[...], axis=0)
    m1 = jnp.mean(p1_ref[...], axis=0)
    o_ref[...] = (m0 + m1) * 0.5


def _cache_block_means(cache_k, page_table_flat, *, n_seq, n_pages):
    _, page, n_h, dh = cache_k.shape
    n_blk = n_pages // PAGES_PER_BLOCK
    grid_spec = pltpu.PrefetchScalarGridSpec(
        num_scalar_prefetch=1,
        grid=(n_seq, n_blk),
        in_specs=[
            pl.BlockSpec((None, page, n_h, dh), lambda b, n, pt: (pt[b * n_pages + 2 * n], 0, 0, 0)),
            pl.BlockSpec((None, page, n_h, dh), lambda b, n, pt: (pt[b * n_pages + 2 * n + 1], 0, 0, 0)),
        ],
        out_specs=pl.BlockSpec((None, None, n_h, dh), lambda b, n, pt: (b, n, 0, 0)),
    )
    return pl.pallas_call(
        _block_mean_kernel,
        grid_spec=grid_spec,
        out_shape=jax.ShapeDtypeStruct((n_seq, n_blk, n_h, dh), F32),
        compiler_params=_params(2),
        name="cache_block_means",
    )(page_table_flat, cache_k, cache_k)


def _select_kernel(q_ref, bm_ref, sel_ref, *, n_valid):
    q = q_ref[...]
    t = q.shape[0]
    n_lane = bm_ref.shape[1]
    lane = lax.broadcasted_iota(jnp.int32, (t, n_lane), 1)
    lane_f = lane.astype(F32)
    out = jnp.zeros((t, n_lane), jnp.int32)
    for h in range(q.shape[1] // HEAD_DIM):
        cols = slice(h * HEAD_DIM, (h + 1) * HEAD_DIM)
        gate = lax.dot_general(q[:, cols], bm_ref[h], (((1,), (1,)), ((), ())),
                               preferred_element_type=F32, precision=lax.Precision.HIGHEST)
        for r, (_, idx) in enumerate(_top_k_picks(gate, lane < n_valid, lane_f, 1)):
            out = jnp.where(lane == h * MOBA_TOP_K + r, idx.astype(jnp.int32), out)
    sel_ref[...] = out


def _select_blocks(q, bm_pad, *, n_seq, n_valid):
    m, d = q.shape
    t = m // n_seq
    _, n_h, n_lane, dh = bm_pad.shape
    return pl.pallas_call(
        functools.partial(_select_kernel, n_valid=n_valid),
        grid=(n_seq,),
        in_specs=[
            pl.BlockSpec((t, d), lambda b: (b, 0)),
            pl.BlockSpec((None, n_h, n_lane, dh), lambda b: (b, 0, 0, 0)),
        ],
        out_specs=pl.BlockSpec((None, t, n_lane), lambda b: (b, 0, 0)),
        out_shape=jax.ShapeDtypeStruct((n_seq, t, n_lane), jnp.int32),
        compiler_params=_params(1),
        name="select_blocks",
    )(q, bm_pad)


def _sample_attn_kernel(sel_ref, pt_ref, q_ref, kn_ref, vn_ref, ck_hbm, cv_hbm, o_ref, kg_ref, vg_ref, sem,
                        *, n_pages):
    b = pl.program_id(0)
    h = pl.program_id(1)
    n_h = pl.num_programs(1)
    n_steps = pl.num_programs(0) * n_h
    step = b * n_h + h
    slot = step % 2
    t_len = q_ref.shape[0]
    n_sel = MOBA_TOP_K * MOBA_BLOCK
    scale = HEAD_DIM ** -0.5

    def gather_copies(bb, hh, sl):
        copies = []
        for t in range(t_len):
            for r in range(MOBA_TOP_K):
                blk = sel_ref[((bb * t_len + t) * n_h + hh) * MOBA_TOP_K + r]
                for pg in range(PAGES_PER_BLOCK):
                    phys = pt_ref[bb * n_pages + blk * PAGES_PER_BLOCK + pg]
                    rows = pl.ds((r * PAGES_PER_BLOCK + pg) * PAGE_SIZE, PAGE_SIZE)
                    copies.append(pltpu.make_async_copy(ck_hbm.at[phys, :, hh, :], kg_ref.at[sl, t, rows, :], sem.at[sl, 0]))
                    copies.append(pltpu.make_async_copy(cv_hbm.at[phys, :, hh, :], vg_ref.at[sl, t, rows, :], sem.at[sl, 1]))
        return copies

    @pl.when(step == 0)
    def _():
        for cp in gather_copies(b, h, slot):
            cp.start()

    @pl.when(step + 1 < n_steps)
    def _():
        nxt = step + 1
        for cp in gather_copies(nxt // n_h, nxt % n_h, 1 - slot):
            cp.start()

    for cp in gather_copies(b, h, slot):
        cp.wait()

    q2 = jnp.concatenate([q_ref[...], q_ref[...]], axis=0).astype(BF16)
    k_own = jnp.concatenate([kn_ref[...], kn_ref[...]], axis=0).astype(BF16)
    v_own = jnp.concatenate([vn_ref[...], vn_ref[...]], axis=0).astype(BF16)
    nt_dims = (((1,), (1,)), ((), ()))
    s_own = lax.dot_general(q2, k_own, nt_dims, preferred_element_type=F32) * scale
    key_i = lax.broadcasted_iota(jnp.int32, s_own.shape, 1)
    qry_i = lax.broadcasted_iota(jnp.int32, s_own.shape, 0)
    s_own = jnp.where(jnp.logical_and(key_i < t_len, key_i <= qry_i), s_own, NEG_INF)
    rows = []
    for t in range(t_len):
        k_t = kg_ref[slot, t].astype(BF16)
        v_t = vg_ref[slot, t].astype(BF16)
        s_sel = lax.dot_general(q2, k_t, nt_dims, preferred_element_type=F32)[t:t + 1, :] * scale
        s_o = s_own[t:t + 1, :]
        m = jnp.maximum(jnp.max(s_sel, axis=1, keepdims=True), jnp.max(s_o, axis=1, keepdims=True))
        p_sel = jnp.exp(s_sel - m)
        p_o = jnp.exp(s_o - m)
        l = jnp.sum(p_sel, axis=1, keepdims=True) + jnp.sum(p_o, axis=1, keepdims=True)
        p_sel2 = jnp.broadcast_to(p_sel, (2 * t_len, n_sel)).astype(BF16)
        p_o2 = jnp.broadcast_to(p_o, (2 * t_len, 2 * t_len)).astype(BF16)
        out = (jnp.dot(p_sel2, v_t, preferred_element_type=F32) + jnp.dot(p_o2, v_own, preferred_element_type=F32))
        rows.append(out[0:1, :] / l)
    o_ref[...] = jnp.concatenate(rows, axis=0)


def _moba_sample(q, k_new, v_new, cache_k3, cache_v3, sel_flat, page_table_flat, *, n_seq, n_pages):
    m, d = q.shape
    t = m // n_seq
    n_h = d // HEAD_DIM
    n_sel = MOBA_TOP_K * MOBA_BLOCK
    row_spec = pl.BlockSpec((t, HEAD_DIM), lambda b, h, sel, pt: (b, h))
    grid_spec = pltpu.PrefetchScalarGridSpec(
        num_scalar_prefetch=2,
        grid=(n_seq, n_h),
        in_specs=[row_spec, row_spec, row_spec, pl.BlockSpec(memory_space=pl.ANY), pl.BlockSpec(memory_space=pl.ANY)],
        out_specs=row_spec,
        scratch_shapes=[
            pltpu.VMEM((2, t, n_sel, HEAD_DIM), F32),
            pltpu.VMEM((2, t, n_sel, HEAD_DIM), F32),
            pltpu.SemaphoreType.DMA((2, 2)),
        ],
    )
    return pl.pallas_call(
        functools.partial(_sample_attn_kernel, n_pages=n_pages),
        grid_spec=grid_spec,
        out_shape=jax.ShapeDtypeStruct((m, d), F32),
        compiler_params=_params(2),
        name="moba_sample",
    )(sel_flat, page_table_flat, q, k_new, v_new, cache_k3, cache_v3)


def _rope_tables(pos):
    half = HEAD_DIM // 2
    inv = 1.0 / (ROPE_THETA ** (jnp.arange(half, dtype=F32) * (2.0 / HEAD_DIM)))
    ang = pos.astype(F32)[:, None] * inv[None, :]
    cos = jnp.cos(ang)
    sin = jnp.sin(ang)
    return jnp.concatenate([cos, cos], axis=-1), jnp.concatenate([-sin, sin], axis=-1)


def kernel(x_prompt, x_sample, state_pool, cache_k, cache_v, page_table, g_pool, w_pool, s_pool, g_ffn, w_gate_up,
           w_down, g_kv, w_kv, g_attn, w_q, w_o, g_final):
    n_b, seq, d = x_prompt.shape
    n_db, t_dec, _ = x_sample.shape
    depth = g_ffn.shape[0]
    n_pool = g_pool.shape[0]
    n_h = d // HEAD_DIM
    n_pages = page_table.shape[1]
    n_past_blk = PAST_LEN // MOBA_BLOCK
    m_p = n_b * seq
    m_s = n_db * t_dec
    state_rows = state_pool.shape[2]
    assert n_pages == n_past_blk * PAGES_PER_BLOCK, "own MoBA block must hold only the new tokens"
    assert seq % MOBA_BLOCK == 0 and state_rows == POOL_HALO - 1

    tm_pool, tm_ffn, tf, tm_proj = 256, 512, 512, 512

    w_pool_b = w_pool.astype(BF16)
    w_gu_b = w_gate_up.astype(BF16)
    w_dn_b = w_down.astype(BF16)
    w_kv_b = w_kv.astype(BF16)
    w_q_b = w_q.astype(BF16)
    w_o_b = w_o.astype(BF16)

    cos_p, sin_p = _rope_tables(jnp.arange(seq))
    cos_s, sin_s = _rope_tables(PAST_LEN + jnp.arange(t_dec))
    cos_s = jnp.tile(cos_s, (n_db, 1))
    sin_s = jnp.tile(sin_s, (n_db, 1))

    pt_flat = page_table.reshape(-1)

    hp = x_prompt
    hs = x_sample
    pool_p, pool_s = [], []
    for layer in range(depth):
        last = layer == depth - 1
        if layer < n_pool:
            hp3, zt_p = _pool_layer(hp.reshape(n_b, seq, d), hp.reshape(n_b, seq, d), g_pool[layer], w_pool_b[layer],
                                    s_pool[layer], tm=tm_pool, halo_normalized=False, pos0=0)
            state = state_pool[layer]
            halo_s = jnp.pad(state, ((0, 0), (POOL_HALO - state_rows, 0), (0, 0)))
            hs3, zt_s = _pool_layer(hs.reshape(n_db, t_dec, d), halo_s, g_pool[layer], w_pool_b[layer], s_pool[layer],
                                    tm=t_dec, halo_normalized=True, pos0=PAST_LEN)
            pool_p.append(zt_p[:, POOL_HALO - state_rows:])
            pool_s.append(jnp.concatenate([state, zt_s], axis=1)[:, -state_rows:])
            hp = hp3.reshape(m_p, d)
            hs = hs3.reshape(m_s, d)
        else:
            a = layer - n_pool
            if a == 0:
                k_p, k_pb, kmean = _k_proj(hp, g_kv, w_kv_b, cos_p, sin_p, tm=tm_proj, with_blocks=True)
                v_p, vt_pb = _v_proj(hp, g_kv, w_kv_b, tm=tm_proj, with_blocks=True, seq=seq)
                (k_s,) = _k_proj(hs, g_kv, w_kv_b, cos_s, sin_s, tm=m_s, with_blocks=False)
                (v_s,) = _v_proj(hs, g_kv, w_kv_b, tm=m_s, with_blocks=False, seq=seq)
                kmean = kmean.reshape(n_b, seq // MOBA_BLOCK, d)
                bm = _cache_block_means(cache_k, pt_flat, n_seq=n_db, n_pages=n_pages)
                bm_pad = jnp.pad(bm.transpose(0, 2, 1, 3), ((0, 0), (0, 0), (0, 128 - n_past_blk), (0, 0)))
            q_p = _q_proj(hp, g_attn[a], w_q_b, a, cos_p, sin_p, tm=tm_proj)
            att_p = _moba_prompt(q_p, k_pb, vt_pb, kmean, batch=n_b, seq=seq)
            hp = _o_proj(att_p, w_o_b, a, hp, tm=tm_proj)

            q_s = _q_proj(hs, g_attn[a], w_q_b, a, cos_s, sin_s, tm=m_s)
            sel = _select_blocks(q_s, bm_pad, n_seq=n_db, n_valid=n_past_blk)
            sel_flat = sel[:, :, :n_h * MOBA_TOP_K].reshape(-1)
            att_s = _moba_sample(q_s, k_s, v_s, cache_k, cache_v, sel_flat, pt_flat, n_seq=n_db, n_pages=n_pages)
            hs = _o_proj(att_s, w_o_b, a, hs, tm=m_s)
        hp = _ffn(hp, g_ffn[layer], w_gu_b, w_dn_b, layer, g_final, tm=tm_ffn, tf=tf, final_norm=last)
        hs = _ffn(hs, g_ffn[layer], w_gu_b, w_dn_b, layer, g_final, tm=m_s, tf=tf, final_norm=last)

    y_prompt = hp.reshape(n_b, seq, d)
    y_sample = hs.reshape(n_db, t_dec, d)
    new_pool_prompt = jnp.stack(pool_p)
    new_pool_sample = jnp.stack(pool_s)
    shape_p = (n_b, seq, n_h, HEAD_DIM)
    shape_s = (n_db, t_dec, n_h, HEAD_DIM)
    return (y_prompt, y_sample, new_pool_prompt, new_pool_sample, k_p.reshape(shape_p), v_p.reshape(shape_p),
            k_s.reshape(shape_s), v_s.reshape(shape_s))
```

```python
import functools

import jax
import jax.numpy as jnp
from jax import lax
from jax.experimental import pallas as pl
from jax.experimental.pallas import tpu as pltpu

F32 = jnp.float32
BF16 = jnp.bfloat16

POOL_WINDOWS = (2, 4, 8, 16)
POOL_HALO = 16
HEAD_DIM = 128
MOBA_BLOCK = 256
MOBA_TOP_K = 3
PAGE_SIZE = 128
PAGES_PER_BLOCK = MOBA_BLOCK // PAGE_SIZE
PAST_LEN = 16384
ROPE_THETA = 10000.0
RMS_EPS = 1e-6
NEG_INF = -1e30
GATE_FLOOR = -3.0e38

VMEM_LIMIT_BYTES = 56 * 1024 * 1024


def _params(n_axes):
    return pltpu.CompilerParams(dimension_semantics=("arbitrary",) * n_axes, vmem_limit_bytes=VMEM_LIMIT_BYTES)


def _rmsnorm(x, g):
    ms = jnp.mean(x * x, axis=-1, keepdims=True)
    return x * lax.rsqrt(ms + RMS_EPS) * g


def _pool_kernel(x_ref, halo_ref, g_ref, w_ref, s_ref, h_ref, zt_ref, zs_ref, *, tm, halo_normalized, pos0):
    i = pl.program_id(1)
    d = x_ref.shape[-1]
    c = d // len(POOL_WINDOWS)
    x = x_ref[...]
    g = g_ref[...]
    z = _rmsnorm(x, g)
    if halo_normalized:
        zh = halo_ref[...]
    else:
        zh = _rmsnorm(halo_ref[...], g) * (i > 0).astype(F32)
    zs_ref[0:POOL_HALO, :] = zh
    zs_ref[POOL_HALO:POOL_HALO + tm, :] = z
    pos = lax.broadcasted_iota(jnp.int32, (tm, c), 0) + (i * tm + pos0)
    for gi, w in enumerate(POOL_WINDOWS):
        cols = slice(gi * c, (gi + 1) * c)
        zc = z[:, cols]
        tot = zc
        for j in range(1, w):
            tot = tot + zs_ref[POOL_HALO - j:POOL_HALO - j + tm, cols]
        cnt = jnp.minimum(pos + 1, w).astype(F32)
        pooled = tot / cnt - zc
        mixed = jnp.dot(pooled.astype(BF16), w_ref[gi], preferred_element_type=F32)
        h_ref[:, cols] = x[:, cols] + mixed * s_ref[:, cols]

    nt = zt_ref.shape[0]

    @pl.when(i == pl.num_programs(1) - 1)
    def _():
        zt_ref[...] = z[tm - nt:, :]


def _pool_layer(x3, halo3, g, w_b, s, *, tm, halo_normalized, pos0):
    nb, length, d = x3.shape
    nt = min(POOL_HALO, tm)
    n_i = length // tm
    hb = tm // POOL_HALO
    if halo_normalized:
        halo_map = lambda b, i: (b, 0, 0)
    else:
        halo_map = lambda b, i: (b, jnp.maximum(i * hb - 1, 0), 0)
    kern = functools.partial(_pool_kernel, tm=tm, halo_normalized=halo_normalized, pos0=pos0)
    return pl.pallas_call(
        kern,
        grid=(nb, n_i),
        in_specs=[
            pl.BlockSpec((None, tm, d), lambda b, i: (b, i, 0)),
            pl.BlockSpec((None, POOL_HALO, d), halo_map),
            pl.BlockSpec((1, d), lambda b, i: (0, 0)),
            pl.BlockSpec(w_b.shape, lambda b, i: (0, 0, 0)),
            pl.BlockSpec((1, d), lambda b, i: (0, 0)),
        ],
        out_specs=[
            pl.BlockSpec((None, tm, d), lambda b, i: (b, i, 0)),
            pl.BlockSpec((None, nt, d), lambda b, i: (b, 0, 0)),
        ],
        out_shape=[jax.ShapeDtypeStruct((nb, length, d), F32), jax.ShapeDtypeStruct((nb, nt, d), F32)],
        scratch_shapes=[pltpu.VMEM((POOL_HALO + tm, d), F32)],
        compiler_params=_params(2),
        name="pool_layer",
    )(x3, halo3, g.reshape(1, d), w_b, s.reshape(1, d))


def _ffn_kernel(x_ref, g_ref, wg_ref, wu_ref, wd_ref, go_ref, o_ref, xn_ref, *, final_norm):
    j = pl.program_id(1)

    @pl.when(j == 0)
    def _():
        x = x_ref[...]
        xn_ref[...] = _rmsnorm(x, g_ref[...]).astype(BF16)
        o_ref[...] = x

    xn = xn_ref[...]
    gate = jnp.dot(xn, wg_ref[...], preferred_element_type=F32)
    up = jnp.dot(xn, wu_ref[...], preferred_element_type=F32)
    act = (gate * jax.nn.sigmoid(gate) * up).astype(BF16)
    o_ref[...] += jnp.dot(act, wd_ref[...], preferred_element_type=F32)

    if final_norm:
        @pl.when(j == pl.num_programs(1) - 1)
        def _():
            o_ref[...] = _rmsnorm(o_ref[...], go_ref[...])


def _ffn(x, g, w_gu_b, w_dn_b, layer, g_out, *, tm, tf, final_norm):
    m, d = x.shape
    f = w_dn_b.shape[1]
    n_f = f // tf
    kern = functools.partial(_ffn_kernel, final_norm=final_norm)
    return pl.pallas_call(
        kern,
        grid=(m // tm, n_f),
        in_specs=[
            pl.BlockSpec((tm, d), lambda i, j: (i, 0)),
            pl.BlockSpec((1, d), lambda i, j: (0, 0)),
            pl.BlockSpec((None, d, tf), lambda i, j: (layer, 0, j)),
            pl.BlockSpec((None, d, tf), lambda i, j: (layer, 0, j + n_f)),
            pl.BlockSpec((None, tf, d), lambda i, j: (layer, j, 0)),
            pl.BlockSpec((1, d), lambda i, j: (0, 0)),
        ],
        out_specs=pl.BlockSpec((tm, d), lambda i, j: (i, 0)),
        out_shape=jax.ShapeDtypeStruct((m, d), F32),
        scratch_shapes=[pltpu.VMEM((tm, d), BF16)],
        compiler_params=_params(2),
        name="swiglu",
    )(x, g.reshape(1, d), w_gu_b, w_gu_b, w_dn_b, g_out.reshape(1, d))


def _rope(xh, cos, sin_signed):
    return xh * cos + pltpu.roll(xh, HEAD_DIM // 2, axis=1) * sin_signed


def _q_kernel(x_ref, g_ref, w_ref, cos_ref, sin_ref, q_ref):
    xn = _rmsnorm(x_ref[...], g_ref[...]).astype(BF16)
    acc = jnp.dot(xn, w_ref[...], preferred_element_type=F32)
    cos = cos_ref[...]
    sin = sin_ref[...]
    for h in range(acc.shape[1] // HEAD_DIM):
        cols = slice(h * HEAD_DIM, (h + 1) * HEAD_DIM)
        q_ref[:, cols] = _rope(acc[:, cols], cos, sin)


def _q_proj(x, g, w_b, layer, cos, sin, *, tm):
    m, d = x.shape
    n_pos = cos.shape[0] // tm
    return pl.pallas_call(
        _q_kernel,
        grid=(m // tm,),
        in_specs=[
            pl.BlockSpec((tm, d), lambda i: (i, 0)),
            pl.BlockSpec((1, d), lambda i: (0, 0)),
            pl.BlockSpec((None, d, d), lambda i: (layer, 0, 0)),
            pl.BlockSpec((tm, HEAD_DIM), lambda i: (i % n_pos, 0)),
            pl.BlockSpec((tm, HEAD_DIM), lambda i: (i % n_pos, 0)),
        ],
        out_specs=pl.BlockSpec((tm, d), lambda i: (i, 0)),
        out_shape=jax.ShapeDtypeStruct((m, d), F32),
        compiler_params=_params(1),
        name="q_proj",
    )(x, g.reshape(1, d), w_b, cos, sin)


def _k_kernel(x_ref, g_ref, w_ref, cos_ref, sin_ref, k_ref, *extra, with_blocks):
    xn = _rmsnorm(x_ref[...], g_ref[...]).astype(BF16)
    acc = jnp.dot(xn, w_ref[...], preferred_element_type=F32)
    cos = cos_ref[...]
    sin = sin_ref[...]
    for h in range(acc.shape[1] // HEAD_DIM):
        cols = slice(h * HEAD_DIM, (h + 1) * HEAD_DIM)
        k_ref[:, cols] = _rope(acc[:, cols], cos, sin)
    if with_blocks:
        kb_ref, km_ref = extra
        k = k_ref[...]
        kb_ref[...] = k.astype(BF16)
        for r in range(k.shape[0] // MOBA_BLOCK):
            km_ref[r] = jnp.mean(k[r * MOBA_BLOCK:(r + 1) * MOBA_BLOCK, :], axis=0, keepdims=True)


def _k_proj(x, g, w_kv_b, cos, sin, *, tm, with_blocks):
    m, d = x.shape
    n_pos = cos.shape[0] // tm
    out_specs = [pl.BlockSpec((tm, d), lambda i: (i, 0))]
    out_shape = [jax.ShapeDtypeStruct((m, d), F32)]
    if with_blocks:
        r = tm // MOBA_BLOCK
        out_specs += [pl.BlockSpec((tm, d), lambda i: (i, 0)), pl.BlockSpec((r, 1, d), lambda i: (i, 0, 0))]
        out_shape += [jax.ShapeDtypeStruct((m, d), BF16), jax.ShapeDtypeStruct((m // MOBA_BLOCK, 1, d), F32)]
    return pl.pallas_call(
        functools.partial(_k_kernel, with_blocks=with_blocks),
        grid=(m // tm,),
        in_specs=[
            pl.BlockSpec((tm, d), lambda i: (i, 0)),
            pl.BlockSpec((1, d), lambda i: (0, 0)),
            pl.BlockSpec((d, d), lambda i: (0, 0)),
            pl.BlockSpec((tm, HEAD_DIM), lambda i: (i % n_pos, 0)),
            pl.BlockSpec((tm, HEAD_DIM), lambda i: (i % n_pos, 0)),
        ],
        out_specs=out_specs,
        out_shape=out_shape,
        compiler_params=_params(1),
        name="k_proj",
    )(x, g.reshape(1, d), w_kv_b, cos, sin)


def _v_kernel(x_ref, g_ref, w_ref, v_ref, *extra, with_blocks):
    xn = _rmsnorm(x_ref[...], g_ref[...]).astype(BF16)
    v = jnp.dot(xn, w_ref[...], preferred_element_type=F32)
    v_ref[...] = v
    if with_blocks:
        (vt_ref,) = extra
        for h in range(v.shape[1] // HEAD_DIM):
            for r in range(v.shape[0] // MOBA_BLOCK):
                blk = v[r * MOBA_BLOCK:(r + 1) * MOBA_BLOCK, h * HEAD_DIM:(h + 1) * HEAD_DIM]
                vt_ref[h, r] = blk.T.astype(BF16)


def _v_proj(x, g, w_kv_b, *, tm, with_blocks, seq):
    m, d = x.shape
    n_h = d // HEAD_DIM
    out_specs = [pl.BlockSpec((tm, d), lambda i: (i, 0))]
    out_shape = [jax.ShapeDtypeStruct((m, d), F32)]
    if with_blocks:
        r = tm // MOBA_BLOCK
        tiles_per_seq = seq // tm
        out_specs.append(pl.BlockSpec((None, n_h, r, HEAD_DIM, MOBA_BLOCK),
                                      lambda i: (i // tiles_per_seq, 0, i % tiles_per_seq, 0, 0)))
        out_shape.append(jax.ShapeDtypeStruct((m // seq, n_h, seq // MOBA_BLOCK, HEAD_DIM, MOBA_BLOCK), BF16))
    return pl.pallas_call(
        functools.partial(_v_kernel, with_blocks=with_blocks),
        grid=(m // tm,),
        in_specs=[
            pl.BlockSpec((tm, d), lambda i: (i, 0)),
            pl.BlockSpec((1, d), lambda i: (0, 0)),
            pl.BlockSpec((d, d), lambda i: (0, 1)),
        ],
        out_specs=out_specs,
        out_shape=out_shape,
        compiler_params=_params(1),
        name="v_proj",
    )(x, g.reshape(1, d), w_kv_b)


def _o_kernel(a_ref, w_ref, h_ref, o_ref):
    o_ref[...] = h_ref[...] + jnp.dot(a_ref[...].astype(BF16), w_ref[...], preferred_element_type=F32)


def _o_proj(a, w_b, layer, h, *, tm):
    m, d = h.shape
    return pl.pallas_call(
        _o_kernel,
        grid=(m // tm,),
        in_specs=[
            pl.BlockSpec((tm, d), lambda i: (i, 0)),
            pl.BlockSpec((None, d, d), lambda i: (layer, 0, 0)),
            pl.BlockSpec((tm, d), lambda i: (i, 0)),
        ],
        out_specs=pl.BlockSpec((tm, d), lambda i: (i, 0)),
        out_shape=jax.ShapeDtypeStruct((m, d), F32),
        compiler_params=_params(1),
        name="o_proj",
    )(a, w_b, h)


def _top_k_picks(gate, candidate, blk_f, axis):
    remaining = candidate
    for _ in range(MOBA_TOP_K):
        gm = jnp.where(remaining, gate, GATE_FLOOR)
        top = jnp.max(gm, axis=axis, keepdims=True)
        hit = jnp.logical_and(remaining, gm == top)
        idx = jnp.min(jnp.where(hit, blk_f, 1e9), axis=axis, keepdims=True)
        pick = blk_f == idx
        remaining = jnp.logical_and(remaining, jnp.logical_not(pick))
        yield pick, idx


def _attn_kernel(q_ref, k_ref, vt_ref, km_ref, o_ref, qt_ref, bias_ref, acc_ref):
    n_g, n_blk = qt_ref.shape[0], qt_ref.shape[1]
    tq = MOBA_BLOCK
    q_scale = HEAD_DIM ** -0.5 * 1.4426950408889634
    blk_i = lax.broadcasted_iota(jnp.int32, (n_blk, tq), 0)
    blk_f = blk_i.astype(F32)
    heads = [slice(g * HEAD_DIM, (g + 1) * HEAD_DIM) for g in range(n_g)]

    for g in range(n_g):
        km = km_ref[:, heads[g]]
        for i in range(n_blk):
            q_t = q_ref[i * tq:(i + 1) * tq, heads[g]].T
            qt_ref[g, i] = (q_t * q_scale).astype(BF16)
            if i <= MOBA_TOP_K:
                selected = blk_i < i
            else:
                gate_t = jnp.dot(km, q_t, preferred_element_type=F32, precision=lax.Precision.HIGHEST)
                selected = jnp.zeros((n_blk, tq), jnp.bool_)
                for pick, _ in _top_k_picks(gate_t, blk_i < i, blk_f, 0):
                    selected = jnp.logical_or(selected, pick)
            bias_ref[g, i] = jnp.where(selected, 0.0, NEG_INF)

    kpos = lax.broadcasted_iota(jnp.int32, (tq, tq), 0)
    qpos = lax.broadcasted_iota(jnp.int32, (tq, tq), 1)
    causal = kpos <= qpos

    def tile(i, _):
        start = pl.multiple_of(i * tq, tq)
        ms, ls = [], []
        s_own = [jnp.dot(k_ref[pl.ds(start, tq), heads[g]], qt_ref[g, i], preferred_element_type=F32)
                 for g in range(n_g)]
        p_own = []
        for g in range(n_g):
            s = jnp.where(causal, s_own[g], NEG_INF)
            m0 = jnp.max(s, axis=0, keepdims=True)
            p = jnp.exp2(s - m0)
            ms.append(m0)
            ls.append(jnp.sum(p, axis=0, keepdims=True))
            p_own.append(p.astype(BF16))
        for g in range(n_g):
            acc_ref[g] = jnp.dot(vt_ref[g, i], p_own[g], preferred_element_type=F32)

        def pair(jj, carry):
            ms, ls = carry
            j0 = 2 * jj
            st = pl.multiple_of(j0 * tq, 2 * tq)
            s2 = [jnp.dot(k_ref[pl.ds(st, 2 * tq), heads[g]], qt_ref[g, i], preferred_element_type=F32)
                  for g in range(n_g)]
            new_ms, new_ls, alphas, ps = [], [], [], []
            for g in range(n_g):
                sa = s2[g][:tq] + bias_ref[g, i, pl.ds(j0, 1), :]
                sb = s2[g][tq:] + bias_ref[g, i, pl.ds(j0 + 1, 1), :]
                m_blk = jnp.maximum(jnp.max(sa, axis=0, keepdims=True), jnp.max(sb, axis=0, keepdims=True))
                m_new = jnp.maximum(ms[g], m_blk)
                alpha = jnp.exp2(ms[g] - m_new)
                pa = jnp.exp2(sa - m_new)
                pb = jnp.exp2(sb - m_new)
                new_ms.append(m_new)
                new_ls.append(alpha * ls[g] + (jnp.sum(pa, axis=0, keepdims=True) + jnp.sum(pb, axis=0, keepdims=True)))
                alphas.append(alpha)
                ps.append((pa.astype(BF16), pb.astype(BF16)))
            for g in range(n_g):
                pv = (jnp.dot(vt_ref[g, j0], ps[g][0], preferred_element_type=F32)
                      + jnp.dot(vt_ref[g, j0 + 1], ps[g][1], preferred_element_type=F32))
                acc_ref[g] = alphas[g] * acc_ref[g] + pv
            return tuple(new_ms), tuple(new_ls)

        _, ls = lax.fori_loop(0, (i + 1) // 2, pair, (tuple(ms), tuple(ls)))
        for g in range(n_g):
            o_ref[pl.ds(start, tq), heads[g]] = (acc_ref[g] * (1.0 / ls[g])).T.astype(o_ref.dtype)
        return 0

    lax.fori_loop(0, n_blk, tile, 0)


ATTN_HEADS_PER_STEP = 4


def _moba_prompt(q, k_b, vt_b, kmean, *, batch, seq):
    m, d = q.shape
    n_g = ATTN_HEADS_PER_STEP
    n_blk = seq // MOBA_BLOCK
    width = n_g * HEAD_DIM
    return pl.pallas_call(
        _attn_kernel,
        grid=(batch, d // width),
        in_specs=[
            pl.BlockSpec((seq, width), lambda b, h: (b, h)),
            pl.BlockSpec((seq, width), lambda b, h: (b, h)),
            pl.BlockSpec((None, n_g, n_blk, HEAD_DIM, MOBA_BLOCK), lambda b, h: (b, h, 0, 0, 0)),
            pl.BlockSpec((None, n_blk, width), lambda b, h: (b, 0, h)),
        ],
        out_specs=pl.BlockSpec((seq, width), lambda b, h: (b, h)),
        out_shape=jax.ShapeDtypeStruct((m, d), BF16),
        scratch_shapes=[
            pltpu.VMEM((n_g, n_blk, HEAD_DIM, MOBA_BLOCK), BF16),
            pltpu.VMEM((n_g, n_blk, n_blk, MOBA_BLOCK), F32),
            pltpu.VMEM((n_g, HEAD_DIM, MOBA_BLOCK), F32),
        ],
        compiler_params=_params(2),
        name="moba_prompt",
    )(q, k_b, vt_b, kmean)


MEAN_BLOCKS_PER_STEP = 4


def _block_mean_kernel(pt_ref, *refs):
    del pt_ref
    page_refs, o_ref = refs[:-1], refs[-1]
    for n in range(MEAN_BLOCKS_PER_STEP):
        m0 = jnp.mean(page_refs[PAGES_PER_BLOCK * n][...], axis=0)
        m1 = jnp.mean(page_refs[PAGES_PER_BLOCK * n + 1][...], axis=0)
        o_ref[n] = (m0 + m1) * 0.5


def _cache_block_means(cache_k, page_table_flat, *, n_seq, n_pages):
    _, page, n_h, dh = cache_k.shape
    n_blk = n_pages // PAGES_PER_BLOCK
    pages_per_step = MEAN_BLOCKS_PER_STEP * PAGES_PER_BLOCK

    def page_spec(p):
        return pl.BlockSpec((None, page, n_h, dh), lambda b, n, pt: (pt[b * n_pages + pages_per_step * n + p], 0, 0, 0))

    grid_spec = pltpu.PrefetchScalarGridSpec(
        num_scalar_prefetch=1,
        grid=(n_seq, n_blk // MEAN_BLOCKS_PER_STEP),
        in_specs=[page_spec(p) for p in range(pages_per_step)],
        out_specs=pl.BlockSpec((None, MEAN_BLOCKS_PER_STEP, n_h, dh), lambda b, n, pt: (b, n, 0, 0)),
    )
    return pl.pallas_call(
        _block_mean_kernel,
        grid_spec=grid_spec,
        out_shape=jax.ShapeDtypeStruct((n_seq, n_blk, n_h, dh), F32),
        compiler_params=_params(2),
        name="cache_block_means",
    )(page_table_flat, *([cache_k] * pages_per_step))


def _select_kernel(q_ref, bm_ref, sel_ref, *, n_valid):
    q = q_ref[...]
    t = q.shape[0]
    n_lane = bm_ref.shape[1]
    lane = lax.broadcasted_iota(jnp.int32, (t, n_lane), 1)
    lane_f = lane.astype(F32)
    out = jnp.zeros((t, n_lane), jnp.int32)
    for h in range(q.shape[1] // HEAD_DIM):
        cols = slice(h * HEAD_DIM, (h + 1) * HEAD_DIM)
        gate = lax.dot_general(q[:, cols], bm_ref[h], (((1,), (1,)), ((), ())),
                               preferred_element_type=F32, precision=lax.Precision.HIGHEST)
        for r, (_, idx) in enumerate(_top_k_picks(gate, lane < n_valid, lane_f, 1)):
            out = jnp.where(lane == h * MOBA_TOP_K + r, idx.astype(jnp.int32), out)
    sel_ref[...] = out


def _select_blocks(q, bm_pad, *, n_seq, n_valid):
    m, d = q.shape
    t = m // n_seq
    _, n_h, n_lane, dh = bm_pad.shape
    return pl.pallas_call(
        functools.partial(_select_kernel, n_valid=n_valid),
        grid=(n_seq,),
        in_specs=[
            pl.BlockSpec((t, d), lambda b: (b, 0)),
            pl.BlockSpec((None, n_h, n_lane, dh), lambda b: (b, 0, 0, 0)),
        ],
        out_specs=pl.BlockSpec((None, t, n_lane), lambda b: (b, 0, 0)),
        out_shape=jax.ShapeDtypeStruct((n_seq, t, n_lane), jnp.int32),
        compiler_params=_params(1),
        name="select_blocks",
    )(q, bm_pad)


def _sample_attn_kernel(sel_ref, pt_ref, q_ref, kn_ref, vn_ref, ck_hbm, cv_hbm, o_ref, kg_ref, vg_ref, sem,
                        *, n_pages):
    b = pl.program_id(0)
    h = pl.program_id(1)
    n_h = pl.num_programs(1)
    n_steps = pl.num_programs(0) * n_h
    step = b * n_h + h
    slot = step % 2
    t_len = q_ref.shape[0]
    n_sel = MOBA_TOP_K * MOBA_BLOCK
    scale = HEAD_DIM ** -0.5

    def gather_copies(bb, hh, sl):
        copies = []
        for t in range(t_len):
            for r in range(MOBA_TOP_K):
                blk = sel_ref[((bb * t_len + t) * n_h + hh) * MOBA_TOP_K + r]
                for pg in range(PAGES_PER_BLOCK):
                    phys = pt_ref[bb * n_pages + blk * PAGES_PER_BLOCK + pg]
                    rows = pl.ds((r * PAGES_PER_BLOCK + pg) * PAGE_SIZE, PAGE_SIZE)
                    copies.append(pltpu.make_async_copy(ck_hbm.at[phys, :, hh, :], kg_ref.at[sl, t, rows, :], sem.at[sl, 0]))
                    copies.append(pltpu.make_async_copy(cv_hbm.at[phys, :, hh, :], vg_ref.at[sl, t, rows, :], sem.at[sl, 1]))
        return copies

    @pl.when(step == 0)
    def _():
        for cp in gather_copies(b, h, slot):
            cp.start()

    @pl.when(step + 1 < n_steps)
    def _():
        nxt = step + 1
        for cp in gather_copies(nxt // n_h, nxt % n_h, 1 - slot):
            cp.start()

    for cp in gather_copies(b, h, slot):
        cp.wait()

    q2 = jnp.concatenate([q_ref[...], q_ref[...]], axis=0).astype(BF16)
    k_own = jnp.concatenate([kn_ref[...], kn_ref[...]], axis=0).astype(BF16)
    v_own = jnp.concatenate([vn_ref[...], vn_ref[...]], axis=0).astype(BF16)
    nt_dims = (((1,), (1,)), ((), ()))
    s_own = lax.dot_general(q2, k_own, nt_dims, preferred_element_type=F32) * scale
    key_i = lax.broadcasted_iota(jnp.int32, s_own.shape, 1)
    qry_i = lax.broadcasted_iota(jnp.int32, s_own.shape, 0)
    s_own = jnp.where(jnp.logical_and(key_i < t_len, key_i <= qry_i), s_own, NEG_INF)
    rows = []
    for t in range(t_len):
        k_t = kg_ref[slot, t].astype(BF16)
        v_t = vg_ref[slot, t].astype(BF16)
        s_sel = lax.dot_general(q2, k_t, nt_dims, preferred_element_type=F32)[t:t + 1, :] * scale
        s_o = s_own[t:t + 1, :]
        m = jnp.maximum(jnp.max(s_sel, axis=1, keepdims=True), jnp.max(s_o, axis=1, keepdims=True))
        p_sel = jnp.exp(s_sel - m)
        p_o = jnp.exp(s_o - m)
        l = jnp.sum(p_sel, axis=1, keepdims=True) + jnp.sum(p_o, axis=1, keepdims=True)
        p_sel2 = jnp.broadcast_to(p_sel, (2 * t_len, n_sel)).astype(BF16)
        p_o2 = jnp.broadcast_to(p_o, (2 * t_len, 2 * t_len)).astype(BF16)
        out = (jnp.dot(p_sel2, v_t, preferred_element_type=F32) + jnp.dot(p_o2, v_own, preferred_element_type=F32))
        rows.append(out[0:1, :] / l)
    o_ref[...] = jnp.concatenate(rows, axis=0)


def _moba_sample(q, k_new, v_new, cache_k3, cache_v3, sel_flat, page_table_flat, *, n_seq, n_pages):
    m, d = q.shape
    t = m // n_seq
    n_h = d // HEAD_DIM
    n_sel = MOBA_TOP_K * MOBA_BLOCK
    row_spec = pl.BlockSpec((t, HEAD_DIM), lambda b, h, sel, pt: (b, h))
    grid_spec = pltpu.PrefetchScalarGridSpec(
        num_scalar_prefetch=2,
        grid=(n_seq, n_h),
        in_specs=[row_spec, row_spec, row_spec, pl.BlockSpec(memory_space=pl.ANY), pl.BlockSpec(memory_space=pl.ANY)],
        out_specs=row_spec,
        scratch_shapes=[
            pltpu.VMEM((2, t, n_sel, HEAD_DIM), F32),
            pltpu.VMEM((2, t, n_sel, HEAD_DIM), F32),
            pltpu.SemaphoreType.DMA((2, 2)),
        ],
    )
    return pl.pallas_call(
        functools.partial(_sample_attn_kernel, n_pages=n_pages),
        grid_spec=grid_spec,
        out_shape=jax.ShapeDtypeStruct((m, d), F32),
        compiler_params=_params(2),
        name="moba_sample",
    )(sel_flat, page_table_flat, q, k_new, v_new, cache_k3, cache_v3)


def _rope_tables(pos):
    half = HEAD_DIM // 2
    inv = 1.0 / (ROPE_THETA ** (jnp.arange(half, dtype=F32) * (2.0 / HEAD_DIM)))
    ang = pos.astype(F32)[:, None] * inv[None, :]
    cos = jnp.cos(ang)
    sin = jnp.sin(ang)
    return jnp.concatenate([cos, cos], axis=-1), jnp.concatenate([-sin, sin], axis=-1)


def kernel(x_prompt, x_sample, state_pool, cache_k, cache_v, page_table, g_pool, w_pool, s_pool, g_ffn, w_gate_up,
           w_down, g_kv, w_kv, g_attn, w_q, w_o, g_final):
    n_b, seq, d = x_prompt.shape
    n_db, t_dec, _ = x_sample.shape
    depth = g_ffn.shape[0]
    n_pool = g_pool.shape[0]
    n_h = d // HEAD_DIM
    n_pages = page_table.shape[1]
    n_past_blk = PAST_LEN // MOBA_BLOCK
    m_p = n_b * seq
    m_s = n_db * t_dec
    state_rows = state_pool.shape[2]
    assert n_pages == n_past_blk * PAGES_PER_BLOCK, "own MoBA block must hold only the new tokens"
    assert seq % MOBA_BLOCK == 0 and state_rows == POOL_HALO - 1

    tm_pool, tm_ffn, tf, tm_proj = 256, 512, 512, 512

    w_pool_b = w_pool.astype(BF16)
    w_gu_b = w_gate_up.astype(BF16)
    w_dn_b = w_down.astype(BF16)
    w_kv_b = w_kv.astype(BF16)
    w_q_b = w_q.astype(BF16)
    w_o_b = w_o.astype(BF16)

    cos_p, sin_p = _rope_tables(jnp.arange(seq))
    cos_s, sin_s = _rope_tables(PAST_LEN + jnp.arange(t_dec))
    cos_s = jnp.tile(cos_s, (n_db, 1))
    sin_s = jnp.tile(sin_s, (n_db, 1))

    pt_flat = page_table.reshape(-1)

    hp = x_prompt
    hs = x_sample
    pool_p, pool_s = [], []
    for layer in range(depth):
        last = layer == depth - 1
        if layer < n_pool:
            hp3, zt_p = _pool_layer(hp.reshape(n_b, seq, d), hp.reshape(n_b, seq, d), g_pool[layer], w_pool_b[layer],
                                    s_pool[layer], tm=tm_pool, halo_normalized=False, pos0=0)
            state = state_pool[layer]
            halo_s = jnp.pad(state, ((0, 0), (POOL_HALO - state_rows, 0), (0, 0)))
            hs3, zt_s = _pool_layer(hs.reshape(n_db, t_dec, d), halo_s, g_pool[layer], w_pool_b[layer], s_pool[layer],
                                    tm=t_dec, halo_normalized=True, pos0=PAST_LEN)
            pool_p.append(zt_p[:, POOL_HALO - state_rows:])
            pool_s.append(jnp.concatenate([state, zt_s], axis=1)[:, -state_rows:])
            hp = hp3.reshape(m_p, d)
            hs = hs3.reshape(m_s, d)
        else:
            a = layer - n_pool
            if a == 0:
                k_p, k_pb, kmean = _k_proj(hp, g_kv, w_kv_b, cos_p, sin_p, tm=tm_proj, with_blocks=True)
                v_p, vt_pb = _v_proj(hp, g_kv, w_kv_b, tm=tm_proj, with_blocks=True, seq=seq)
                (k_s,) = _k_proj(hs, g_kv, w_kv_b, cos_s, sin_s, tm=m_s, with_blocks=False)
                (v_s,) = _v_proj(hs, g_kv, w_kv_b, tm=m_s, with_blocks=False, seq=seq)
                kmean = kmean.reshape(n_b, seq // MOBA_BLOCK, d)
                bm = _cache_block_means(cache_k, pt_flat, n_seq=n_db, n_pages=n_pages)
                bm_pad = jnp.pad(bm.transpose(0, 2, 1, 3), ((0, 0), (0, 0), (0, 128 - n_past_blk), (0, 0)))
            q_p = _q_proj(hp, g_attn[a], w_q_b, a, cos_p, sin_p, tm=tm_proj)
            att_p = _moba_prompt(q_p, k_pb, vt_pb, kmean, batch=n_b, seq=seq)
            hp = _o_proj(att_p, w_o_b, a, hp, tm=tm_proj)

            q_s = _q_proj(hs, g_attn[a], w_q_b, a, cos_s, sin_s, tm=m_s)
            sel = _select_blocks(q_s, bm_pad, n_seq=n_db, n_valid=n_past_blk)
            sel_flat = sel[:, :, :n_h * MOBA_TOP_K].reshape(-1)
            att_s = _moba_sample(q_s, k_s, v_s, cache_k, cache_v, sel_flat, pt_flat, n_seq=n_db, n_pages=n_pages)
            hs = _o_proj(att_s, w_o_b, a, hs, tm=m_s)
        hp = _ffn(hp, g_ffn[layer], w_gu_b, w_dn_b, layer, g_final, tm=tm_ffn, tf=tf, final_norm=last)
        hs = _ffn(hs, g_ffn[layer], w_gu_b, w_dn_b, layer, g_final, tm=m_s, tf=tf, final_norm=last)

    y_prompt = hp.reshape(n_b, seq, d)
    y_sample = hs.reshape(n_db, t_dec, d)
    new_pool_prompt = jnp.stack(pool_p)
    new_pool_sample = jnp.stack(pool_s)
    shape_p = (n_b, seq, n_h, HEAD_DIM)
    shape_s = (n_db, t_dec, n_h, HEAD_DIM)
    return (y_prompt, y_sample, new_pool_prompt, new_pool_sample, k_p.reshape(shape_p), v_p.reshape(shape_p),
            k_s.reshape(shape_s), v_s.reshape(shape_s))
```

```python
import functools

import jax
import jax.numpy as jnp
from jax import lax
from jax.experimental import pallas as pl
from jax.experimental.pallas import tpu as pltpu

F32 = jnp.float32
BF16 = jnp.bfloat16

POOL_WINDOWS = (2, 4, 8, 16)
POOL_HALO = 16
HEAD_DIM = 128
MOBA_BLOCK = 256
MOBA_TOP_K = 3
PAGE_SIZE = 128
PAGES_PER_BLOCK = MOBA_BLOCK // PAGE_SIZE
PAST_LEN = 16384
ROPE_THETA = 10000.0
RMS_EPS = 1e-6
NEG_INF = -1e30
GATE_FLOOR = -3.0e38

VMEM_LIMIT_BYTES = 56 * 1024 * 1024


def _params(n_axes):
    return pltpu.CompilerParams(dimension_semantics=("arbitrary",) * n_axes, vmem_limit_bytes=VMEM_LIMIT_BYTES)


def _rmsnorm(x, g):
    ms = jnp.mean(x * x, axis=-1, keepdims=True)
    return x * lax.rsqrt(ms + RMS_EPS) * g


def _pool_kernel(x_ref, halo_ref, g_ref, w_ref, s_ref, h_ref, zt_ref, zs_ref, *, tm, halo_normalized, pos0):
    i = pl.program_id(1)
    d = x_ref.shape[-1]
    c = d // len(POOL_WINDOWS)
    x = x_ref[...]
    g = g_ref[...]
    z = _rmsnorm(x, g)
    if halo_normalized:
        zh = halo_ref[...]
    else:
        zh = _rmsnorm(halo_ref[...], g) * (i > 0).astype(F32)
    zs_ref[0:POOL_HALO, :] = zh
    zs_ref[POOL_HALO:POOL_HALO + tm, :] = z
    pos = lax.broadcasted_iota(jnp.int32, (tm, c), 0) + (i * tm + pos0)
    for gi, w in enumerate(POOL_WINDOWS):
        cols = slice(gi * c, (gi + 1) * c)
        zc = z[:, cols]
        tot = zc
        for j in range(1, w):
            tot = tot + zs_ref[POOL_HALO - j:POOL_HALO - j + tm, cols]
        cnt = jnp.minimum(pos + 1, w).astype(F32)
        pooled = tot / cnt - zc
        mixed = jnp.dot(pooled.astype(BF16), w_ref[gi], preferred_element_type=F32)
        h_ref[:, cols] = x[:, cols] + mixed * s_ref[:, cols]

    nt = zt_ref.shape[0]

    @pl.when(i == pl.num_programs(1) - 1)
    def _():
        zt_ref[...] = z[tm - nt:, :]


def _pool_layer(x3, halo3, g, w_b, s, *, tm, halo_normalized, pos0):
    nb, length, d = x3.shape
    nt = min(POOL_HALO, tm)
    n_i = length // tm
    hb = tm // POOL_HALO
    if halo_normalized:
        halo_map = lambda b, i: (b, 0, 0)
    else:
        halo_map = lambda b, i: (b, jnp.maximum(i * hb - 1, 0), 0)
    kern = functools.partial(_pool_kernel, tm=tm, halo_normalized=halo_normalized, pos0=pos0)
    return pl.pallas_call(
        kern,
        grid=(nb, n_i),
        in_specs=[
            pl.BlockSpec((None, tm, d), lambda b, i: (b, i, 0)),
            pl.BlockSpec((None, POOL_HALO, d), halo_map),
            pl.BlockSpec((1, d), lambda b, i: (0, 0)),
            pl.BlockSpec(w_b.shape, lambda b, i: (0, 0, 0)),
            pl.BlockSpec((1, d), lambda b, i: (0, 0)),
        ],
        out_specs=[
            pl.BlockSpec((None, tm, d), lambda b, i: (b, i, 0)),
            pl.BlockSpec((None, nt, d), lambda b, i: (b, 0, 0)),
        ],
        out_shape=[jax.ShapeDtypeStruct((nb, length, d), F32), jax.ShapeDtypeStruct((nb, nt, d), F32)],
        scratch_shapes=[pltpu.VMEM((POOL_HALO + tm, d), F32)],
        compiler_params=_params(2),
        name="pool_layer",
    )(x3, halo3, g.reshape(1, d), w_b, s.reshape(1, d))


def _ffn_kernel(x_ref, g_ref, wg_ref, wu_ref, wd_ref, go_ref, o_ref, xn_ref, *, final_norm):
    j = pl.program_id(1)

    @pl.when(j == 0)
    def _():
        x = x_ref[...]
        xn_ref[...] = _rmsnorm(x, g_ref[...]).astype(BF16)
        o_ref[...] = x

    xn = xn_ref[...]
    gate = jnp.dot(xn, wg_ref[...], preferred_element_type=F32)
    up = jnp.dot(xn, wu_ref[...], preferred_element_type=F32)
    act = (gate * jax.nn.sigmoid(gate) * up).astype(BF16)
    o_ref[...] += jnp.dot(act, wd_ref[...], preferred_element_type=F32)

    if final_norm:
        @pl.when(j == pl.num_programs(1) - 1)
        def _():
            o_ref[...] = _rmsnorm(o_ref[...], go_ref[...])


def _ffn(x, g, w_gu_b, w_dn_b, layer, g_out, *, tm, tf, final_norm):
    m, d = x.shape
    f = w_dn_b.shape[1]
    n_f = f // tf
    kern = functools.partial(_ffn_kernel, final_norm=final_norm)
    return pl.pallas_call(
        kern,
        grid=(m // tm, n_f),
        in_specs=[
            pl.BlockSpec((tm, d), lambda i, j: (i, 0)),
            pl.BlockSpec((1, d), lambda i, j: (0, 0)),
            pl.BlockSpec((None, d, tf), lambda i, j: (layer, 0, j)),
            pl.BlockSpec((None, d, tf), lambda i, j: (layer, 0, j + n_f)),
            pl.BlockSpec((None, tf, d), lambda i, j: (layer, j, 0)),
            pl.BlockSpec((1, d), lambda i, j: (0, 0)),
        ],
        out_specs=pl.BlockSpec((tm, d), lambda i, j: (i, 0)),
        out_shape=jax.ShapeDtypeStruct((m, d), F32),
        scratch_shapes=[pltpu.VMEM((tm, d), BF16)],
        compiler_params=_params(2),
        name="swiglu",
    )(x, g.reshape(1, d), w_gu_b, w_gu_b, w_dn_b, g_out.reshape(1, d))


def _rope(xh, cos, sin_signed):
    return xh * cos + pltpu.roll(xh, HEAD_DIM // 2, axis=1) * sin_signed


def _q_kernel(x_ref, g_ref, w_ref, cos_ref, sin_ref, q_ref):
    xn = _rmsnorm(x_ref[...], g_ref[...]).astype(BF16)
    acc = jnp.dot(xn, w_ref[...], preferred_element_type=F32)
    cos = cos_ref[...]
    sin = sin_ref[...]
    for h in range(acc.shape[1] // HEAD_DIM):
        cols = slice(h * HEAD_DIM, (h + 1) * HEAD_DIM)
        q_ref[:, cols] = _rope(acc[:, cols], cos, sin)


def _q_proj(x, g, w_b, layer, cos, sin, *, tm):
    m, d = x.shape
    n_pos = cos.shape[0] // tm
    return pl.pallas_call(
        _q_kernel,
        grid=(m // tm,),
        in_specs=[
            pl.BlockSpec((tm, d), lambda i: (i, 0)),
            pl.BlockSpec((1, d), lambda i: (0, 0)),
            pl.BlockSpec((None, d, d), lambda i: (layer, 0, 0)),
            pl.BlockSpec((tm, HEAD_DIM), lambda i: (i % n_pos, 0)),
            pl.BlockSpec((tm, HEAD_DIM), lambda i: (i % n_pos, 0)),
        ],
        out_specs=pl.BlockSpec((tm, d), lambda i: (i, 0)),
        out_shape=jax.ShapeDtypeStruct((m, d), F32),
        compiler_params=_params(1),
        name="q_proj",
    )(x, g.reshape(1, d), w_b, cos, sin)


def _k_kernel(x_ref, g_ref, w_ref, cos_ref, sin_ref, k_ref, *extra, with_blocks, tiles_per_seq):
    xn = _rmsnorm(x_ref[...], g_ref[...]).astype(BF16)
    acc = jnp.dot(xn, w_ref[...], preferred_element_type=F32)
    cos = cos_ref[...]
    sin = sin_ref[...]
    n_h = acc.shape[1] // HEAD_DIM
    for h in range(n_h):
        cols = slice(h * HEAD_DIM, (h + 1) * HEAD_DIM)
        k_ref[:, cols] = _rope(acc[:, cols], cos, sin)
    if with_blocks:
        kb_ref, km_ref = extra
        k = k_ref[...]
        tm = k.shape[0]
        lane = lax.broadcasted_iota(jnp.int32, (MOBA_BLOCK, HEAD_DIM), 1)
        first_blk = (pl.program_id(0) % tiles_per_seq) * (tm // MOBA_BLOCK)
        for h in range(n_h):
            kb_ref[:, 2 * h * HEAD_DIM:(2 * h + 1) * HEAD_DIM] = k[:, h * HEAD_DIM:(h + 1) * HEAD_DIM].astype(BF16)
        for r in range(tm // MOBA_BLOCK):
            rows = slice(r * MOBA_BLOCK, (r + 1) * MOBA_BLOCK)
            onehot = jnp.where(lane == first_blk + r, 1.0, 0.0).astype(BF16)
            for h in range(n_h):
                kb_ref[rows, (2 * h + 1) * HEAD_DIM:(2 * h + 2) * HEAD_DIM] = onehot
            km_ref[r] = jnp.mean(k[rows, :], axis=0, keepdims=True)


def _k_proj(x, g, w_kv_b, cos, sin, *, tm, with_blocks):
    m, d = x.shape
    n_pos = cos.shape[0] // tm
    out_specs = [pl.BlockSpec((tm, d), lambda i: (i, 0))]
    out_shape = [jax.ShapeDtypeStruct((m, d), F32)]
    if with_blocks:
        r = tm // MOBA_BLOCK
        out_specs += [pl.BlockSpec((tm, 2 * d), lambda i: (i, 0)), pl.BlockSpec((r, 1, d), lambda i: (i, 0, 0))]
        out_shape += [jax.ShapeDtypeStruct((m, 2 * d), BF16), jax.ShapeDtypeStruct((m // MOBA_BLOCK, 1, d), F32)]
    return pl.pallas_call(
        functools.partial(_k_kernel, with_blocks=with_blocks, tiles_per_seq=n_pos),
        grid=(m // tm,),
        in_specs=[
            pl.BlockSpec((tm, d), lambda i: (i, 0)),
            pl.BlockSpec((1, d), lambda i: (0, 0)),
            pl.BlockSpec((d, d), lambda i: (0, 0)),
            pl.BlockSpec((tm, HEAD_DIM), lambda i: (i % n_pos, 0)),
            pl.BlockSpec((tm, HEAD_DIM), lambda i: (i % n_pos, 0)),
        ],
        out_specs=out_specs,
        out_shape=out_shape,
        compiler_params=_params(1),
        name="k_proj",
    )(x, g.reshape(1, d), w_kv_b, cos, sin)


def _v_kernel(x_ref, g_ref, w_ref, v_ref, *extra, with_blocks):
    xn = _rmsnorm(x_ref[...], g_ref[...]).astype(BF16)
    v = jnp.dot(xn, w_ref[...], preferred_element_type=F32)
    v_ref[...] = v
    if with_blocks:
        (vt_ref,) = extra
        extra_rows = vt_ref.shape[2] - HEAD_DIM
        row = lax.broadcasted_iota(jnp.int32, (extra_rows, MOBA_BLOCK), 0)
        ones_row = jnp.where(row == 0, 1.0, 0.0).astype(BF16)
        for h in range(v.shape[1] // HEAD_DIM):
            for r in range(v.shape[0] // MOBA_BLOCK):
                blk = v[r * MOBA_BLOCK:(r + 1) * MOBA_BLOCK, h * HEAD_DIM:(h + 1) * HEAD_DIM]
                vt_ref[h, r, 0:HEAD_DIM, :] = blk.T.astype(BF16)
                vt_ref[h, r, HEAD_DIM:, :] = ones_row


VT_ROWS = HEAD_DIM + 16


def _v_proj(x, g, w_kv_b, *, tm, with_blocks, seq):
    m, d = x.shape
    n_h = d // HEAD_DIM
    out_specs = [pl.BlockSpec((tm, d), lambda i: (i, 0))]
    out_shape = [jax.ShapeDtypeStruct((m, d), F32)]
    if with_blocks:
        r = tm // MOBA_BLOCK
        tiles_per_seq = seq // tm
        out_specs.append(pl.BlockSpec((None, n_h, r, VT_ROWS, MOBA_BLOCK),
                                      lambda i: (i // tiles_per_seq, 0, i % tiles_per_seq, 0, 0)))
        out_shape.append(jax.ShapeDtypeStruct((m // seq, n_h, seq // MOBA_BLOCK, VT_ROWS, MOBA_BLOCK), BF16))
    return pl.pallas_call(
        functools.partial(_v_kernel, with_blocks=with_blocks),
        grid=(m // tm,),
        in_specs=[
            pl.BlockSpec((tm, d), lambda i: (i, 0)),
            pl.BlockSpec((1, d), lambda i: (0, 0)),
            pl.BlockSpec((d, d), lambda i: (0, 1)),
        ],
        out_specs=out_specs,
        out_shape=out_shape,
        compiler_params=_params(1),
        name="v_proj",
    )(x, g.reshape(1, d), w_kv_b)


def _o_kernel(a_ref, w_ref, h_ref, o_ref):
    o_ref[...] = h_ref[...] + jnp.dot(a_ref[...].astype(BF16), w_ref[...], preferred_element_type=F32)


def _o_proj(a, w_b, layer, h, *, tm):
    m, d = h.shape
    return pl.pallas_call(
        _o_kernel,
        grid=(m // tm,),
        in_specs=[
            pl.BlockSpec((tm, d), lambda i: (i, 0)),
            pl.BlockSpec((None, d, d), lambda i: (layer, 0, 0)),
            pl.BlockSpec((tm, d), lambda i: (i, 0)),
        ],
        out_specs=pl.BlockSpec((tm, d), lambda i: (i, 0)),
        out_shape=jax.ShapeDtypeStruct((m, d), F32),
        compiler_params=_params(1),
        name="o_proj",
    )(a, w_b, h)


def _top_k_picks(gate, candidate, blk_f, axis):
    remaining = candidate
    for _ in range(MOBA_TOP_K):
        gm = jnp.where(remaining, gate, GATE_FLOOR)
        top = jnp.max(gm, axis=axis, keepdims=True)
        hit = jnp.logical_and(remaining, gm == top)
        idx = jnp.min(jnp.where(hit, blk_f, 1e9), axis=axis, keepdims=True)
        pick = blk_f == idx
        remaining = jnp.logical_and(remaining, jnp.logical_not(pick))
        yield pick, idx


def _attn_kernel(q_ref, k_ref, vt_ref, km_ref, o_ref, qt_ref, acc_ref, sa_ref, sb_ref):
    n_g, n_blk = qt_ref.shape[0], qt_ref.shape[1]
    tq = MOBA_BLOCK
    q_scale = HEAD_DIM ** -0.5 * 1.4426950408889634
    blk_i = lax.broadcasted_iota(jnp.int32, (n_blk, tq), 0)
    blk_f = blk_i.astype(F32)
    heads = [slice(g * HEAD_DIM, (g + 1) * HEAD_DIM) for g in range(n_g)]
    keys = [slice(2 * g * HEAD_DIM, (2 * g + 1) * HEAD_DIM) for g in range(n_g)]
    keys_aug = [slice(2 * g * HEAD_DIM, (2 * g + 2) * HEAD_DIM) for g in range(n_g)]
    assert n_blk <= qt_ref.shape[2] - HEAD_DIM

    pad = jnp.zeros((qt_ref.shape[2] - HEAD_DIM - n_blk, tq), BF16)
    for g in range(n_g):
        km = km_ref[:, heads[g]]
        for i in range(n_blk):
            q_t = q_ref[i * tq:(i + 1) * tq, heads[g]].T
            qt_ref[g, i, 0:HEAD_DIM, :] = (q_t * q_scale).astype(BF16)
            if i <= MOBA_TOP_K:
                selected = blk_i < i
            else:
                gate_t = jnp.dot(km, q_t, preferred_element_type=F32, precision=lax.Precision.HIGHEST)
                selected = jnp.zeros((n_blk, tq), jnp.bool_)
                for pick, _ in _top_k_picks(gate_t, blk_i < i, blk_f, 0):
                    selected = jnp.logical_or(selected, pick)
            qt_ref[g, i, HEAD_DIM:HEAD_DIM + n_blk, :] = jnp.where(selected, 0.0, NEG_INF).astype(BF16)
            qt_ref[g, i, HEAD_DIM + n_blk:, :] = pad

    kpos = lax.broadcasted_iota(jnp.int32, (tq, tq), 0)
    qpos = lax.broadcasted_iota(jnp.int32, (tq, tq), 1)
    causal = kpos <= qpos

    last_pair = n_blk // 2 - 1

    def tile(i, _):
        start = pl.multiple_of(i * tq, tq)

        def score_pair(pair_idx, dst_ref):
            st = pl.multiple_of(jnp.minimum(pair_idx, last_pair) * (2 * tq), 2 * tq)
            for g in range(n_g):
                dst_ref[g] = jnp.dot(k_ref[pl.ds(st, 2 * tq), keys_aug[g]], qt_ref[g, i], preferred_element_type=F32)

        def fold_pair(src_ref, pair_idx, ms):
            j0 = 2 * jnp.minimum(pair_idx, last_pair)
            new_ms, alphas, ps = [], [], []
            for g in range(n_g):
                sa = src_ref[g, 0:tq, :]
                sb = src_ref[g, tq:2 * tq, :]
                m_blk = jnp.maximum(jnp.max(sa, axis=0, keepdims=True), jnp.max(sb, axis=0, keepdims=True))
                m_new = jnp.maximum(ms[g], m_blk)
                new_ms.append(m_new)
                alphas.append(jnp.exp2(ms[g] - m_new))
                ps.append((jnp.exp2(sa - m_new).astype(BF16), jnp.exp2(sb - m_new).astype(BF16)))
            for g in range(n_g):
                pv = (jnp.dot(vt_ref[g, j0], ps[g][0], preferred_element_type=F32)
                      + jnp.dot(vt_ref[g, j0 + 1], ps[g][1], preferred_element_type=F32))
                acc_ref[g] = alphas[g] * acc_ref[g] + pv
            return tuple(new_ms)

        ms = []
        s_own = [jnp.dot(k_ref[pl.ds(start, tq), keys[g]], qt_ref[g, i, 0:HEAD_DIM, :], preferred_element_type=F32)
                 for g in range(n_g)]
        score_pair(0, sa_ref)
        p_own = []
        for g in range(n_g):
            s = jnp.where(causal, s_own[g], NEG_INF)
            m0 = jnp.max(s, axis=0, keepdims=True)
            ms.append(m0)
            p_own.append(jnp.exp2(s - m0).astype(BF16))
        for g in range(n_g):
            acc_ref[g] = jnp.dot(vt_ref[g, i], p_own[g], preferred_element_type=F32)

        def two_pairs(t, ms):
            score_pair(2 * t + 1, sb_ref)
            ms = fold_pair(sa_ref, 2 * t, ms)
            score_pair(2 * t + 2, sa_ref)
            return fold_pair(sb_ref, 2 * t + 1, ms)

        n_pairs = (i + 1) // 2
        lax.fori_loop(0, (n_pairs + 1) // 2, two_pairs, tuple(ms))
        for g in range(n_g):
            inv_l = 1.0 / acc_ref[g, HEAD_DIM:HEAD_DIM + 1, :]
            o_ref[pl.ds(start, tq), heads[g]] = (acc_ref[g, 0:HEAD_DIM, :] * inv_l).T.astype(o_ref.dtype)
        return 0

    lax.fori_loop(0, n_blk, tile, 0)


ATTN_HEADS_PER_STEP = 4


def _moba_prompt(q, k_aug, vt_b, kmean, *, batch, seq):
    m, d = q.shape
    n_g = ATTN_HEADS_PER_STEP
    n_blk = seq // MOBA_BLOCK
    width = n_g * HEAD_DIM
    once = pl.Buffered(1)
    return pl.pallas_call(
        _attn_kernel,
        grid=(batch, d // width),
        in_specs=[
            pl.BlockSpec((seq, width), lambda b, h: (b, h), pipeline_mode=once),
            pl.BlockSpec((seq, 2 * width), lambda b, h: (b, h), pipeline_mode=once),
            pl.BlockSpec((None, n_g, n_blk, VT_ROWS, MOBA_BLOCK), lambda b, h: (b, h, 0, 0, 0), pipeline_mode=once),
            pl.BlockSpec((None, n_blk, width), lambda b, h: (b, 0, h)),
        ],
        out_specs=pl.BlockSpec((seq, width), lambda b, h: (b, h), pipeline_mode=once),
        out_shape=jax.ShapeDtypeStruct((m, d), BF16),
        scratch_shapes=[
            pltpu.VMEM((n_g, n_blk, 2 * HEAD_DIM, MOBA_BLOCK), BF16),
            pltpu.VMEM((n_g, VT_ROWS, MOBA_BLOCK), F32),
            pltpu.VMEM((n_g, 2 * MOBA_BLOCK, MOBA_BLOCK), F32),
            pltpu.VMEM((n_g, 2 * MOBA_BLOCK, MOBA_BLOCK), F32),
        ],
        compiler_params=_params(2),
        name="moba_prompt",
    )(q, k_aug, vt_b, kmean)


MEAN_BLOCKS_PER_STEP = 4


def _block_mean_kernel(pt_ref, *refs):
    del pt_ref
    page_refs, o_ref = refs[:-1], refs[-1]
    for n in range(MEAN_BLOCKS_PER_STEP):
        m0 = jnp.mean(page_refs[PAGES_PER_BLOCK * n][...], axis=0)
        m1 = jnp.mean(page_refs[PAGES_PER_BLOCK * n + 1][...], axis=0)
        o_ref[n] = (m0 + m1) * 0.5


def _cache_block_means(cache_k, page_table_flat, *, n_seq, n_pages):
    _, page, n_h, dh = cache_k.shape
    n_blk = n_pages // PAGES_PER_BLOCK
    pages_per_step = MEAN_BLOCKS_PER_STEP * PAGES_PER_BLOCK

    def page_spec(p):
        return pl.BlockSpec((None, page, n_h, dh), lambda b, n, pt: (pt[b * n_pages + pages_per_step * n + p], 0, 0, 0))

    grid_spec = pltpu.PrefetchScalarGridSpec(
        num_scalar_prefetch=1,
        grid=(n_seq, n_blk // MEAN_BLOCKS_PER_STEP),
        in_specs=[page_spec(p) for p in range(pages_per_step)],
        out_specs=pl.BlockSpec((None, MEAN_BLOCKS_PER_STEP, n_h, dh), lambda b, n, pt: (b, n, 0, 0)),
    )
    return pl.pallas_call(
        _block_mean_kernel,
        grid_spec=grid_spec,
        out_shape=jax.ShapeDtypeStruct((n_seq, n_blk, n_h, dh), F32),
        compiler_params=_params(2),
        name="cache_block_means",
    )(page_table_flat, *([cache_k] * pages_per_step))


def _select_kernel(q_ref, bm_ref, sel_ref, *, n_valid):
    q = q_ref[...]
    t = q.shape[0]
    n_lane = bm_ref.shape[1]
    lane = lax.broadcasted_iota(jnp.int32, (t, n_lane), 1)
    lane_f = lane.astype(F32)
    out = jnp.zeros((t, n_lane), jnp.int32)
    for h in range(q.shape[1] // HEAD_DIM):
        cols = slice(h * HEAD_DIM, (h + 1) * HEAD_DIM)
        gate = lax.dot_general(q[:, cols], bm_ref[h], (((1,), (1,)), ((), ())),
                               preferred_element_type=F32, precision=lax.Precision.HIGHEST)
        for r, (_, idx) in enumerate(_top_k_picks(gate, lane < n_valid, lane_f, 1)):
            out = jnp.where(lane == h * MOBA_TOP_K + r, idx.astype(jnp.int32), out)
    sel_ref[...] = out


def _select_blocks(q, bm_pad, *, n_seq, n_valid):
    m, d = q.shape
    t = m // n_seq
    _, n_h, n_lane, dh = bm_pad.shape
    return pl.pallas_call(
        functools.partial(_select_kernel, n_valid=n_valid),
        grid=(n_seq,),
        in_specs=[
            pl.BlockSpec((t, d), lambda b: (b, 0)),
            pl.BlockSpec((None, n_h, n_lane, dh), lambda b: (b, 0, 0, 0)),
        ],
        out_specs=pl.BlockSpec((None, t, n_lane), lambda b: (b, 0, 0)),
        out_shape=jax.ShapeDtypeStruct((n_seq, t, n_lane), jnp.int32),
        compiler_params=_params(1),
        name="select_blocks",
    )(q, bm_pad)


def _sample_attn_kernel(sel_ref, pt_ref, q_ref, kn_ref, vn_ref, ck_hbm, cv_hbm, o_ref, kg_ref, vg_ref, sem,
                        *, n_pages):
    b = pl.program_id(0)
    h = pl.program_id(1)
    n_h = pl.num_programs(1)
    n_steps = pl.num_programs(0) * n_h
    step = b * n_h + h
    slot = step % 2
    t_len = q_ref.shape[0]
    n_sel = MOBA_TOP_K * MOBA_BLOCK
    scale = HEAD_DIM ** -0.5

    def gather_copies(bb, hh, sl):
        copies = []
        for t in range(t_len):
            for r in range(MOBA_TOP_K):
                blk = sel_ref[((bb * t_len + t) * n_h + hh) * MOBA_TOP_K + r]
                for pg in range(PAGES_PER_BLOCK):
                    phys = pt_ref[bb * n_pages + blk * PAGES_PER_BLOCK + pg]
                    rows = pl.ds((r * PAGES_PER_BLOCK + pg) * PAGE_SIZE, PAGE_SIZE)
                    copies.append(pltpu.make_async_copy(ck_hbm.at[phys, :, hh, :], kg_ref.at[sl, t, rows, :], sem.at[sl, 0]))
                    copies.append(pltpu.make_async_copy(cv_hbm.at[phys, :, hh, :], vg_ref.at[sl, t, rows, :], sem.at[sl, 1]))
        return copies

    @pl.when(step == 0)
    def _():
        for cp in gather_copies(b, h, slot):
            cp.start()

    @pl.when(step + 1 < n_steps)
    def _():
        nxt = step + 1
        for cp in gather_copies(nxt // n_h, nxt % n_h, 1 - slot):
            cp.start()

    for cp in gather_copies(b, h, slot):
        cp.wait()

    q = q_ref[...]
    k_own = kn_ref[...]
    v_own = vn_ref[...]
    own_i = lax.broadcasted_iota(jnp.int32, (t_len, 1), 0)
    rows = []
    for t in range(t_len):
        q_t = q[t:t + 1, :]
        s_sel = jnp.sum(kg_ref[slot, t] * q_t, axis=1, keepdims=True) * scale
        s_o = jnp.sum(k_own * q_t, axis=1, keepdims=True) * scale
        s_o = jnp.where(own_i <= t, s_o, NEG_INF)
        m = jnp.maximum(jnp.max(s_sel, axis=0, keepdims=True), jnp.max(s_o, axis=0, keepdims=True))
        p_sel = jnp.exp(s_sel - m)
        p_o = jnp.exp(s_o - m)
        l = jnp.sum(p_sel, axis=0, keepdims=True) + jnp.sum(p_o, axis=0, keepdims=True)
        out = (jnp.sum(p_sel * vg_ref[slot, t], axis=0, keepdims=True)
               + jnp.sum(p_o * v_own, axis=0, keepdims=True))
        rows.append(out / l)
    o_ref[...] = jnp.concatenate(rows, axis=0)


def _moba_sample(q, k_new, v_new, cache_k3, cache_v3, sel_flat, page_table_flat, *, n_seq, n_pages):
    m, d = q.shape
    t = m // n_seq
    n_h = d // HEAD_DIM
    n_sel = MOBA_TOP_K * MOBA_BLOCK
    row_spec = pl.BlockSpec((t, HEAD_DIM), lambda b, h, sel, pt: (b, h))
    grid_spec = pltpu.PrefetchScalarGridSpec(
        num_scalar_prefetch=2,
        grid=(n_seq, n_h),
        in_specs=[row_spec, row_spec, row_spec, pl.BlockSpec(memory_space=pl.ANY), pl.BlockSpec(memory_space=pl.ANY)],
        out_specs=row_spec,
        scratch_shapes=[
            pltpu.VMEM((2, t, n_sel, HEAD_DIM), F32),
            pltpu.VMEM((2, t, n_sel, HEAD_DIM), F32),
            pltpu.SemaphoreType.DMA((2, 2)),
        ],
    )
    return pl.pallas_call(
        functools.partial(_sample_attn_kernel, n_pages=n_pages),
        grid_spec=grid_spec,
        out_shape=jax.ShapeDtypeStruct((m, d), F32),
        compiler_params=_params(2),
        name="moba_sample",
    )(sel_flat, page_table_flat, q, k_new, v_new, cache_k3, cache_v3)


def _rope_tables(pos):
    half = HEAD_DIM // 2
    inv = 1.0 / (ROPE_THETA ** (jnp.arange(half, dtype=F32) * (2.0 / HEAD_DIM)))
    ang = pos.astype(F32)[:, None] * inv[None, :]
    cos = jnp.cos(ang)
    sin = jnp.sin(ang)
    return jnp.concatenate([cos, cos], axis=-1), jnp.concatenate([-sin, sin], axis=-1)


def kernel(x_prompt, x_sample, state_pool, cache_k, cache_v, page_table, g_pool, w_pool, s_pool, g_ffn, w_gate_up,
           w_down, g_kv, w_kv, g_attn, w_q, w_o, g_final):
    n_b, seq, d = x_prompt.shape
    n_db, t_dec, _ = x_sample.shape
    depth = g_ffn.shape[0]
    n_pool = g_pool.shape[0]
    n_h = d // HEAD_DIM
    n_pages = page_table.shape[1]
    n_past_blk = PAST_LEN // MOBA_BLOCK
    m_p = n_b * seq
    m_s = n_db * t_dec
    state_rows = state_pool.shape[2]
    assert n_pages == n_past_blk * PAGES_PER_BLOCK, "own MoBA block must hold only the new tokens"
    assert seq % MOBA_BLOCK == 0 and state_rows == POOL_HALO - 1

    tm_pool, tm_ffn, tf, tm_proj = 256, 512, 512, 512

    w_pool_b = w_pool.astype(BF16)
    w_gu_b = w_gate_up.astype(BF16)
    w_dn_b = w_down.astype(BF16)
    w_kv_b = w_kv.astype(BF16)
    w_q_b = w_q.astype(BF16)
    w_o_b = w_o.astype(BF16)

    cos_p, sin_p = _rope_tables(jnp.arange(seq))
    cos_s, sin_s = _rope_tables(PAST_LEN + jnp.arange(t_dec))
    cos_s = jnp.tile(cos_s, (n_db, 1))
    sin_s = jnp.tile(sin_s, (n_db, 1))

    pt_flat = page_table.reshape(-1)

    hp = x_prompt
    hs = x_sample
    pool_p, pool_s = [], []
    for layer in range(depth):
        last = layer == depth - 1
        if layer < n_pool:
            hp3, zt_p = _pool_layer(hp.reshape(n_b, seq, d), hp.reshape(n_b, seq, d), g_pool[layer], w_pool_b[layer],
                                    s_pool[layer], tm=tm_pool, halo_normalized=False, pos0=0)
            state = state_pool[layer]
            halo_s = jnp.pad(state, ((0, 0), (POOL_HALO - state_rows, 0), (0, 0)))
            hs3, zt_s = _pool_layer(hs.reshape(n_db, t_dec, d), halo_s, g_pool[layer], w_pool_b[layer], s_pool[layer],
                                    tm=t_dec, halo_normalized=True, pos0=PAST_LEN)
            pool_p.append(zt_p[:, POOL_HALO - state_rows:])
            pool_s.append(jnp.concatenate([state, zt_s], axis=1)[:, -state_rows:])
            hp = hp3.reshape(m_p, d)
            hs = hs3.reshape(m_s, d)
        else:
            a = layer - n_pool
            if a == 0:
                k_p, k_pb, kmean = _k_proj(hp, g_kv, w_kv_b, cos_p, sin_p, tm=tm_proj, with_blocks=True)
                v_p, vt_pb = _v_proj(hp, g_kv, w_kv_b, tm=tm_proj, with_blocks=True, seq=seq)
                (k_s,) = _k_proj(hs, g_kv, w_kv_b, cos_s, sin_s, tm=m_s, with_blocks=False)
                (v_s,) = _v_proj(hs, g_kv, w_kv_b, tm=m_s, with_blocks=False, seq=seq)
                kmean = kmean.reshape(n_b, seq // MOBA_BLOCK, d)
                bm = _cache_block_means(cache_k, pt_flat, n_seq=n_db, n_pages=n_pages)
                bm_pad = jnp.pad(bm.transpose(0, 2, 1, 3), ((0, 0), (0, 0), (0, 128 - n_past_blk), (0, 0)))
            q_p = _q_proj(hp, g_attn[a], w_q_b, a, cos_p, sin_p, tm=tm_proj)
            att_p = _moba_prompt(q_p, k_pb, vt_pb, kmean, batch=n_b, seq=seq)
            hp = _o_proj(att_p, w_o_b, a, hp, tm=tm_proj)

            q_s = _q_proj(hs, g_attn[a], w_q_b, a, cos_s, sin_s, tm=m_s)
            sel = _select_blocks(q_s, bm_pad, n_seq=n_db, n_valid=n_past_blk)
            sel_flat = sel[:, :, :n_h * MOBA_TOP_K].reshape(-1)
            att_s = _moba_sample(q_s, k_s, v_s, cache_k, cache_v, sel_flat, pt_flat, n_seq=n_db, n_pages=n_pages)
            hs = _o_proj(att_s, w_o_b, a, hs, tm=m_s)
        hp = _ffn(hp, g_ffn[layer], w_gu_b, w_dn_b, layer, g_final, tm=tm_ffn, tf=tf, final_norm=last)
        hs = _ffn(hs, g_ffn[layer], w_gu_b, w_dn_b, layer, g_final, tm=m_s, tf=tf, final_norm=last)

    y_prompt = hp.reshape(n_b, seq, d)
    y_sample = hs.reshape(n_db, t_dec, d)
    new_pool_prompt = jnp.stack(pool_p)
    new_pool_sample = jnp.stack(pool_s)
    shape_p = (n_b, seq, n_h, HEAD_DIM)
    shape_s = (n_db, t_dec, n_h, HEAD_DIM)
    return (y_prompt, y_sample, new_pool_prompt, new_pool_sample, k_p.reshape(shape_p), v_p.reshape(shape_p),
            k_s.reshape(shape_s), v_s.reshape(shape_s))
```

```python
import functools

import jax
import jax.numpy as jnp
from jax import lax
from jax.experimental import pallas as pl
from jax.experimental.pallas import tpu as pltpu

F32 = jnp.float32
BF16 = jnp.bfloat16

POOL_WINDOWS = (2, 4, 8, 16)
POOL_HALO = 16
HEAD_DIM = 128
MOBA_BLOCK = 256
MOBA_TOP_K = 3
PAGE_SIZE = 128
PAGES_PER_BLOCK = MOBA_BLOCK // PAGE_SIZE
PAST_LEN = 16384
ROPE_THETA = 10000.0
RMS_EPS = 1e-6
NEG_INF = -1e30
GATE_FLOOR = -3.0e38

VMEM_LIMIT_BYTES = 56 * 1024 * 1024


def _params(n_axes):
    return pltpu.CompilerParams(dimension_semantics=("arbitrary",) * n_axes, vmem_limit_bytes=VMEM_LIMIT_BYTES)


def _rmsnorm(x, g):
    ms = jnp.mean(x * x, axis=-1, keepdims=True)
    return x * lax.rsqrt(ms + RMS_EPS) * g


def _pool_kernel(x_ref, halo_ref, g_ref, w_ref, s_ref, h_ref, zt_ref, zs_ref, *, tm, halo_normalized, pos0):
    i = pl.program_id(1)
    d = x_ref.shape[-1]
    c = d // len(POOL_WINDOWS)
    x = x_ref[...]
    g = g_ref[...]
    z = _rmsnorm(x, g)
    if halo_normalized:
        zh = halo_ref[...]
    else:
        zh = _rmsnorm(halo_ref[...], g) * (i > 0).astype(F32)
    zs_ref[0:POOL_HALO, :] = zh
    zs_ref[POOL_HALO:POOL_HALO + tm, :] = z
    pos = lax.broadcasted_iota(jnp.int32, (tm, c), 0) + (i * tm + pos0)
    for gi, w in enumerate(POOL_WINDOWS):
        cols = slice(gi * c, (gi + 1) * c)
        zc = z[:, cols]
        tot = zc
        for j in range(1, w):
            tot = tot + zs_ref[POOL_HALO - j:POOL_HALO - j + tm, cols]
        cnt = jnp.minimum(pos + 1, w).astype(F32)
        pooled = tot / cnt - zc
        mixed = jnp.dot(pooled.astype(BF16), w_ref[gi], preferred_element_type=F32)
        h_ref[:, cols] = x[:, cols] + mixed * s_ref[:, cols]

    nt = zt_ref.shape[0]

    @pl.when(i == pl.num_programs(1) - 1)
    def _():
        zt_ref[...] = z[tm - nt:, :]


def _pool_layer(x3, halo3, g, w_b, s, *, tm, halo_normalized, pos0):
    nb, length, d = x3.shape
    nt = min(POOL_HALO, tm)
    n_i = length // tm
    hb = tm // POOL_HALO
    if halo_normalized:
        halo_map = lambda b, i: (b, 0, 0)
    else:
        halo_map = lambda b, i: (b, jnp.maximum(i * hb - 1, 0), 0)
    kern = functools.partial(_pool_kernel, tm=tm, halo_normalized=halo_normalized, pos0=pos0)
    return pl.pallas_call(
        kern,
        grid=(nb, n_i),
        in_specs=[
            pl.BlockSpec((None, tm, d), lambda b, i: (b, i, 0)),
            pl.BlockSpec((None, POOL_HALO, d), halo_map),
            pl.BlockSpec((1, d), lambda b, i: (0, 0)),
            pl.BlockSpec(w_b.shape, lambda b, i: (0, 0, 0)),
            pl.BlockSpec((1, d), lambda b, i: (0, 0)),
        ],
        out_specs=[
            pl.BlockSpec((None, tm, d), lambda b, i: (b, i, 0)),
            pl.BlockSpec((None, nt, d), lambda b, i: (b, 0, 0)),
        ],
        out_shape=[jax.ShapeDtypeStruct((nb, length, d), F32), jax.ShapeDtypeStruct((nb, nt, d), F32)],
        scratch_shapes=[pltpu.VMEM((POOL_HALO + tm, d), F32)],
        compiler_params=_params(2),
        name="pool_layer",
    )(x3, halo3, g.reshape(1, d), w_b, s.reshape(1, d))


def _ffn_kernel(x_ref, xs_ref, g_ref, wg_ref, wu_ref, wd_ref, go_ref, o_ref, os_ref, xn_ref, xns_ref, *, final_norm):
    j = pl.program_id(1)

    def rows(x_ref, o_ref, xn_ref):
        @pl.when(j == 0)
        def _():
            x = x_ref[...]
            xn_ref[...] = _rmsnorm(x, g_ref[...]).astype(BF16)
            o_ref[...] = x

        xn = xn_ref[...]
        gate = jnp.dot(xn, wg_ref[...], preferred_element_type=F32)
        up = jnp.dot(xn, wu_ref[...], preferred_element_type=F32)
        act = (gate * jax.nn.sigmoid(gate) * up).astype(BF16)
        o_ref[...] += jnp.dot(act, wd_ref[...], preferred_element_type=F32)

        if final_norm:
            @pl.when(j == pl.num_programs(1) - 1)
            def _():
                o_ref[...] = _rmsnorm(o_ref[...], go_ref[...])

    rows(x_ref, o_ref, xn_ref)

    @pl.when(pl.program_id(0) == pl.num_programs(0) - 1)
    def _():
        rows(xs_ref, os_ref, xns_ref)


def _ffn(x, xs, g, w_gu_b, w_dn_b, layer, g_out, *, tm, tf, final_norm):
    m, d = x.shape
    m_s = xs.shape[0]
    f = w_dn_b.shape[1]
    n_f = f // tf
    kern = functools.partial(_ffn_kernel, final_norm=final_norm)
    return pl.pallas_call(
        kern,
        grid=(m // tm, n_f),
        in_specs=[
            pl.BlockSpec((tm, d), lambda i, j: (i, 0)),
            pl.BlockSpec((m_s, d), lambda i, j: (0, 0)),
            pl.BlockSpec((1, d), lambda i, j: (0, 0)),
            pl.BlockSpec((None, d, tf), lambda i, j: (layer, 0, j)),
            pl.BlockSpec((None, d, tf), lambda i, j: (layer, 0, j + n_f)),
            pl.BlockSpec((None, tf, d), lambda i, j: (layer, j, 0)),
            pl.BlockSpec((1, d), lambda i, j: (0, 0)),
        ],
        out_specs=[pl.BlockSpec((tm, d), lambda i, j: (i, 0)), pl.BlockSpec((m_s, d), lambda i, j: (0, 0))],
        out_shape=[jax.ShapeDtypeStruct((m, d), F32), jax.ShapeDtypeStruct((m_s, d), F32)],
        scratch_shapes=[pltpu.VMEM((tm, d), BF16), pltpu.VMEM((m_s, d), BF16)],
        compiler_params=_params(2),
        name="swiglu",
    )(x, xs, g.reshape(1, d), w_gu_b, w_gu_b, w_dn_b, g_out.reshape(1, d))


def _rope(xh, cos, sin_signed):
    return xh * cos + pltpu.roll(xh, HEAD_DIM // 2, axis=1) * sin_signed


def _q_kernel(x_ref, g_ref, w_ref, cos_ref, sin_ref, q_ref):
    xn = _rmsnorm(x_ref[...], g_ref[...]).astype(BF16)
    acc = jnp.dot(xn, w_ref[...], preferred_element_type=F32)
    cos = cos_ref[...]
    sin = sin_ref[...]
    for h in range(acc.shape[1] // HEAD_DIM):
        cols = slice(h * HEAD_DIM, (h + 1) * HEAD_DIM)
        q_ref[:, cols] = _rope(acc[:, cols], cos, sin)


def _q_proj(x, g, w_b, layer, cos, sin, *, tm):
    m, d = x.shape
    n_pos = cos.shape[0] // tm
    return pl.pallas_call(
        _q_kernel,
        grid=(m // tm,),
        in_specs=[
            pl.BlockSpec((tm, d), lambda i: (i, 0)),
            pl.BlockSpec((1, d), lambda i: (0, 0)),
            pl.BlockSpec((None, d, d), lambda i: (layer, 0, 0)),
            pl.BlockSpec((tm, HEAD_DIM), lambda i: (i % n_pos, 0)),
            pl.BlockSpec((tm, HEAD_DIM), lambda i: (i % n_pos, 0)),
        ],
        out_specs=pl.BlockSpec((tm, d), lambda i: (i, 0)),
        out_shape=jax.ShapeDtypeStruct((m, d), F32),
        compiler_params=_params(1),
        name="q_proj",
    )(x, g.reshape(1, d), w_b, cos, sin)


def _k_kernel(x_ref, g_ref, w_ref, cos_ref, sin_ref, k_ref, *extra, with_blocks):
    xn = _rmsnorm(x_ref[...], g_ref[...]).astype(BF16)
    acc = jnp.dot(xn, w_ref[...], preferred_element_type=F32)
    cos = cos_ref[...]
    sin = sin_ref[...]
    n_h = acc.shape[1] // HEAD_DIM
    for h in range(n_h):
        cols = slice(h * HEAD_DIM, (h + 1) * HEAD_DIM)
        k_ref[:, cols] = _rope(acc[:, cols], cos, sin)
    if with_blocks:
        kb_ref, km_ref = extra
        k = k_ref[...]
        kb_ref[...] = k.astype(BF16)
        for r in range(k.shape[0] // MOBA_BLOCK):
            km_ref[r] = jnp.mean(k[r * MOBA_BLOCK:(r + 1) * MOBA_BLOCK, :], axis=0, keepdims=True)


def _k_proj(x, g, w_kv_b, cos, sin, *, tm, with_blocks):
    m, d = x.shape
    n_pos = cos.shape[0] // tm
    out_specs = [pl.BlockSpec((tm, d), lambda i: (i, 0))]
    out_shape = [jax.ShapeDtypeStruct((m, d), F32)]
    if with_blocks:
        r = tm // MOBA_BLOCK
        out_specs += [pl.BlockSpec((tm, d), lambda i: (i, 0)), pl.BlockSpec((r, 1, d), lambda i: (i, 0, 0))]
        out_shape += [jax.ShapeDtypeStruct((m, d), BF16), jax.ShapeDtypeStruct((m // MOBA_BLOCK, 1, d), F32)]
    return pl.pallas_call(
        functools.partial(_k_kernel, with_blocks=with_blocks),
        grid=(m // tm,),
        in_specs=[
            pl.BlockSpec((tm, d), lambda i: (i, 0)),
            pl.BlockSpec((1, d), lambda i: (0, 0)),
            pl.BlockSpec((d, d), lambda i: (0, 0)),
            pl.BlockSpec((tm, HEAD_DIM), lambda i: (i % n_pos, 0)),
            pl.BlockSpec((tm, HEAD_DIM), lambda i: (i % n_pos, 0)),
        ],
        out_specs=out_specs,
        out_shape=out_shape,
        compiler_params=_params(1),
        name="k_proj",
    )(x, g.reshape(1, d), w_kv_b, cos, sin)


def _v_kernel(x_ref, g_ref, w_ref, v_ref, *extra, with_blocks):
    xn = _rmsnorm(x_ref[...], g_ref[...]).astype(BF16)
    v = jnp.dot(xn, w_ref[...], preferred_element_type=F32)
    v_ref[...] = v
    if with_blocks:
        (vt_ref,) = extra
        for h in range(v.shape[1] // HEAD_DIM):
            for r in range(v.shape[0] // MOBA_BLOCK):
                blk = v[r * MOBA_BLOCK:(r + 1) * MOBA_BLOCK, h * HEAD_DIM:(h + 1) * HEAD_DIM]
                vt_ref[h, r] = blk.T.astype(BF16)


def _v_proj(x, g, w_kv_b, *, tm, with_blocks, seq):
    m, d = x.shape
    n_h = d // HEAD_DIM
    out_specs = [pl.BlockSpec((tm, d), lambda i: (i, 0))]
    out_shape = [jax.ShapeDtypeStruct((m, d), F32)]
    if with_blocks:
        r = tm // MOBA_BLOCK
        tiles_per_seq = seq // tm
        out_specs.append(pl.BlockSpec((None, n_h, r, HEAD_DIM, MOBA_BLOCK),
                                      lambda i: (i // tiles_per_seq, 0, i % tiles_per_seq, 0, 0)))
        out_shape.append(jax.ShapeDtypeStruct((m // seq, n_h, seq // MOBA_BLOCK, HEAD_DIM, MOBA_BLOCK), BF16))
    return pl.pallas_call(
        functools.partial(_v_kernel, with_blocks=with_blocks),
        grid=(m // tm,),
        in_specs=[
            pl.BlockSpec((tm, d), lambda i: (i, 0)),
            pl.BlockSpec((1, d), lambda i: (0, 0)),
            pl.BlockSpec((d, d), lambda i: (0, 1)),
        ],
        out_specs=out_specs,
        out_shape=out_shape,
        compiler_params=_params(1),
        name="v_proj",
    )(x, g.reshape(1, d), w_kv_b)


def _o_kernel(a_ref, w_ref, h_ref, o_ref):
    o_ref[...] = h_ref[...] + jnp.dot(a_ref[...].astype(BF16), w_ref[...], preferred_element_type=F32)


def _o_proj(a, w_b, layer, h, *, tm):
    m, d = h.shape
    return pl.pallas_call(
        _o_kernel,
        grid=(m // tm,),
        in_specs=[
            pl.BlockSpec((tm, d), lambda i: (i, 0)),
            pl.BlockSpec((None, d, d), lambda i: (layer, 0, 0)),
            pl.BlockSpec((tm, d), lambda i: (i, 0)),
        ],
        out_specs=pl.BlockSpec((tm, d), lambda i: (i, 0)),
        out_shape=jax.ShapeDtypeStruct((m, d), F32),
        compiler_params=_params(1),
        name="o_proj",
    )(a, w_b, h)


def _top_k_picks(gate, candidate, blk_f, axis):
    remaining = candidate
    for _ in range(MOBA_TOP_K):
        gm = jnp.where(remaining, gate, GATE_FLOOR)
        top = jnp.max(gm, axis=axis, keepdims=True)
        hit = jnp.logical_and(remaining, gm == top)
        idx = jnp.min(jnp.where(hit, blk_f, 1e9), axis=axis, keepdims=True)
        pick = blk_f == idx
        remaining = jnp.logical_and(remaining, jnp.logical_not(pick))
        yield pick, idx


def _attn_kernel(q_ref, k_ref, vt_ref, km_ref, o_ref, qt_ref, acc_ref, sa_ref, sb_ref):
    n_g, n_blk = qt_ref.shape[0], qt_ref.shape[1]
    tq = MOBA_BLOCK
    q_scale = HEAD_DIM ** -0.5 * 1.4426950408889634
    blk_i = lax.broadcasted_iota(jnp.int32, (n_blk, tq), 0)
    blk_f = blk_i.astype(F32)
    heads = [slice(g * HEAD_DIM, (g + 1) * HEAD_DIM) for g in range(n_g)]
    assert n_blk <= qt_ref.shape[2] - HEAD_DIM

    pad = jnp.zeros((qt_ref.shape[2] - HEAD_DIM - n_blk, tq), BF16)
    for g in range(n_g):
        km = km_ref[:, heads[g]]
        for i in range(n_blk):
            q_t = q_ref[i * tq:(i + 1) * tq, heads[g]].T
            qt_ref[g, i, 0:HEAD_DIM, :] = (q_t * q_scale).astype(BF16)
            if i <= MOBA_TOP_K:
                selected = blk_i < i
            else:
                gate_t = jnp.dot(km, q_t, preferred_element_type=F32, precision=lax.Precision.HIGHEST)
                selected = jnp.zeros((n_blk, tq), jnp.bool_)
                for pick, _ in _top_k_picks(gate_t, blk_i < i, blk_f, 0):
                    selected = jnp.logical_or(selected, pick)
            qt_ref[g, i, HEAD_DIM:HEAD_DIM + n_blk, :] = jnp.where(selected, 0.0, NEG_INF).astype(BF16)
            qt_ref[g, i, HEAD_DIM + n_blk:, :] = pad

    kpos = lax.broadcasted_iota(jnp.int32, (tq, tq), 0)
    qpos = lax.broadcasted_iota(jnp.int32, (tq, tq), 1)
    causal = kpos <= qpos

    last_pair = n_blk // 2 - 1
    pair_row = lax.broadcasted_iota(jnp.int32, (2 * tq, HEAD_DIM), 0)
    pair_lane = lax.broadcasted_iota(jnp.int32, (2 * tq, HEAD_DIM), 1)
    lane_minus_half = pair_lane - jnp.where(pair_row >= tq, 1, 0)
    sum_rows = acc_ref.shape[1] - HEAD_DIM
    ones_rows = jnp.where(lax.broadcasted_iota(jnp.int32, (sum_rows, tq), 0) == 0, 1.0, 0.0).astype(BF16)

    def values_aug(g, j):
        return jnp.concatenate([vt_ref[g, j], ones_rows], axis=0)

    def tile(i, _):
        start = pl.multiple_of(i * tq, tq)

        def score_pair(pair_idx, dst_ref):
            pair_c = jnp.minimum(pair_idx, last_pair)
            st = pl.multiple_of(pair_c * (2 * tq), 2 * tq)
            onehot = jnp.where(lane_minus_half == 2 * pair_c, 1.0, 0.0).astype(BF16)
            for g in range(n_g):
                keys_aug = jnp.concatenate([k_ref[pl.ds(st, 2 * tq), heads[g]], onehot], axis=1)
                dst_ref[g] = jnp.dot(keys_aug, qt_ref[g, i], preferred_element_type=F32)

        def fold_pair(src_ref, pair_idx, ms):
            j0 = 2 * jnp.minimum(pair_idx, last_pair)
            new_ms, alphas, ps = [], [], []
            for g in range(n_g):
                sa = src_ref[g, 0:tq, :]
                sb = src_ref[g, tq:2 * tq, :]
                m_blk = jnp.maximum(jnp.max(sa, axis=0, keepdims=True), jnp.max(sb, axis=0, keepdims=True))
                m_new = jnp.maximum(ms[g], m_blk)
                new_ms.append(m_new)
                alphas.append(jnp.exp2(ms[g] - m_new))
                ps.append((jnp.exp2(sa - m_new).astype(BF16), jnp.exp2(sb - m_new).astype(BF16)))
            for g in range(n_g):
                pv = (jnp.dot(values_aug(g, j0), ps[g][0], preferred_element_type=F32)
                      + jnp.dot(values_aug(g, j0 + 1), ps[g][1], preferred_element_type=F32))
                acc_ref[g] = alphas[g] * acc_ref[g] + pv
            return tuple(new_ms)

        ms = []
        s_own = [jnp.dot(k_ref[pl.ds(start, tq), heads[g]], qt_ref[g, i, 0:HEAD_DIM, :], preferred_element_type=F32)
                 for g in range(n_g)]
        score_pair(0, sa_ref)
        p_own = []
        for g in range(n_g):
            s = jnp.where(causal, s_own[g], NEG_INF)
            m0 = jnp.max(s, axis=0, keepdims=True)
            ms.append(m0)
            p_own.append(jnp.exp2(s - m0).astype(BF16))
        for g in range(n_g):
            acc_ref[g] = jnp.dot(values_aug(g, i), p_own[g], preferred_element_type=F32)

        def two_pairs(t, ms):
            score_pair(2 * t + 1, sb_ref)
            ms = fold_pair(sa_ref, 2 * t, ms)
            score_pair(2 * t + 2, sa_ref)
            return fold_pair(sb_ref, 2 * t + 1, ms)

        n_pairs = (i + 1) // 2
        lax.fori_loop(0, (n_pairs + 1) // 2, two_pairs, tuple(ms))
        for g in range(n_g):
            inv_l = 1.0 / acc_ref[g, HEAD_DIM:HEAD_DIM + 1, :]
            o_ref[pl.ds(start, tq), heads[g]] = (acc_ref[g, 0:HEAD_DIM, :] * inv_l).T.astype(o_ref.dtype)
        return 0

    lax.fori_loop(0, n_blk, tile, 0)


ATTN_HEADS_PER_STEP = 4


ATTN_SUM_ROWS = 16


def _moba_prompt(q, k_b, vt_b, kmean, *, batch, seq):
    m, d = q.shape
    n_g = ATTN_HEADS_PER_STEP
    n_blk = seq // MOBA_BLOCK
    width = n_g * HEAD_DIM
    return pl.pallas_call(
        _attn_kernel,
        grid=(batch, d // width),
        in_specs=[
            pl.BlockSpec((seq, width), lambda b, h: (b, h)),
            pl.BlockSpec((seq, width), lambda b, h: (b, h)),
            pl.BlockSpec((None, n_g, n_blk, HEAD_DIM, MOBA_BLOCK), lambda b, h: (b, h, 0, 0, 0)),
            pl.BlockSpec((None, n_blk, width), lambda b, h: (b, 0, h)),
        ],
        out_specs=pl.BlockSpec((seq, width), lambda b, h: (b, h), pipeline_mode=pl.Buffered(1)),
        out_shape=jax.ShapeDtypeStruct((m, d), BF16),
        scratch_shapes=[
            pltpu.VMEM((n_g, n_blk, 2 * HEAD_DIM, MOBA_BLOCK), BF16),
            pltpu.VMEM((n_g, HEAD_DIM + ATTN_SUM_ROWS, MOBA_BLOCK), F32),
            pltpu.VMEM((n_g, 2 * MOBA_BLOCK, MOBA_BLOCK), F32),
            pltpu.VMEM((n_g, 2 * MOBA_BLOCK, MOBA_BLOCK), F32),
        ],
        compiler_params=_params(2),
        name="moba_prompt",
    )(q, k_b, vt_b, kmean)


MEAN_BLOCKS_PER_STEP = 4


def _block_mean_kernel(pt_ref, *refs):
    del pt_ref
    page_refs, o_ref = refs[:-1], refs[-1]
    for n in range(MEAN_BLOCKS_PER_STEP):
        m0 = jnp.mean(page_refs[PAGES_PER_BLOCK * n][...], axis=0)
        m1 = jnp.mean(page_refs[PAGES_PER_BLOCK * n + 1][...], axis=0)
        o_ref[n] = (m0 + m1) * 0.5


def _cache_block_means(cache_k, page_table_flat, *, n_seq, n_pages):
    _, page, n_h, dh = cache_k.shape
    n_blk = n_pages // PAGES_PER_BLOCK
    pages_per_step = MEAN_BLOCKS_PER_STEP * PAGES_PER_BLOCK

    def page_spec(p):
        return pl.BlockSpec((None, page, n_h, dh), lambda b, n, pt: (pt[b * n_pages + pages_per_step * n + p], 0, 0, 0))

    grid_spec = pltpu.PrefetchScalarGridSpec(
        num_scalar_prefetch=1,
        grid=(n_seq, n_blk // MEAN_BLOCKS_PER_STEP),
        in_specs=[page_spec(p) for p in range(pages_per_step)],
        out_specs=pl.BlockSpec((None, MEAN_BLOCKS_PER_STEP, n_h, dh), lambda b, n, pt: (b, n, 0, 0)),
    )
    return pl.pallas_call(
        _block_mean_kernel,
        grid_spec=grid_spec,
        out_shape=jax.ShapeDtypeStruct((n_seq, n_blk, n_h, dh), F32),
        compiler_params=_params(2),
        name="cache_block_means",
    )(page_table_flat, *([cache_k] * pages_per_step))


def _select_kernel(q_ref, bm_ref, sel_ref, *, n_valid):
    q = q_ref[...]
    t = q.shape[0]
    n_lane = bm_ref.shape[1]
    lane = lax.broadcasted_iota(jnp.int32, (t, n_lane), 1)
    lane_f = lane.astype(F32)
    out = jnp.zeros((t, n_lane), jnp.int32)
    for h in range(q.shape[1] // HEAD_DIM):
        cols = slice(h * HEAD_DIM, (h + 1) * HEAD_DIM)
        gate = lax.dot_general(q[:, cols], bm_ref[h], (((1,), (1,)), ((), ())),
                               preferred_element_type=F32, precision=lax.Precision.HIGHEST)
        for r, (_, idx) in enumerate(_top_k_picks(gate, lane < n_valid, lane_f, 1)):
            out = jnp.where(lane == h * MOBA_TOP_K + r, idx.astype(jnp.int32), out)
    sel_ref[...] = out


def _select_blocks(q, bm_pad, *, n_seq, n_valid):
    m, d = q.shape
    t = m // n_seq
    _, n_h, n_lane, dh = bm_pad.shape
    return pl.pallas_call(
        functools.partial(_select_kernel, n_valid=n_valid),
        grid=(n_seq,),
        in_specs=[
            pl.BlockSpec((t, d), lambda b: (b, 0)),
            pl.BlockSpec((None, n_h, n_lane, dh), lambda b: (b, 0, 0, 0)),
        ],
        out_specs=pl.BlockSpec((None, t, n_lane), lambda b: (b, 0, 0)),
        out_shape=jax.ShapeDtypeStruct((n_seq, t, n_lane), jnp.int32),
        compiler_params=_params(1),
        name="select_blocks",
    )(q, bm_pad)


def _sample_attn_kernel(sel_ref, pt_ref, q_ref, kn_ref, vn_ref, ck_hbm, cv_hbm, o_ref, kg_ref, vg_ref, sem,
                        *, n_pages):
    b = pl.program_id(0)
    h = pl.program_id(1)
    n_h = pl.num_programs(1)
    n_steps = pl.num_programs(0) * n_h
    step = b * n_h + h
    slot = step % 2
    t_len = q_ref.shape[0]
    n_sel = MOBA_TOP_K * MOBA_BLOCK
    scale = HEAD_DIM ** -0.5

    def gather_copies(bb, hh, sl):
        copies = []
        for t in range(t_len):
            for r in range(MOBA_TOP_K):
                blk = sel_ref[((bb * t_len + t) * n_h + hh) * MOBA_TOP_K + r]
                for pg in range(PAGES_PER_BLOCK):
                    phys = pt_ref[bb * n_pages + blk * PAGES_PER_BLOCK + pg]
                    rows = pl.ds((r * PAGES_PER_BLOCK + pg) * PAGE_SIZE, PAGE_SIZE)
                    copies.append(pltpu.make_async_copy(ck_hbm.at[phys, :, hh, :], kg_ref.at[sl, t, rows, :], sem.at[sl, 0]))
                    copies.append(pltpu.make_async_copy(cv_hbm.at[phys, :, hh, :], vg_ref.at[sl, t, rows, :], sem.at[sl, 1]))
        return copies

    @pl.when(step == 0)
    def _():
        for cp in gather_copies(b, h, slot):
            cp.start()

    @pl.when(step + 1 < n_steps)
    def _():
        nxt = step + 1
        for cp in gather_copies(nxt // n_h, nxt % n_h, 1 - slot):
            cp.start()

    for cp in gather_copies(b, h, slot):
        cp.wait()

    q = q_ref[...]
    k_own = kn_ref[...]
    v_own = vn_ref[...]
    own_i = lax.broadcasted_iota(jnp.int32, (t_len, 1), 0)
    rows = []
    for t in range(t_len):
        q_t = q[t:t + 1, :]
        s_sel = jnp.sum(kg_ref[slot, t] * q_t, axis=1, keepdims=True) * scale
        s_o = jnp.sum(k_own * q_t, axis=1, keepdims=True) * scale
        s_o = jnp.where(own_i <= t, s_o, NEG_INF)
        m = jnp.maximum(jnp.max(s_sel, axis=0, keepdims=True), jnp.max(s_o, axis=0, keepdims=True))
        p_sel = jnp.exp(s_sel - m)
        p_o = jnp.exp(s_o - m)
        l = jnp.sum(p_sel, axis=0, keepdims=True) + jnp.sum(p_o, axis=0, keepdims=True)
        out = (jnp.sum(p_sel * vg_ref[slot, t], axis=0, keepdims=True)
               + jnp.sum(p_o * v_own, axis=0, keepdims=True))
        rows.append(out / l)
    o_ref[...] = jnp.concatenate(rows, axis=0)


def _moba_sample(q, k_new, v_new, cache_k3, cache_v3, sel_flat, page_table_flat, *, n_seq, n_pages):
    m, d = q.shape
    t = m // n_seq
    n_h = d // HEAD_DIM
    n_sel = MOBA_TOP_K * MOBA_BLOCK
    row_spec = pl.BlockSpec((t, HEAD_DIM), lambda b, h, sel, pt: (b, h))
    grid_spec = pltpu.PrefetchScalarGridSpec(
        num_scalar_prefetch=2,
        grid=(n_seq, n_h),
        in_specs=[row_spec, row_spec, row_spec, pl.BlockSpec(memory_space=pl.ANY), pl.BlockSpec(memory_space=pl.ANY)],
        out_specs=row_spec,
        scratch_shapes=[
            pltpu.VMEM((2, t, n_sel, HEAD_DIM), F32),
            pltpu.VMEM((2, t, n_sel, HEAD_DIM), F32),
            pltpu.SemaphoreType.DMA((2, 2)),
        ],
    )
    return pl.pallas_call(
        functools.partial(_sample_attn_kernel, n_pages=n_pages),
        grid_spec=grid_spec,
        out_shape=jax.ShapeDtypeStruct((m, d), F32),
        compiler_params=_params(2),
        name="moba_sample",
    )(sel_flat, page_table_flat, q, k_new, v_new, cache_k3, cache_v3)


def _rope_tables(pos):
    half = HEAD_DIM // 2
    inv = 1.0 / (ROPE_THETA ** (jnp.arange(half, dtype=F32) * (2.0 / HEAD_DIM)))
    ang = pos.astype(F32)[:, None] * inv[None, :]
    cos = jnp.cos(ang)
    sin = jnp.sin(ang)
    return jnp.concatenate([cos, cos], axis=-1), jnp.concatenate([-sin, sin], axis=-1)


def kernel(x_prompt, x_sample, state_pool, cache_k, cache_v, page_table, g_pool, w_pool, s_pool, g_ffn, w_gate_up,
           w_down, g_kv, w_kv, g_attn, w_q, w_o, g_final):
    n_b, seq, d = x_prompt.shape
    n_db, t_dec, _ = x_sample.shape
    depth = g_ffn.shape[0]
    n_pool = g_pool.shape[0]
    n_h = d // HEAD_DIM
    n_pages = page_table.shape[1]
    n_past_blk = PAST_LEN // MOBA_BLOCK
    m_p = n_b * seq
    m_s = n_db * t_dec
    state_rows = state_pool.shape[2]
    assert n_pages == n_past_blk * PAGES_PER_BLOCK, "own MoBA block must hold only the new tokens"
    assert seq % MOBA_BLOCK == 0 and state_rows == POOL_HALO - 1

    tm_pool, tm_ffn, tf, tm_proj = 256, 512, 512, 512

    w_pool_b = w_pool.astype(BF16)
    w_gu_b = w_gate_up.astype(BF16)
    w_dn_b = w_down.astype(BF16)
    w_kv_b = w_kv.astype(BF16)
    w_q_b = w_q.astype(BF16)
    w_o_b = w_o.astype(BF16)

    cos_p, sin_p = _rope_tables(jnp.arange(seq))
    cos_s, sin_s = _rope_tables(PAST_LEN + jnp.arange(t_dec))
    cos_s = jnp.tile(cos_s, (n_db, 1))
    sin_s = jnp.tile(sin_s, (n_db, 1))

    pt_flat = page_table.reshape(-1)

    hp = x_prompt
    hs = x_sample
    pool_p, pool_s = [], []
    for layer in range(depth):
        last = layer == depth - 1
        if layer < n_pool:
            hp3, zt_p = _pool_layer(hp.reshape(n_b, seq, d), hp.reshape(n_b, seq, d), g_pool[layer], w_pool_b[layer],
                                    s_pool[layer], tm=tm_pool, halo_normalized=False, pos0=0)
            state = state_pool[layer]
            halo_s = jnp.pad(state, ((0, 0), (POOL_HALO - state_rows, 0), (0, 0)))
            hs3, zt_s = _pool_layer(hs.reshape(n_db, t_dec, d), halo_s, g_pool[layer], w_pool_b[layer], s_pool[layer],
                                    tm=t_dec, halo_normalized=True, pos0=PAST_LEN)
            pool_p.append(zt_p[:, POOL_HALO - state_rows:])
            pool_s.append(jnp.concatenate([state, zt_s], axis=1)[:, -state_rows:])
            hp = hp3.reshape(m_p, d)
            hs = hs3.reshape(m_s, d)
        else:
            a = layer - n_pool
            if a == 0:
                k_p, k_pb, kmean = _k_proj(hp, g_kv, w_kv_b, cos_p, sin_p, tm=tm_proj, with_blocks=True)
                v_p, vt_pb = _v_proj(hp, g_kv, w_kv_b, tm=tm_proj, with_blocks=True, seq=seq)
                (k_s,) = _k_proj(hs, g_kv, w_kv_b, cos_s, sin_s, tm=m_s, with_blocks=False)
                (v_s,) = _v_proj(hs, g_kv, w_kv_b, tm=m_s, with_blocks=False, seq=seq)
                kmean = kmean.reshape(n_b, seq // MOBA_BLOCK, d)
                bm = _cache_block_means(cache_k, pt_flat, n_seq=n_db, n_pages=n_pages)
                bm_pad = jnp.pad(bm.transpose(0, 2, 1, 3), ((0, 0), (0, 0), (0, 128 - n_past_blk), (0, 0)))
            q_p = _q_proj(hp, g_attn[a], w_q_b, a, cos_p, sin_p, tm=tm_proj)
            att_p = _moba_prompt(q_p, k_pb, vt_pb, kmean, batch=n_b, seq=seq)
            hp = _o_proj(att_p, w_o_b, a, hp, tm=tm_proj)

            q_s = _q_proj(hs, g_attn[a], w_q_b, a, cos_s, sin_s, tm=m_s)
            sel = _select_blocks(q_s, bm_pad, n_seq=n_db, n_valid=n_past_blk)
            sel_flat = sel[:, :, :n_h * MOBA_TOP_K].reshape(-1)
            att_s = _moba_sample(q_s, k_s, v_s, cache_k, cache_v, sel_flat, pt_flat, n_seq=n_db, n_pages=n_pages)
            hs = _o_proj(att_s, w_o_b, a, hs, tm=m_s)
        hp, hs = _ffn(hp, hs, g_ffn[layer], w_gu_b, w_dn_b, layer, g_final, tm=tm_ffn, tf=tf, final_norm=last)

    y_prompt = hp.reshape(n_b, seq, d)
    y_sample = hs.reshape(n_db, t_dec, d)
    new_pool_prompt = jnp.stack(pool_p)
    new_pool_sample = jnp.stack(pool_s)
    shape_p = (n_b, seq, n_h, HEAD_DIM)
    shape_s = (n_db, t_dec, n_h, HEAD_DIM)
    return (y_prompt, y_sample, new_pool_prompt, new_pool_sample, k_p.reshape(shape_p), v_p.reshape(shape_p),
            k_s.reshape(shape_s), v_s.reshape(shape_s))
```

```python
import functools

import jax
import jax.numpy as jnp
from jax import lax
from jax.experimental import pallas as pl
from jax.experimental.pallas import tpu as pltpu

F32 = jnp.float32
BF16 = jnp.bfloat16

POOL_WINDOWS = (2, 4, 8, 16)
POOL_HALO = 16
HEAD_DIM = 128
MOBA_BLOCK = 256
MOBA_TOP_K = 3
PAGE_SIZE = 128
PAGES_PER_BLOCK = MOBA_BLOCK // PAGE_SIZE
PAST_LEN = 16384
ROPE_THETA = 10000.0
RMS_EPS = 1e-6
NEG_INF = -1e30
GATE_FLOOR = -3.0e38

VMEM_LIMIT_BYTES = 56 * 1024 * 1024


def _params(n_axes):
    return pltpu.CompilerParams(dimension_semantics=("arbitrary",) * n_axes, vmem_limit_bytes=VMEM_LIMIT_BYTES)


def _rmsnorm(x, g):
    ms = jnp.mean(x * x, axis=-1, keepdims=True)
    return x * lax.rsqrt(ms + RMS_EPS) * g


def _pool_kernel(x_ref, halo_ref, g_ref, w_ref, s_ref, h_ref, zt_ref, zs_ref, *, tm, halo_normalized, pos0):
    i = pl.program_id(1)
    d = x_ref.shape[-1]
    c = d // len(POOL_WINDOWS)
    x = x_ref[...]
    g = g_ref[...]
    z = _rmsnorm(x, g)
    if halo_normalized:
        zh = halo_ref[...]
    else:
        zh = _rmsnorm(halo_ref[...], g) * (i > 0).astype(F32)
    zs_ref[0:POOL_HALO, :] = zh
    zs_ref[POOL_HALO:POOL_HALO + tm, :] = z
    pos = lax.broadcasted_iota(jnp.int32, (tm, c), 0) + (i * tm + pos0)
    for gi, w in enumerate(POOL_WINDOWS):
        cols = slice(gi * c, (gi + 1) * c)
        zc = z[:, cols]
        tot = zc
        for j in range(1, w):
            tot = tot + zs_ref[POOL_HALO - j:POOL_HALO - j + tm, cols]
        cnt = jnp.minimum(pos + 1, w).astype(F32)
        pooled = tot / cnt - zc
        mixed = jnp.dot(pooled.astype(BF16), w_ref[gi].astype(BF16), preferred_element_type=F32)
        h_ref[:, cols] = x[:, cols] + mixed * s_ref[:, cols]

    nt = zt_ref.shape[0]

    @pl.when(i == pl.num_programs(1) - 1)
    def _():
        zt_ref[...] = z[tm - nt:, :]


def _pool_layer(x3, halo3, g, w_b, s, *, tm, halo_normalized, pos0):
    nb, length, d = x3.shape
    nt = min(POOL_HALO, tm)
    n_i = length // tm
    hb = tm // POOL_HALO
    if halo_normalized:
        halo_map = lambda b, i: (b, 0, 0)
    else:
        halo_map = lambda b, i: (b, jnp.maximum(i * hb - 1, 0), 0)
    kern = functools.partial(_pool_kernel, tm=tm, halo_normalized=halo_normalized, pos0=pos0)
    return pl.pallas_call(
        kern,
        grid=(nb, n_i),
        in_specs=[
            pl.BlockSpec((None, tm, d), lambda b, i: (b, i, 0)),
            pl.BlockSpec((None, POOL_HALO, d), halo_map),
            pl.BlockSpec((1, d), lambda b, i: (0, 0)),
            pl.BlockSpec(w_b.shape, lambda b, i: (0, 0, 0)),
            pl.BlockSpec((1, d), lambda b, i: (0, 0)),
        ],
        out_specs=[
            pl.BlockSpec((None, tm, d), lambda b, i: (b, i, 0)),
            pl.BlockSpec((None, nt, d), lambda b, i: (b, 0, 0)),
        ],
        out_shape=[jax.ShapeDtypeStruct((nb, length, d), F32), jax.ShapeDtypeStruct((nb, nt, d), F32)],
        scratch_shapes=[pltpu.VMEM((POOL_HALO + tm, d), F32)],
        compiler_params=_params(2),
        name="pool_layer",
    )(x3, halo3, g.reshape(1, d), w_b, s.reshape(1, d))


def _ffn_kernel(x_ref, xs_ref, g_ref, wg_ref, wu_ref, wd_ref, go_ref, o_ref, os_ref, xn_ref, xns_ref, *, final_norm):
    j = pl.program_id(1)

    def rows(x_ref, o_ref, xn_ref):
        @pl.when(j == 0)
        def _():
            x = x_ref[...]
            xn_ref[...] = _rmsnorm(x, g_ref[...]).astype(BF16)
            o_ref[...] = x

        xn = xn_ref[...]
        gate = jnp.dot(xn, wg_ref[...], preferred_element_type=F32)
        up = jnp.dot(xn, wu_ref[...], preferred_element_type=F32)
        act = (gate * jax.nn.sigmoid(gate) * up).astype(BF16)
        o_ref[...] += jnp.dot(act, wd_ref[...], preferred_element_type=F32)

        if final_norm:
            @pl.when(j == pl.num_programs(1) - 1)
            def _():
                o_ref[...] = _rmsnorm(o_ref[...], go_ref[...])

    rows(x_ref, o_ref, xn_ref)

    @pl.when(pl.program_id(0) == pl.num_programs(0) - 1)
    def _():
        rows(xs_ref, os_ref, xns_ref)


def _ffn(x, xs, g, w_gu_b, w_dn_b, layer, g_out, *, tm, tf, final_norm):
    m, d = x.shape
    m_s = xs.shape[0]
    f = w_dn_b.shape[1]
    n_f = f // tf
    kern = functools.partial(_ffn_kernel, final_norm=final_norm)
    return pl.pallas_call(
        kern,
        grid=(m // tm, n_f),
        in_specs=[
            pl.BlockSpec((tm, d), lambda i, j: (i, 0)),
            pl.BlockSpec((m_s, d), lambda i, j: (0, 0)),
            pl.BlockSpec((1, d), lambda i, j: (0, 0)),
            pl.BlockSpec((None, d, tf), lambda i, j: (layer, 0, j)),
            pl.BlockSpec((None, d, tf), lambda i, j: (layer, 0, j + n_f)),
            pl.BlockSpec((None, tf, d), lambda i, j: (layer, j, 0)),
            pl.BlockSpec((1, d), lambda i, j: (0, 0)),
        ],
        out_specs=[pl.BlockSpec((tm, d), lambda i, j: (i, 0)), pl.BlockSpec((m_s, d), lambda i, j: (0, 0))],
        out_shape=[jax.ShapeDtypeStruct((m, d), F32), jax.ShapeDtypeStruct((m_s, d), F32)],
        scratch_shapes=[pltpu.VMEM((tm, d), BF16), pltpu.VMEM((m_s, d), BF16)],
        compiler_params=_params(2),
        name="swiglu",
    )(x, xs, g.reshape(1, d), w_gu_b, w_gu_b, w_dn_b, g_out.reshape(1, d))


def _rope(xh, cos, sin_signed):
    return xh * cos + pltpu.roll(xh, HEAD_DIM // 2, axis=1) * sin_signed


def _round_weight_once(w_ref, wb_ref):
    @pl.when(pl.program_id(0) == 0)
    def _():
        wb_ref[...] = w_ref[...].astype(BF16)


def _prompt_then_sample(n_tiles, prompt_fn, sample_fn):
    i = pl.program_id(0)
    pl.when(i < n_tiles)(prompt_fn)
    pl.when(i == n_tiles)(sample_fn)


def _proj_specs(tm, d, m_s, n_tiles):
    del d
    tile = lambda i: jnp.minimum(i, n_tiles - 1)
    prompt = lambda width: pl.BlockSpec((tm, width), lambda i: (tile(i), 0))
    sample = lambda width: pl.BlockSpec((m_s, width), lambda i: (0, 0))
    return tile, prompt, sample


def _q_kernel(xp_ref, xs_ref, g_ref, w_ref, cosp_ref, sinp_ref, coss_ref, sins_ref, qp_ref, qs_ref, wb_ref, *, n_tiles):
    _round_weight_once(w_ref, wb_ref)

    def rows(x_ref, cos_ref, sin_ref, q_ref):
        xn = _rmsnorm(x_ref[...], g_ref[...]).astype(BF16)
        acc = jnp.dot(xn, wb_ref[...], preferred_element_type=F32)
        cos = cos_ref[...]
        sin = sin_ref[...]
        for h in range(acc.shape[1] // HEAD_DIM):
            cols = slice(h * HEAD_DIM, (h + 1) * HEAD_DIM)
            q_ref[:, cols] = _rope(acc[:, cols], cos, sin)

    _prompt_then_sample(n_tiles, lambda: rows(xp_ref, cosp_ref, sinp_ref, qp_ref),
                        lambda: rows(xs_ref, coss_ref, sins_ref, qs_ref))


def _q_proj(xp, xs, g, w, layer, cos_p, sin_p, cos_s, sin_s, *, tm):
    m, d = xp.shape
    m_s = xs.shape[0]
    n_tiles = m // tm
    n_pos = cos_p.shape[0] // tm
    tile, prompt, sample = _proj_specs(tm, d, m_s, n_tiles)
    rope_p = pl.BlockSpec((tm, HEAD_DIM), lambda i: (tile(i) % n_pos, 0))
    return pl.pallas_call(
        functools.partial(_q_kernel, n_tiles=n_tiles),
        grid=(n_tiles + 1,),
        in_specs=[
            prompt(d), sample(d),
            pl.BlockSpec((1, d), lambda i: (0, 0)),
            pl.BlockSpec((None, d, d), lambda i: (layer, 0, 0), pipeline_mode=pl.Buffered(1)),
            rope_p, rope_p, sample(HEAD_DIM), sample(HEAD_DIM),
        ],
        out_specs=[prompt(d), sample(d)],
        out_shape=[jax.ShapeDtypeStruct((m, d), F32), jax.ShapeDtypeStruct((m_s, d), F32)],
        scratch_shapes=[pltpu.VMEM((d, d), BF16)],
        compiler_params=_params(1),
        name="q_proj",
    )(xp, xs, g.reshape(1, d), w, cos_p, sin_p, cos_s, sin_s)


def _k_kernel(xp_ref, xs_ref, g_ref, w_ref, cosp_ref, sinp_ref, coss_ref, sins_ref, kp_ref, kb_ref, km_ref, ks_ref,
              wb_ref, *, n_tiles):
    _round_weight_once(w_ref, wb_ref)

    def rows(x_ref, cos_ref, sin_ref, k_ref):
        xn = _rmsnorm(x_ref[...], g_ref[...]).astype(BF16)
        acc = jnp.dot(xn, wb_ref[...], preferred_element_type=F32)
        cos = cos_ref[...]
        sin = sin_ref[...]
        for h in range(acc.shape[1] // HEAD_DIM):
            cols = slice(h * HEAD_DIM, (h + 1) * HEAD_DIM)
            k_ref[:, cols] = _rope(acc[:, cols], cos, sin)

    def prompt_rows():
        rows(xp_ref, cosp_ref, sinp_ref, kp_ref)
        k = kp_ref[...]
        kb_ref[...] = k.astype(BF16)
        for r in range(k.shape[0] // MOBA_BLOCK):
            km_ref[r] = jnp.mean(k[r * MOBA_BLOCK:(r + 1) * MOBA_BLOCK, :], axis=0, keepdims=True)

    _prompt_then_sample(n_tiles, prompt_rows, lambda: rows(xs_ref, coss_ref, sins_ref, ks_ref))


def _k_proj(xp, xs, g, w_kv, cos_p, sin_p, cos_s, sin_s, *, tm):
    m, d = xp.shape
    m_s = xs.shape[0]
    n_tiles = m // tm
    n_pos = cos_p.shape[0] // tm
    r = tm // MOBA_BLOCK
    tile, prompt, sample = _proj_specs(tm, d, m_s, n_tiles)
    rope_p = pl.BlockSpec((tm, HEAD_DIM), lambda i: (tile(i) % n_pos, 0))
    return pl.pallas_call(
        functools.partial(_k_kernel, n_tiles=n_tiles),
        grid=(n_tiles + 1,),
        in_specs=[
            prompt(d), sample(d),
            pl.BlockSpec((1, d), lambda i: (0, 0)),
            pl.BlockSpec((d, d), lambda i: (0, 0), pipeline_mode=pl.Buffered(1)),
            rope_p, rope_p, sample(HEAD_DIM), sample(HEAD_DIM),
        ],
        out_specs=[prompt(d), prompt(d), pl.BlockSpec((r, 1, d), lambda i: (tile(i), 0, 0)), sample(d)],
        out_shape=[jax.ShapeDtypeStruct((m, d), F32), jax.ShapeDtypeStruct((m, d), BF16),
                   jax.ShapeDtypeStruct((m // MOBA_BLOCK, 1, d), F32), jax.ShapeDtypeStruct((m_s, d), F32)],
        scratch_shapes=[pltpu.VMEM((d, d), BF16)],
        compiler_params=_params(1),
        name="k_proj",
    )(xp, xs, g.reshape(1, d), w_kv, cos_p, sin_p, cos_s, sin_s)


def _v_kernel(xp_ref, xs_ref, g_ref, w_ref, vp_ref, vt_ref, vs_ref, wb_ref, *, n_tiles):
    _round_weight_once(w_ref, wb_ref)

    def values(x_ref):
        xn = _rmsnorm(x_ref[...], g_ref[...]).astype(BF16)
        return jnp.dot(xn, wb_ref[...], preferred_element_type=F32)

    def prompt_rows():
        v = values(xp_ref)
        vp_ref[...] = v
        for h in range(v.shape[1] // HEAD_DIM):
            for r in range(v.shape[0] // MOBA_BLOCK):
                blk = v[r * MOBA_BLOCK:(r + 1) * MOBA_BLOCK, h * HEAD_DIM:(h + 1) * HEAD_DIM]
                vt_ref[h, r] = blk.T.astype(BF16)

    def sample_rows():
        vs_ref[...] = values(xs_ref)

    _prompt_then_sample(n_tiles, prompt_rows, sample_rows)


def _v_proj(xp, xs, g, w_kv, *, tm, seq):
    m, d = xp.shape
    m_s = xs.shape[0]
    n_h = d // HEAD_DIM
    n_tiles = m // tm
    r = tm // MOBA_BLOCK
    tiles_per_seq = seq // tm
    tile, prompt, sample = _proj_specs(tm, d, m_s, n_tiles)
    vt_spec = pl.BlockSpec((None, n_h, r, HEAD_DIM, MOBA_BLOCK),
                           lambda i: (tile(i) // tiles_per_seq, 0, tile(i) % tiles_per_seq, 0, 0))
    return pl.pallas_call(
        functools.partial(_v_kernel, n_tiles=n_tiles),
        grid=(n_tiles + 1,),
        in_specs=[
            prompt(d), sample(d),
            pl.BlockSpec((1, d), lambda i: (0, 0)),
            pl.BlockSpec((d, d), lambda i: (0, 1), pipeline_mode=pl.Buffered(1)),
        ],
        out_specs=[prompt(d), vt_spec, sample(d)],
        out_shape=[jax.ShapeDtypeStruct((m, d), F32),
                   jax.ShapeDtypeStruct((m // seq, n_h, seq // MOBA_BLOCK, HEAD_DIM, MOBA_BLOCK), BF16),
                   jax.ShapeDtypeStruct((m_s, d), F32)],
        scratch_shapes=[pltpu.VMEM((d, d), BF16)],
        compiler_params=_params(1),
        name="v_proj",
    )(xp, xs, g.reshape(1, d), w_kv)


def _o_kernel(ap_ref, as_ref, w_ref, hp_ref, hs_ref, op_ref, os_ref, wb_ref, *, n_tiles):
    _round_weight_once(w_ref, wb_ref)

    def rows(a_ref, h_ref, o_ref):
        o_ref[...] = h_ref[...] + jnp.dot(a_ref[...].astype(BF16), wb_ref[...], preferred_element_type=F32)

    _prompt_then_sample(n_tiles, lambda: rows(ap_ref, hp_ref, op_ref), lambda: rows(as_ref, hs_ref, os_ref))


def _o_proj(ap, a_s, w, layer, hp, hs, *, tm):
    m, d = hp.shape
    m_s = hs.shape[0]
    n_tiles = m // tm
    _, prompt, sample = _proj_specs(tm, d, m_s, n_tiles)
    return pl.pallas_call(
        functools.partial(_o_kernel, n_tiles=n_tiles),
        grid=(n_tiles + 1,),
        in_specs=[
            prompt(d), sample(d),
            pl.BlockSpec((None, d, d), lambda i: (layer, 0, 0), pipeline_mode=pl.Buffered(1)),
            prompt(d), sample(d),
        ],
        out_specs=[prompt(d), sample(d)],
        out_shape=[jax.ShapeDtypeStruct((m, d), F32), jax.ShapeDtypeStruct((m_s, d), F32)],
        scratch_shapes=[pltpu.VMEM((d, d), BF16)],
        compiler_params=_params(1),
        name="o_proj",
    )(ap, a_s, w, hp, hs)


def _top_k_picks(gate, candidate, blk_f, axis):
    remaining = candidate
    for _ in range(MOBA_TOP_K):
        gm = jnp.where(remaining, gate, GATE_FLOOR)
        top = jnp.max(gm, axis=axis, keepdims=True)
        hit = jnp.logical_and(remaining, gm == top)
        idx = jnp.min(jnp.where(hit, blk_f, 1e9), axis=axis, keepdims=True)
        pick = blk_f == idx
        remaining = jnp.logical_and(remaining, jnp.logical_not(pick))
        yield pick, idx


def _attn_kernel(q_ref, k_ref, vt_ref, km_ref, o_ref, qt_ref, acc_ref, sa_ref, sb_ref):
    n_g, n_blk = qt_ref.shape[0], qt_ref.shape[1]
    tq = MOBA_BLOCK
    q_scale = HEAD_DIM ** -0.5 * 1.4426950408889634
    blk_i = lax.broadcasted_iota(jnp.int32, (n_blk, tq), 0)
    blk_f = blk_i.astype(F32)
    heads = [slice(g * HEAD_DIM, (g + 1) * HEAD_DIM) for g in range(n_g)]
    assert n_blk <= qt_ref.shape[2] - HEAD_DIM

    pad = jnp.zeros((qt_ref.shape[2] - HEAD_DIM - n_blk, tq), BF16)
    for g in range(n_g):
        km = km_ref[:, heads[g]]
        for i in range(n_blk):
            q_t = q_ref[i * tq:(i + 1) * tq, heads[g]].T
            qt_ref[g, i, 0:HEAD_DIM, :] = (q_t * q_scale).astype(BF16)
            if i <= MOBA_TOP_K:
                selected = blk_i < i
            else:
                gate_t = jnp.dot(km, q_t, preferred_element_type=F32, precision=lax.Precision.HIGHEST)
                selected = jnp.zeros((n_blk, tq), jnp.bool_)
                for pick, _ in _top_k_picks(gate_t, blk_i < i, blk_f, 0):
                    selected = jnp.logical_or(selected, pick)
            qt_ref[g, i, HEAD_DIM:HEAD_DIM + n_blk, :] = jnp.where(selected, 0.0, NEG_INF).astype(BF16)
            qt_ref[g, i, HEAD_DIM + n_blk:, :] = pad

    kpos = lax.broadcasted_iota(jnp.int32, (tq, tq), 0)
    qpos = lax.broadcasted_iota(jnp.int32, (tq, tq), 1)
    causal = kpos <= qpos

    last_pair = n_blk // 2 - 1
    pair_row = lax.broadcasted_iota(jnp.int32, (2 * tq, HEAD_DIM), 0)
    pair_lane = lax.broadcasted_iota(jnp.int32, (2 * tq, HEAD_DIM), 1)
    lane_minus_half = pair_lane - jnp.where(pair_row >= tq, 1, 0)
    sum_rows = acc_ref.shape[1] - HEAD_DIM
    ones_rows = jnp.where(lax.broadcasted_iota(jnp.int32, (sum_rows, tq), 0) == 0, 1.0, 0.0).astype(BF16)

    def values_aug(g, j):
        return jnp.concatenate([vt_ref[g, j], ones_rows], axis=0)

    def tile(i, _):
        start = pl.multiple_of(i * tq, tq)

        def score_pair(pair_idx, dst_ref):
            pair_c = jnp.minimum(pair_idx, last_pair)
            st = pl.multiple_of(pair_c * (2 * tq), 2 * tq)
            onehot = jnp.where(lane_minus_half == 2 * pair_c, 1.0, 0.0).astype(BF16)
            for g in range(n_g):
                keys_aug = jnp.concatenate([k_ref[pl.ds(st, 2 * tq), heads[g]], onehot], axis=1)
                dst_ref[g] = jnp.dot(keys_aug, qt_ref[g, i], preferred_element_type=F32)

        def fold_pair(src_ref, pair_idx, ms):
            j0 = 2 * jnp.minimum(pair_idx, last_pair)
            new_ms, alphas, ps = [], [], []
            for g in range(n_g):
                sa = src_ref[g, 0:tq, :]
                sb = src_ref[g, tq:2 * tq, :]
                m_blk = jnp.maximum(jnp.max(sa, axis=0, keepdims=True), jnp.max(sb, axis=0, keepdims=True))
                m_new = jnp.maximum(ms[g], m_blk)
                new_ms.append(m_new)
                alphas.append(jnp.exp2(ms[g] - m_new))
                ps.append((jnp.exp2(sa - m_new).astype(BF16), jnp.exp2(sb - m_new).astype(BF16)))
            for g in range(n_g):
                pv = (jnp.dot(values_aug(g, j0), ps[g][0], preferred_element_type=F32)
                      + jnp.dot(values_aug(g, j0 + 1), ps[g][1], preferred_element_type=F32))
                acc_ref[g] = alphas[g] * acc_ref[g] + pv
            return tuple(new_ms)

        ms = []
        s_own = [jnp.dot(k_ref[pl.ds(start, tq), heads[g]], qt_ref[g, i, 0:HEAD_DIM, :], preferred_element_type=F32)
                 for g in range(n_g)]
        score_pair(0, sa_ref)
        p_own = []
        for g in range(n_g):
            s = jnp.where(causal, s_own[g], NEG_INF)
            m0 = jnp.max(s, axis=0, keepdims=True)
            ms.append(m0)
            p_own.append(jnp.exp2(s - m0).astype(BF16))
        for g in range(n_g):
            acc_ref[g] = jnp.dot(values_aug(g, i), p_own[g], preferred_element_type=F32)

        def two_pairs(t, ms):
            score_pair(2 * t + 1, sb_ref)
            ms = fold_pair(sa_ref, 2 * t, ms)
            score_pair(2 * t + 2, sa_ref)
            return fold_pair(sb_ref, 2 * t + 1, ms)

        n_pairs = (i + 1) // 2
        ms = lax.fori_loop(0, n_pairs // 2, two_pairs, tuple(ms))

        @pl.when(n_pairs % 2 == 1)
        def _():
            fold_pair(sa_ref, n_pairs - 1, ms)

        for g in range(n_g):
            inv_l = 1.0 / acc_ref[g, HEAD_DIM:HEAD_DIM + 1, :]
            o_ref[pl.ds(start, tq), heads[g]] = (acc_ref[g, 0:HEAD_DIM, :] * inv_l).T.astype(o_ref.dtype)
        return 0

    lax.fori_loop(0, n_blk, tile, 0)


ATTN_HEADS_PER_STEP = 4


ATTN_SUM_ROWS = 16


def _moba_prompt(q, k_b, vt_b, kmean, *, batch, seq):
    m, d = q.shape
    n_g = ATTN_HEADS_PER_STEP
    n_blk = seq // MOBA_BLOCK
    width = n_g * HEAD_DIM
    return pl.pallas_call(
        _attn_kernel,
        grid=(batch, d // width),
        in_specs=[
            pl.BlockSpec((seq, width), lambda b, h: (b, h)),
            pl.BlockSpec((seq, width), lambda b, h: (b, h)),
            pl.BlockSpec((None, n_g, n_blk, HEAD_DIM, MOBA_BLOCK), lambda b, h: (b, h, 0, 0, 0)),
            pl.BlockSpec((None, n_blk, width), lambda b, h: (b, 0, h)),
        ],
        out_specs=pl.BlockSpec((seq, width), lambda b, h: (b, h), pipeline_mode=pl.Buffered(1)),
        out_shape=jax.ShapeDtypeStruct((m, d), BF16),
        scratch_shapes=[
            pltpu.VMEM((n_g, n_blk, 2 * HEAD_DIM, MOBA_BLOCK), BF16),
            pltpu.VMEM((n_g, HEAD_DIM + ATTN_SUM_ROWS, MOBA_BLOCK), F32),
            pltpu.VMEM((n_g, 2 * MOBA_BLOCK, MOBA_BLOCK), F32),
            pltpu.VMEM((n_g, 2 * MOBA_BLOCK, MOBA_BLOCK), F32),
        ],
        compiler_params=_params(2),
        name="moba_prompt",
    )(q, k_b, vt_b, kmean)


MEAN_BLOCKS_PER_STEP = 4


def _block_mean_kernel(pt_ref, *refs):
    del pt_ref
    page_refs, o_ref = refs[:-1], refs[-1]
    for n in range(MEAN_BLOCKS_PER_STEP):
        m0 = jnp.mean(page_refs[PAGES_PER_BLOCK * n][...], axis=0)
        m1 = jnp.mean(page_refs[PAGES_PER_BLOCK * n + 1][...], axis=0)
        o_ref[n] = (m0 + m1) * 0.5


def _cache_block_means(cache_k, page_table_flat, *, n_seq, n_pages):
    _, page, n_h, dh = cache_k.shape
    n_blk = n_pages // PAGES_PER_BLOCK
    pages_per_step = MEAN_BLOCKS_PER_STEP * PAGES_PER_BLOCK

    def page_spec(p):
        return pl.BlockSpec((None, page, n_h, dh), lambda b, n, pt: (pt[b * n_pages + pages_per_step * n + p], 0, 0, 0))

    grid_spec = pltpu.PrefetchScalarGridSpec(
        num_scalar_prefetch=1,
        grid=(n_seq, n_blk // MEAN_BLOCKS_PER_STEP),
        in_specs=[page_spec(p) for p in range(pages_per_step)],
        out_specs=pl.BlockSpec((None, MEAN_BLOCKS_PER_STEP, n_h, dh), lambda b, n, pt: (b, n, 0, 0)),
    )
    return pl.pallas_call(
        _block_mean_kernel,
        grid_spec=grid_spec,
        out_shape=jax.ShapeDtypeStruct((n_seq, n_blk, n_h, dh), F32),
        compiler_params=_params(2),
        name="cache_block_means",
    )(page_table_flat, *([cache_k] * pages_per_step))


def _select_kernel(q_ref, bm_ref, sel_ref, *, n_valid):
    q = q_ref[...]
    t = q.shape[0]
    n_lane = bm_ref.shape[1]
    lane = lax.broadcasted_iota(jnp.int32, (t, n_lane), 1)
    lane_f = lane.astype(F32)
    out = jnp.zeros((t, n_lane), jnp.int32)
    for h in range(q.shape[1] // HEAD_DIM):
        cols = slice(h * HEAD_DIM, (h + 1) * HEAD_DIM)
        gate = lax.dot_general(q[:, cols], bm_ref[h], (((1,), (1,)), ((), ())),
                               preferred_element_type=F32, precision=lax.Precision.HIGHEST)
        for r, (_, idx) in enumerate(_top_k_picks(gate, lane < n_valid, lane_f, 1)):
            out = jnp.where(lane == h * MOBA_TOP_K + r, idx.astype(jnp.int32), out)
    sel_ref[...] = out


def _select_blocks(q, bm_pad, *, n_seq, n_valid):
    m, d = q.shape
    t = m // n_seq
    _, n_h, n_lane, dh = bm_pad.shape
    return pl.pallas_call(
        functools.partial(_select_kernel, n_valid=n_valid),
        grid=(n_seq,),
        in_specs=[
            pl.BlockSpec((t, d), lambda b: (b, 0)),
            pl.BlockSpec((None, n_h, n_lane, dh), lambda b: (b, 0, 0, 0)),
        ],
        out_specs=pl.BlockSpec((None, t, n_lane), lambda b: (b, 0, 0)),
        out_shape=jax.ShapeDtypeStruct((n_seq, t, n_lane), jnp.int32),
        compiler_params=_params(1),
        name="select_blocks",
    )(q, bm_pad)


def _sample_attn_kernel(sel_ref, pt_ref, q_ref, kn_ref, vn_ref, ck_hbm, cv_hbm, o_ref, kg_ref, vg_ref, sem,
                        *, n_pages):
    b = pl.program_id(0)
    h = pl.program_id(1)
    n_h = pl.num_programs(1)
    n_steps = pl.num_programs(0) * n_h
    step = b * n_h + h
    slot = step % 2
    t_len = q_ref.shape[0]
    n_sel = MOBA_TOP_K * MOBA_BLOCK
    scale = HEAD_DIM ** -0.5

    def gather_copies(bb, hh, sl):
        copies = []
        for t in range(t_len):
            for r in range(MOBA_TOP_K):
                blk = sel_ref[((bb * t_len + t) * n_h + hh) * MOBA_TOP_K + r]
                for pg in range(PAGES_PER_BLOCK):
                    phys = pt_ref[bb * n_pages + blk * PAGES_PER_BLOCK + pg]
                    rows = pl.ds((r * PAGES_PER_BLOCK + pg) * PAGE_SIZE, PAGE_SIZE)
                    copies.append(pltpu.make_async_copy(ck_hbm.at[phys, :, hh, :], kg_ref.at[sl, t, rows, :], sem.at[sl, 0]))
                    copies.append(pltpu.make_async_copy(cv_hbm.at[phys, :, hh, :], vg_ref.at[sl, t, rows, :], sem.at[sl, 1]))
        return copies

    @pl.when(step == 0)
    def _():
        for cp in gather_copies(b, h, slot):
            cp.start()

    @pl.when(step + 1 < n_steps)
    def _():
        nxt = step + 1
        for cp in gather_copies(nxt // n_h, nxt % n_h, 1 - slot):
            cp.start()

    for cp in gather_copies(b, h, slot):
        cp.wait()

    q = q_ref[...]
    k_own = kn_ref[...]
    v_own = vn_ref[...]
    own_i = lax.broadcasted_iota(jnp.int32, (t_len, 1), 0)
    rows = []
    for t in range(t_len):
        q_t = q[t:t + 1, :]
        s_sel = jnp.sum(kg_ref[slot, t] * q_t, axis=1, keepdims=True) * scale
        s_o = jnp.sum(k_own * q_t, axis=1, keepdims=True) * scale
        s_o = jnp.where(own_i <= t, s_o, NEG_INF)
        m = jnp.maximum(jnp.max(s_sel, axis=0, keepdims=True), jnp.max(s_o, axis=0, keepdims=True))
        p_sel = jnp.exp(s_sel - m)
        p_o = jnp.exp(s_o - m)
        l = jnp.sum(p_sel, axis=0, keepdims=True) + jnp.sum(p_o, axis=0, keepdims=True)
        out = (jnp.sum(p_sel * vg_ref[slot, t], axis=0, keepdims=True)
               + jnp.sum(p_o * v_own, axis=0, keepdims=True))
        rows.append(out / l)
    o_ref[...] = jnp.concatenate(rows, axis=0)


def _moba_sample(q, k_new, v_new, cache_k3, cache_v3, sel_flat, page_table_flat, *, n_seq, n_pages):
    m, d = q.shape
    t = m // n_seq
    n_h = d // HEAD_DIM
    n_sel = MOBA_TOP_K * MOBA_BLOCK
    row_spec = pl.BlockSpec((t, HEAD_DIM), lambda b, h, sel, pt: (b, h))
    grid_spec = pltpu.PrefetchScalarGridSpec(
        num_scalar_prefetch=2,
        grid=(n_seq, n_h),
        in_specs=[row_spec, row_spec, row_spec, pl.BlockSpec(memory_space=pl.ANY), pl.BlockSpec(memory_space=pl.ANY)],
        out_specs=row_spec,
        scratch_shapes=[
            pltpu.VMEM((2, t, n_sel, HEAD_DIM), F32),
            pltpu.VMEM((2, t, n_sel, HEAD_DIM), F32),
            pltpu.SemaphoreType.DMA((2, 2)),
        ],
    )
    return pl.pallas_call(
        functools.partial(_sample_attn_kernel, n_pages=n_pages),
        grid_spec=grid_spec,
        out_shape=jax.ShapeDtypeStruct((m, d), F32),
        compiler_params=_params(2),
        name="moba_sample",
    )(sel_flat, page_table_flat, q, k_new, v_new, cache_k3, cache_v3)


def _rope_tables(pos):
    half = HEAD_DIM // 2
    inv = 1.0 / (ROPE_THETA ** (jnp.arange(half, dtype=F32) * (2.0 / HEAD_DIM)))
    ang = pos.astype(F32)[:, None] * inv[None, :]
    cos = jnp.cos(ang)
    sin = jnp.sin(ang)
    return jnp.concatenate([cos, cos], axis=-1), jnp.concatenate([-sin, sin], axis=-1)


def kernel(x_prompt, x_sample, state_pool, cache_k, cache_v, page_table, g_pool, w_pool, s_pool, g_ffn, w_gate_up,
           w_down, g_kv, w_kv, g_attn, w_q, w_o, g_final):
    n_b, seq, d = x_prompt.shape
    n_db, t_dec, _ = x_sample.shape
    depth = g_ffn.shape[0]
    n_pool = g_pool.shape[0]
    n_h = d // HEAD_DIM
    n_pages = page_table.shape[1]
    n_past_blk = PAST_LEN // MOBA_BLOCK
    m_p = n_b * seq
    m_s = n_db * t_dec
    state_rows = state_pool.shape[2]
    assert n_pages == n_past_blk * PAGES_PER_BLOCK, "own MoBA block must hold only the new tokens"
    assert seq % MOBA_BLOCK == 0 and state_rows == POOL_HALO - 1

    tm_pool, tm_ffn, tf, tm_proj = 512, 512, 512, 512

    w_gu_b = w_gate_up.astype(BF16)
    w_dn_b = w_down.astype(BF16)

    cos_p, sin_p = _rope_tables(jnp.arange(seq))
    cos_s, sin_s = _rope_tables(PAST_LEN + jnp.arange(t_dec))
    cos_s = jnp.tile(cos_s, (n_db, 1))
    sin_s = jnp.tile(sin_s, (n_db, 1))

    pt_flat = page_table.reshape(-1)

    hp = x_prompt
    hs = x_sample
    pool_p, pool_s = [], []
    for layer in range(depth):
        last = layer == depth - 1
        if layer < n_pool:
            hp3, zt_p = _pool_layer(hp.reshape(n_b, seq, d), hp.reshape(n_b, seq, d), g_pool[layer], w_pool[layer],
                                    s_pool[layer], tm=tm_pool, halo_normalized=False, pos0=0)
            state = state_pool[layer]
            halo_s = jnp.pad(state, ((0, 0), (POOL_HALO - state_rows, 0), (0, 0)))
            hs3, zt_s = _pool_layer(hs.reshape(n_db, t_dec, d), halo_s, g_pool[layer], w_pool[layer], s_pool[layer],
                                    tm=t_dec, halo_normalized=True, pos0=PAST_LEN)
            pool_p.append(zt_p[:, POOL_HALO - state_rows:])
            pool_s.append(jnp.concatenate([state, zt_s], axis=1)[:, -state_rows:])
            hp = hp3.reshape(m_p, d)
            hs = hs3.reshape(m_s, d)
        else:
            a = layer - n_pool
            if a == 0:
                k_p, k_pb, kmean, k_s = _k_proj(hp, hs, g_kv, w_kv, cos_p, sin_p, cos_s, sin_s, tm=tm_proj)
                v_p, vt_pb, v_s = _v_proj(hp, hs, g_kv, w_kv, tm=tm_proj, seq=seq)
                kmean = kmean.reshape(n_b, seq // MOBA_BLOCK, d)
                bm = _cache_block_means(cache_k, pt_flat, n_seq=n_db, n_pages=n_pages)
                bm_pad = jnp.pad(bm.transpose(0, 2, 1, 3), ((0, 0), (0, 0), (0, 128 - n_past_blk), (0, 0)))
            q_p, q_s = _q_proj(hp, hs, g_attn[a], w_q, a, cos_p, sin_p, cos_s, sin_s, tm=tm_proj)
            att_p = _moba_prompt(q_p, k_pb, vt_pb, kmean, batch=n_b, seq=seq)
            sel = _select_blocks(q_s, bm_pad, n_seq=n_db, n_valid=n_past_blk)
            sel_flat = sel[:, :, :n_h * MOBA_TOP_K].reshape(-1)
            att_s = _moba_sample(q_s, k_s, v_s, cache_k, cache_v, sel_flat, pt_flat, n_seq=n_db, n_pages=n_pages)
            hp, hs = _o_proj(att_p, att_s, w_o, a, hp, hs, tm=tm_proj)
        hp, hs = _ffn(hp, hs, g_ffn[layer], w_gu_b, w_dn_b, layer, g_final, tm=tm_ffn, tf=tf, final_norm=last)

    y_prompt = hp.reshape(n_b, seq, d)
    y_sample = hs.reshape(n_db, t_dec, d)
    new_pool_prompt = jnp.stack(pool_p)
    new_pool_sample = jnp.stack(pool_s)
    shape_p = (n_b, seq, n_h, HEAD_DIM)
    shape_s = (n_db, t_dec, n_h, HEAD_DIM)
    return (y_prompt, y_sample, new_pool_prompt, new_pool_sample, k_p.reshape(shape_p), v_p.reshape(shape_p),
            k_s.reshape(shape_s), v_s.reshape(shape_s))
```

```python
import functools

import jax
import jax.numpy as jnp
from jax import lax
from jax.experimental import pallas as pl
from jax.experimental.pallas import tpu as pltpu

F32 = jnp.float32
BF16 = jnp.bfloat16

POOL_WINDOWS = (2, 4, 8, 16)
POOL_HALO = 16
HEAD_DIM = 128
MOBA_BLOCK = 256
MOBA_TOP_K = 3
PAGE_SIZE = 128
PAGES_PER_BLOCK = MOBA_BLOCK // PAGE_SIZE
PAST_LEN = 16384
ROPE_THETA = 10000.0
RMS_EPS = 1e-6
NEG_INF = -1e30
GATE_FLOOR = -3.0e38

VMEM_LIMIT_BYTES = 56 * 1024 * 1024


def _params(n_axes):
    return pltpu.CompilerParams(dimension_semantics=("arbitrary",) * n_axes, vmem_limit_bytes=VMEM_LIMIT_BYTES)


def _rmsnorm(x, g):
    ms = jnp.mean(x * x, axis=-1, keepdims=True)
    return x * lax.rsqrt(ms + RMS_EPS) * g


def _pool_kernel(x_ref, halo_ref, g_ref, w_ref, s_ref, h_ref, zt_ref, zs_ref, *, tm, halo_normalized, pos0):
    i = pl.program_id(1)
    d = x_ref.shape[-1]
    c = d // len(POOL_WINDOWS)
    x = x_ref[...]
    g = g_ref[...]
    z = _rmsnorm(x, g)
    if halo_normalized:
        zh = halo_ref[...]
    else:
        zh = _rmsnorm(halo_ref[...], g) * (i > 0).astype(F32)
    zs_ref[0:POOL_HALO, :] = zh
    zs_ref[POOL_HALO:POOL_HALO + tm, :] = z
    pos = lax.broadcasted_iota(jnp.int32, (tm, c), 0) + (i * tm + pos0)
    for gi, w in enumerate(POOL_WINDOWS):
        cols = slice(gi * c, (gi + 1) * c)
        zc = z[:, cols]
        tot = zc
        for j in range(1, w):
            tot = tot + zs_ref[POOL_HALO - j:POOL_HALO - j + tm, cols]
        cnt = jnp.minimum(pos + 1, w).astype(F32)
        pooled = tot / cnt - zc
        mixed = jnp.dot(pooled.astype(BF16), w_ref[gi].astype(BF16), preferred_element_type=F32)
        h_ref[:, cols] = x[:, cols] + mixed * s_ref[:, cols]

    nt = zt_ref.shape[0]

    @pl.when(i == pl.num_programs(1) - 1)
    def _():
        zt_ref[...] = z[tm - nt:, :]


def _pool_layer(x3, halo3, g, w_b, s, *, tm, halo_normalized, pos0):
    nb, length, d = x3.shape
    nt = min(POOL_HALO, tm)
    n_i = length // tm
    hb = tm // POOL_HALO
    if halo_normalized:
        halo_map = lambda b, i: (b, 0, 0)
    else:
        halo_map = lambda b, i: (b, jnp.maximum(i * hb - 1, 0), 0)
    kern = functools.partial(_pool_kernel, tm=tm, halo_normalized=halo_normalized, pos0=pos0)
    return pl.pallas_call(
        kern,
        grid=(nb, n_i),
        in_specs=[
            pl.BlockSpec((None, tm, d), lambda b, i: (b, i, 0)),
            pl.BlockSpec((None, POOL_HALO, d), halo_map),
            pl.BlockSpec((1, d), lambda b, i: (0, 0)),
            pl.BlockSpec(w_b.shape, lambda b, i: (0, 0, 0)),
            pl.BlockSpec((1, d), lambda b, i: (0, 0)),
        ],
        out_specs=[
            pl.BlockSpec((None, tm, d), lambda b, i: (b, i, 0)),
            pl.BlockSpec((None, nt, d), lambda b, i: (b, 0, 0)),
        ],
        out_shape=[jax.ShapeDtypeStruct((nb, length, d), F32), jax.ShapeDtypeStruct((nb, nt, d), F32)],
        scratch_shapes=[pltpu.VMEM((POOL_HALO + tm, d), F32)],
        compiler_params=_params(2),
        name="pool_layer",
    )(x3, halo3, g.reshape(1, d), w_b, s.reshape(1, d))


def _stream_block_means(pt_ref, cache_hbm, bm_ref, page_buf, page_sem, *, first_page, n_call_pages):
    step = pl.program_id(0) * pl.num_programs(1) + pl.program_id(1)
    pages_per_step = page_buf.shape[1]
    blocks_per_step = pages_per_step // PAGES_PER_BLOCK
    n_fetch = n_call_pages // pages_per_step

    def page_copy(at_step, p, slot):
        phys = pt_ref[first_page + at_step * pages_per_step + p]
        return pltpu.make_async_copy(cache_hbm.at[phys], page_buf.at[slot, p], page_sem.at[slot])

    def start_step(at_step, slot):
        for p in range(pages_per_step):
            page_copy(at_step, p, slot).start()

    @pl.when(step == 0)
    def _():
        start_step(0, 0)

    @pl.when(step + 1 < n_fetch)
    def _():
        start_step(step + 1, (step + 1) % 2)

    @pl.when(step < n_fetch)
    def _():
        for p in range(pages_per_step):
            page_copy(step, p, step % 2).wait()

    def add_pages():
        src = jnp.minimum(step, n_fetch - 1)
        slot = src % 2
        for b in range(blocks_per_step):
            rows = sum(jnp.sum(page_buf[slot, PAGES_PER_BLOCK * b + pg], axis=0) for pg in range(PAGES_PER_BLOCK))
            bm_ref[src * blocks_per_step + b] = rows * (1.0 / (PAGES_PER_BLOCK * PAGE_SIZE))

    return add_pages


def _ffn_kernel(*refs, final_norm, means):
    if means is None:
        x_ref, xs_ref, g_ref, wg_ref, wu_ref, wd_ref, go_ref, o_ref, os_ref, xn_ref, xns_ref = refs
        side_work = None
    else:
        (pt_ref, x_ref, xs_ref, g_ref, wg_ref, wu_ref, wd_ref, go_ref, cache_hbm, o_ref, os_ref, bm_ref, xn_ref, xns_ref,
         page_buf, page_sem) = refs
        side_work = _stream_block_means(pt_ref, cache_hbm, bm_ref, page_buf, page_sem, **means)
    j = pl.program_id(1)

    def rows(x_ref, o_ref, xn_ref, side_work=None):
        @pl.when(j == 0)
        def _():
            x = x_ref[...]
            xn_ref[...] = _rmsnorm(x, g_ref[...]).astype(BF16)
            o_ref[...] = x

        xn = xn_ref[...]
        gate = jnp.dot(xn, wg_ref[...], preferred_element_type=F32)
        up = jnp.dot(xn, wu_ref[...], preferred_element_type=F32)
        act = (gate * jax.nn.sigmoid(gate) * up).astype(BF16)
        o_ref[...] += jnp.dot(act, wd_ref[...], preferred_element_type=F32)
        if side_work is not None:
            side_work()

        if final_norm:
            @pl.when(j == pl.num_programs(1) - 1)
            def _():
                o_ref[...] = _rmsnorm(o_ref[...], go_ref[...])

    rows(x_ref, o_ref, xn_ref, side_work)

    @pl.when(pl.program_id(0) == pl.num_programs(0) - 1)
    def _():
        rows(xs_ref, os_ref, xns_ref)


def _ffn(x, xs, g, w_gu_b, w_dn_b, layer, g_out, *, tm, tf, final_norm, cache_means=None):
    m, d = x.shape
    m_s = xs.shape[0]
    f = w_dn_b.shape[1]
    n_f = f // tf
    grid = (m // tm, n_f)
    in_specs = [
        pl.BlockSpec((tm, d), lambda i, j, *_: (i, 0)),
        pl.BlockSpec((m_s, d), lambda i, j, *_: (0, 0)),
        pl.BlockSpec((1, d), lambda i, j, *_: (0, 0)),
        pl.BlockSpec((None, d, tf), lambda i, j, *_: (layer, 0, j)),
        pl.BlockSpec((None, d, tf), lambda i, j, *_: (layer, 0, j + n_f)),
        pl.BlockSpec((None, tf, d), lambda i, j, *_: (layer, j, 0)),
        pl.BlockSpec((1, d), lambda i, j, *_: (0, 0)),
    ]
    out_specs = [pl.BlockSpec((tm, d), lambda i, j, *_: (i, 0)), pl.BlockSpec((m_s, d), lambda i, j, *_: (0, 0))]
    out_shape = [jax.ShapeDtypeStruct((m, d), F32), jax.ShapeDtypeStruct((m_s, d), F32)]
    scratch = [pltpu.VMEM((tm, d), BF16), pltpu.VMEM((m_s, d), BF16)]
    operands = [x, xs, g.reshape(1, d), w_gu_b, w_gu_b, w_dn_b, g_out.reshape(1, d)]
    if cache_means is None:
        means = None
        prefetch = []
    else:
        cache_k, page_table_flat, first_page, n_call_pages = cache_means
        _, page, n_h, dh = cache_k.shape
        blocks_per_step = -(-n_call_pages // (PAGES_PER_BLOCK * grid[0] * grid[1]))
        while n_call_pages % (PAGES_PER_BLOCK * blocks_per_step):
            blocks_per_step += 1
        pages_per_step = PAGES_PER_BLOCK * blocks_per_step
        means = dict(first_page=first_page, n_call_pages=n_call_pages)
        prefetch = [page_table_flat]
        in_specs.append(pl.BlockSpec(memory_space=pl.ANY))
        operands.append(cache_k)
        n_call_blk = n_call_pages // PAGES_PER_BLOCK
        out_specs.append(pl.BlockSpec((n_call_blk, n_h, dh), lambda i, j, *_: (0, 0, 0)))
        out_shape.append(jax.ShapeDtypeStruct((n_call_blk, n_h, dh), F32))
        scratch += [pltpu.VMEM((2, pages_per_step, page, n_h, dh), F32), pltpu.SemaphoreType.DMA((2,))]
    grid_spec = pltpu.PrefetchScalarGridSpec(
        num_scalar_prefetch=len(prefetch), grid=grid, in_specs=in_specs, out_specs=out_specs, scratch_shapes=scratch)
    return pl.pallas_call(
        functools.partial(_ffn_kernel, final_norm=final_norm, means=means),
        grid_spec=grid_spec,
        out_shape=out_shape,
        compiler_params=_params(2),
        name="swiglu",
    )(*prefetch, *operands)


def _rope(xh, cos, sin_signed):
    return xh * cos + pltpu.roll(xh, HEAD_DIM // 2, axis=1) * sin_signed


def _round_weight_once(w_ref, wb_ref):
    @pl.when(pl.program_id(0) == 0)
    def _():
        wb_ref[...] = w_ref[...].astype(BF16)


def _prompt_then_sample(n_tiles, prompt_fn, sample_fn):
    i = pl.program_id(0)
    pl.when(i < n_tiles)(prompt_fn)
    pl.when(i == n_tiles)(sample_fn)


def _proj_specs(tm, d, m_s, n_tiles):
    del d
    tile = lambda i: jnp.minimum(i, n_tiles - 1)
    prompt = lambda width: pl.BlockSpec((tm, width), lambda i: (tile(i), 0))
    sample = lambda width: pl.BlockSpec((m_s, width), lambda i: (0, 0))
    return tile, prompt, sample


def _q_kernel(xp_ref, xs_ref, g_ref, w_ref, cosp_ref, sinp_ref, coss_ref, sins_ref, qp_ref, qs_ref, wb_ref, *, n_tiles):
    _round_weight_once(w_ref, wb_ref)

    def rows(x_ref, cos_ref, sin_ref, q_ref):
        xn = _rmsnorm(x_ref[...], g_ref[...]).astype(BF16)
        acc = jnp.dot(xn, wb_ref[...], preferred_element_type=F32)
        cos = cos_ref[...]
        sin = sin_ref[...]
        for h in range(acc.shape[1] // HEAD_DIM):
            cols = slice(h * HEAD_DIM, (h + 1) * HEAD_DIM)
            q_ref[:, cols] = _rope(acc[:, cols], cos, sin)

    _prompt_then_sample(n_tiles, lambda: rows(xp_ref, cosp_ref, sinp_ref, qp_ref),
                        lambda: rows(xs_ref, coss_ref, sins_ref, qs_ref))


def _q_proj(xp, xs, g, w, layer, cos_p, sin_p, cos_s, sin_s, *, tm):
    m, d = xp.shape
    m_s = xs.shape[0]
    n_tiles = m // tm
    n_pos = cos_p.shape[0] // tm
    tile, prompt, sample = _proj_specs(tm, d, m_s, n_tiles)
    rope_p = pl.BlockSpec((tm, HEAD_DIM), lambda i: (tile(i) % n_pos, 0))
    return pl.pallas_call(
        functools.partial(_q_kernel, n_tiles=n_tiles),
        grid=(n_tiles + 1,),
        in_specs=[
            prompt(d), sample(d),
            pl.BlockSpec((1, d), lambda i: (0, 0)),
            pl.BlockSpec((None, d, d), lambda i: (layer, 0, 0), pipeline_mode=pl.Buffered(1)),
            rope_p, rope_p, sample(HEAD_DIM), sample(HEAD_DIM),
        ],
        out_specs=[prompt(d), sample(d)],
        out_shape=[jax.ShapeDtypeStruct((m, d), F32), jax.ShapeDtypeStruct((m_s, d), F32)],
        scratch_shapes=[pltpu.VMEM((d, d), BF16)],
        compiler_params=_params(1),
        name="q_proj",
    )(xp, xs, g.reshape(1, d), w, cos_p, sin_p, cos_s, sin_s)


def _k_kernel(xp_ref, xs_ref, g_ref, w_ref, cosp_ref, sinp_ref, coss_ref, sins_ref, kp_ref, kb_ref, km_ref, ks_ref,
              wb_ref, *, n_tiles):
    _round_weight_once(w_ref, wb_ref)

    def rows(x_ref, cos_ref, sin_ref, k_ref):
        xn = _rmsnorm(x_ref[...], g_ref[...]).astype(BF16)
        acc = jnp.dot(xn, wb_ref[...], preferred_element_type=F32)
        cos = cos_ref[...]
        sin = sin_ref[...]
        for h in range(acc.shape[1] // HEAD_DIM):
            cols = slice(h * HEAD_DIM, (h + 1) * HEAD_DIM)
            k_ref[:, cols] = _rope(acc[:, cols], cos, sin)

    def prompt_rows():
        rows(xp_ref, cosp_ref, sinp_ref, kp_ref)
        k = kp_ref[...]
        kb_ref[...] = k.astype(BF16)
        for r in range(k.shape[0] // MOBA_BLOCK):
            km_ref[r] = jnp.mean(k[r * MOBA_BLOCK:(r + 1) * MOBA_BLOCK, :], axis=0, keepdims=True)

    _prompt_then_sample(n_tiles, prompt_rows, lambda: rows(xs_ref, coss_ref, sins_ref, ks_ref))


def _k_proj(xp, xs, g, w_kv, cos_p, sin_p, cos_s, sin_s, *, tm):
    m, d = xp.shape
    m_s = xs.shape[0]
    n_tiles = m // tm
    n_pos = cos_p.shape[0] // tm
    r = tm // MOBA_BLOCK
    tile, prompt, sample = _proj_specs(tm, d, m_s, n_tiles)
    rope_p = pl.BlockSpec((tm, HEAD_DIM), lambda i: (tile(i) % n_pos, 0))
    return pl.pallas_call(
        functools.partial(_k_kernel, n_tiles=n_tiles),
        grid=(n_tiles + 1,),
        in_specs=[
            prompt(d), sample(d),
            pl.BlockSpec((1, d), lambda i: (0, 0)),
            pl.BlockSpec((d, d), lambda i: (0, 0), pipeline_mode=pl.Buffered(1)),
            rope_p, rope_p, sample(HEAD_DIM), sample(HEAD_DIM),
        ],
        out_specs=[prompt(d), prompt(d), pl.BlockSpec((r, 1, d), lambda i: (tile(i), 0, 0)), sample(d)],
        out_shape=[jax.ShapeDtypeStruct((m, d), F32), jax.ShapeDtypeStruct((m, d), BF16),
                   jax.ShapeDtypeStruct((m // MOBA_BLOCK, 1, d), F32), jax.ShapeDtypeStruct((m_s, d), F32)],
        scratch_shapes=[pltpu.VMEM((d, d), BF16)],
        compiler_params=_params(1),
        name="k_proj",
    )(xp, xs, g.reshape(1, d), w_kv, cos_p, sin_p, cos_s, sin_s)


def _v_kernel(xp_ref, xs_ref, g_ref, w_ref, vp_ref, vt_ref, vs_ref, wb_ref, *, n_tiles):
    _round_weight_once(w_ref, wb_ref)

    def values(x_ref):
        xn = _rmsnorm(x_ref[...], g_ref[...]).astype(BF16)
        return jnp.dot(xn, wb_ref[...], preferred_element_type=F32)

    def prompt_rows():
        v = values(xp_ref)
        vp_ref[...] = v
        for h in range(v.shape[1] // HEAD_DIM):
            for r in range(v.shape[0] // MOBA_BLOCK):
                blk = v[r * MOBA_BLOCK:(r + 1) * MOBA_BLOCK, h * HEAD_DIM:(h + 1) * HEAD_DIM]
                vt_ref[h, r] = blk.T.astype(BF16)

    def sample_rows():
        vs_ref[...] = values(xs_ref)

    _prompt_then_sample(n_tiles, prompt_rows, sample_rows)


def _v_proj(xp, xs, g, w_kv, *, tm, seq):
    m, d = xp.shape
    m_s = xs.shape[0]
    n_h = d // HEAD_DIM
    n_tiles = m // tm
    r = tm // MOBA_BLOCK
    tiles_per_seq = seq // tm
    tile, prompt, sample = _proj_specs(tm, d, m_s, n_tiles)
    vt_spec = pl.BlockSpec((None, n_h, r, HEAD_DIM, MOBA_BLOCK),
                           lambda i: (tile(i) // tiles_per_seq, 0, tile(i) % tiles_per_seq, 0, 0))
    return pl.pallas_call(
        functools.partial(_v_kernel, n_tiles=n_tiles),
        grid=(n_tiles + 1,),
        in_specs=[
            prompt(d), sample(d),
            pl.BlockSpec((1, d), lambda i: (0, 0)),
            pl.BlockSpec((d, d), lambda i: (0, 1), pipeline_mode=pl.Buffered(1)),
        ],
        out_specs=[prompt(d), vt_spec, sample(d)],
        out_shape=[jax.ShapeDtypeStruct((m, d), F32),
                   jax.ShapeDtypeStruct((m // seq, n_h, seq // MOBA_BLOCK, HEAD_DIM, MOBA_BLOCK), BF16),
                   jax.ShapeDtypeStruct((m_s, d), F32)],
        scratch_shapes=[pltpu.VMEM((d, d), BF16)],
        compiler_params=_params(1),
        name="v_proj",
    )(xp, xs, g.reshape(1, d), w_kv)


def _o_kernel(ap_ref, as_ref, w_ref, hp_ref, hs_ref, op_ref, os_ref, wb_ref, *, n_tiles):
    _round_weight_once(w_ref, wb_ref)

    def rows(a_ref, h_ref, o_ref):
        o_ref[...] = h_ref[...] + jnp.dot(a_ref[...].astype(BF16), wb_ref[...], preferred_element_type=F32)

    _prompt_then_sample(n_tiles, lambda: rows(ap_ref, hp_ref, op_ref), lambda: rows(as_ref, hs_ref, os_ref))


def _o_proj(ap, a_s, w, layer, hp, hs, *, tm):
    m, d = hp.shape
    m_s = hs.shape[0]
    n_tiles = m // tm
    _, prompt, sample = _proj_specs(tm, d, m_s, n_tiles)
    return pl.pallas_call(
        functools.partial(_o_kernel, n_tiles=n_tiles),
        grid=(n_tiles + 1,),
        in_specs=[
            prompt(d), sample(d),
            pl.BlockSpec((None, d, d), lambda i: (layer, 0, 0), pipeline_mode=pl.Buffered(1)),
            prompt(d), sample(d),
        ],
        out_specs=[prompt(d), sample(d)],
        out_shape=[jax.ShapeDtypeStruct((m, d), F32), jax.ShapeDtypeStruct((m_s, d), F32)],
        scratch_shapes=[pltpu.VMEM((d, d), BF16)],
        compiler_params=_params(1),
        name="o_proj",
    )(ap, a_s, w, hp, hs)


def _top_k_picks(gate, candidate, blk_f, axis):
    remaining = candidate
    for _ in range(MOBA_TOP_K):
        gm = jnp.where(remaining, gate, GATE_FLOOR)
        top = jnp.max(gm, axis=axis, keepdims=True)
        hit = jnp.logical_and(remaining, gm == top)
        idx = jnp.min(jnp.where(hit, blk_f, 1e9), axis=axis, keepdims=True)
        pick = blk_f == idx
        remaining = jnp.logical_and(remaining, jnp.logical_not(pick))
        yield pick, idx


def _attn_kernel(q_ref, k_ref, vt_ref, km_ref, o_ref, qt_ref, acc_ref, sa_ref, sb_ref):
    n_g, n_blk = qt_ref.shape[0], qt_ref.shape[1]
    tq = MOBA_BLOCK
    q_scale = HEAD_DIM ** -0.5 * 1.4426950408889634
    blk_i = lax.broadcasted_iota(jnp.int32, (n_blk, tq), 0)
    blk_f = blk_i.astype(F32)
    heads = [slice(g * HEAD_DIM, (g + 1) * HEAD_DIM) for g in range(n_g)]
    assert n_blk <= qt_ref.shape[2] - HEAD_DIM

    pad = jnp.zeros((qt_ref.shape[2] - HEAD_DIM - n_blk, tq), BF16)
    for g in range(n_g):
        km = km_ref[:, heads[g]]
        for i in range(n_blk):
            q_t = q_ref[i * tq:(i + 1) * tq, heads[g]].T
            qt_ref[g, i, 0:HEAD_DIM, :] = (q_t * q_scale).astype(BF16)
            if i <= MOBA_TOP_K:
                selected = blk_i < i
            else:
                gate_t = jnp.dot(km, q_t, preferred_element_type=F32, precision=lax.Precision.HIGHEST)
                selected = jnp.zeros((n_blk, tq), jnp.bool_)
                for pick, _ in _top_k_picks(gate_t, blk_i < i, blk_f, 0):
                    selected = jnp.logical_or(selected, pick)
            qt_ref[g, i, HEAD_DIM:HEAD_DIM + n_blk, :] = jnp.where(selected, 0.0, NEG_INF).astype(BF16)
            qt_ref[g, i, HEAD_DIM + n_blk:, :] = pad

    kpos = lax.broadcasted_iota(jnp.int32, (tq, tq), 0)
    qpos = lax.broadcasted_iota(jnp.int32, (tq, tq), 1)
    causal = kpos <= qpos

    last_pair = n_blk // 2 - 1
    pair_row = lax.broadcasted_iota(jnp.int32, (2 * tq, HEAD_DIM), 0)
    pair_lane = lax.broadcasted_iota(jnp.int32, (2 * tq, HEAD_DIM), 1)
    lane_minus_half = pair_lane - jnp.where(pair_row >= tq, 1, 0)
    sum_rows = acc_ref.shape[1] - HEAD_DIM
    ones_rows = jnp.where(lax.broadcasted_iota(jnp.int32, (sum_rows, tq), 0) == 0, 1.0, 0.0).astype(BF16)

    def values_aug(g, j):
        return jnp.concatenate([vt_ref[g, j], ones_rows], axis=0)

    def tile(i, _):
        start = pl.multiple_of(i * tq, tq)

        def score_pair(pair_idx, dst_ref):
            pair_c = jnp.minimum(pair_idx, last_pair)
            st = pl.multiple_of(pair_c * (2 * tq), 2 * tq)
            onehot = jnp.where(lane_minus_half == 2 * pair_c, 1.0, 0.0).astype(BF16)
            for g in range(n_g):
                keys_aug = jnp.concatenate([k_ref[pl.ds(st, 2 * tq), heads[g]], onehot], axis=1)
                dst_ref[g] = jnp.dot(keys_aug, qt_ref[g, i], preferred_element_type=F32)

        def fold_pair(src_ref, pair_idx, ms):
            j0 = 2 * jnp.minimum(pair_idx, last_pair)
            new_ms, alphas, ps = [], [], []
            for g in range(n_g):
                sa = src_ref[g, 0:tq, :]
                sb = src_ref[g, tq:2 * tq, :]
                m_blk = jnp.maximum(jnp.max(sa, axis=0, keepdims=True), jnp.max(sb, axis=0, keepdims=True))
                m_new = jnp.maximum(ms[g], m_blk)
                new_ms.append(m_new)
                alphas.append(jnp.exp2(ms[g] - m_new))
                ps.append((jnp.exp2(sa - m_new).astype(BF16), jnp.exp2(sb - m_new).astype(BF16)))
            for g in range(n_g):
                pv = (jnp.dot(values_aug(g, j0), ps[g][0], preferred_element_type=F32)
                      + jnp.dot(values_aug(g, j0 + 1), ps[g][1], preferred_element_type=F32))
                acc_ref[g] = alphas[g] * acc_ref[g] + pv
            return tuple(new_ms)

        ms = []
        s_own = [jnp.dot(k_ref[pl.ds(start, tq), heads[g]], qt_ref[g, i, 0:HEAD_DIM, :], preferred_element_type=F32)
                 for g in range(n_g)]
        score_pair(0, sa_ref)
        p_own = []
        for g in range(n_g):
            s = jnp.where(causal, s_own[g], NEG_INF)
            m0 = jnp.max(s, axis=0, keepdims=True)
            ms.append(m0)
            p_own.append(jnp.exp2(s - m0).astype(BF16))
        for g in range(n_g):
            acc_ref[g] = jnp.dot(values_aug(g, i), p_own[g], preferred_element_type=F32)

        def two_pairs(t, ms):
            score_pair(2 * t + 1, sb_ref)
            ms = fold_pair(sa_ref, 2 * t, ms)
            score_pair(2 * t + 2, sa_ref)
            return fold_pair(sb_ref, 2 * t + 1, ms)

        n_pairs = (i + 1) // 2
        ms = lax.fori_loop(0, n_pairs // 2, two_pairs, tuple(ms))

        @pl.when(n_pairs % 2 == 1)
        def _():
            fold_pair(sa_ref, n_pairs - 1, ms)

        for g in range(n_g):
            inv_l = 1.0 / acc_ref[g, HEAD_DIM:HEAD_DIM + 1, :]
            o_ref[pl.ds(start, tq), heads[g]] = (acc_ref[g, 0:HEAD_DIM, :] * inv_l).T.astype(o_ref.dtype)
        return 0

    lax.fori_loop(0, n_blk, tile, 0)


ATTN_HEADS_PER_STEP = 4


ATTN_SUM_ROWS = 16


def _moba_prompt(q, k_b, vt_b, kmean, *, batch, seq):
    m, d = q.shape
    n_g = ATTN_HEADS_PER_STEP
    n_blk = seq // MOBA_BLOCK
    width = n_g * HEAD_DIM
    return pl.pallas_call(
        _attn_kernel,
        grid=(batch, d // width),
        in_specs=[
            pl.BlockSpec((seq, width), lambda b, h: (b, h)),
            pl.BlockSpec((seq, width), lambda b, h: (b, h)),
            pl.BlockSpec((None, n_g, n_blk, HEAD_DIM, MOBA_BLOCK), lambda b, h: (b, h, 0, 0, 0)),
            pl.BlockSpec((None, n_blk, width), lambda b, h: (b, 0, h)),
        ],
        out_specs=pl.BlockSpec((seq, width), lambda b, h: (b, h), pipeline_mode=pl.Buffered(1)),
        out_shape=jax.ShapeDtypeStruct((m, d), BF16),
        scratch_shapes=[
            pltpu.VMEM((n_g, n_blk, 2 * HEAD_DIM, MOBA_BLOCK), BF16),
            pltpu.VMEM((n_g, HEAD_DIM + ATTN_SUM_ROWS, MOBA_BLOCK), F32),
            pltpu.VMEM((n_g, 2 * MOBA_BLOCK, MOBA_BLOCK), F32),
            pltpu.VMEM((n_g, 2 * MOBA_BLOCK, MOBA_BLOCK), F32),
        ],
        compiler_params=_params(2),
        name="moba_prompt",
    )(q, k_b, vt_b, kmean)


def _select_kernel(q_ref, bm_ref, sel_ref, *, n_valid):
    q = q_ref[...]
    t = q.shape[0]
    n_lane = bm_ref.shape[1]
    lane = lax.broadcasted_iota(jnp.int32, (t, n_lane), 1)
    lane_f = lane.astype(F32)
    out = jnp.zeros((t, n_lane), jnp.int32)
    for h in range(q.shape[1] // HEAD_DIM):
        cols = slice(h * HEAD_DIM, (h + 1) * HEAD_DIM)
        gate = lax.dot_general(q[:, cols], bm_ref[h], (((1,), (1,)), ((), ())),
                               preferred_element_type=F32, precision=lax.Precision.HIGHEST)
        for r, (_, idx) in enumerate(_top_k_picks(gate, lane < n_valid, lane_f, 1)):
            out = jnp.where(lane == h * MOBA_TOP_K + r, idx.astype(jnp.int32), out)
    sel_ref[...] = out


def _select_blocks(q, bm_pad, *, n_seq, n_valid):
    m, d = q.shape
    t = m // n_seq
    _, n_h, n_lane, dh = bm_pad.shape
    return pl.pallas_call(
        functools.partial(_select_kernel, n_valid=n_valid),
        grid=(n_seq,),
        in_specs=[
            pl.BlockSpec((t, d), lambda b: (b, 0)),
            pl.BlockSpec((None, n_h, n_lane, dh), lambda b: (b, 0, 0, 0)),
        ],
        out_specs=pl.BlockSpec((None, t, n_lane), lambda b: (b, 0, 0)),
        out_shape=jax.ShapeDtypeStruct((n_seq, t, n_lane), jnp.int32),
        compiler_params=_params(1),
        name="select_blocks",
    )(q, bm_pad)


def _sample_attn_kernel(sel_ref, pt_ref, q_ref, kn_ref, vn_ref, ck_hbm, cv_hbm, o_ref, kg_ref, vg_ref, sem,
                        *, n_pages):
    b = pl.program_id(0)
    h = pl.program_id(1)
    n_h = pl.num_programs(1)
    n_steps = pl.num_programs(0) * n_h
    step = b * n_h + h
    slot = step % 2
    t_len = q_ref.shape[0]
    n_sel = MOBA_TOP_K * MOBA_BLOCK
    scale = HEAD_DIM ** -0.5

    def gather_copies(bb, hh, sl):
        copies = []
        for t in range(t_len):
            for r in range(MOBA_TOP_K):
                blk = sel_ref[((bb * t_len + t) * n_h + hh) * MOBA_TOP_K + r]
                for pg in range(PAGES_PER_BLOCK):
                    phys = pt_ref[bb * n_pages + blk * PAGES_PER_BLOCK + pg]
                    rows = pl.ds((r * PAGES_PER_BLOCK + pg) * PAGE_SIZE, PAGE_SIZE)
                    copies.append(pltpu.make_async_copy(ck_hbm.at[phys, :, hh, :], kg_ref.at[sl, t, rows, :], sem.at[sl, 0]))
                    copies.append(pltpu.make_async_copy(cv_hbm.at[phys, :, hh, :], vg_ref.at[sl, t, rows, :], sem.at[sl, 1]))
        return copies

    @pl.when(step == 0)
    def _():
        for cp in gather_copies(b, h, slot):
            cp.start()

    @pl.when(step + 1 < n_steps)
    def _():
        nxt = step + 1
        for cp in gather_copies(nxt // n_h, nxt % n_h, 1 - slot):
            cp.start()

    for cp in gather_copies(b, h, slot):
        cp.wait()

    q = q_ref[...]
    k_own = kn_ref[...]
    v_own = vn_ref[...]
    own_i = lax.broadcasted_iota(jnp.int32, (t_len, 1), 0)
    rows = []
    for t in range(t_len):
        q_t = q[t:t + 1, :]
        s_sel = jnp.sum(kg_ref[slot, t] * q_t, axis=1, keepdims=True) * scale
        s_o = jnp.sum(k_own * q_t, axis=1, keepdims=True) * scale
        s_o = jnp.where(own_i <= t, s_o, NEG_INF)
        m = jnp.maximum(jnp.max(s_sel, axis=0, keepdims=True), jnp.max(s_o, axis=0, keepdims=True))
        p_sel = jnp.exp(s_sel - m)
        p_o = jnp.exp(s_o - m)
        l = jnp.sum(p_sel, axis=0, keepdims=True) + jnp.sum(p_o, axis=0, keepdims=True)
        out = (jnp.sum(p_sel * vg_ref[slot, t], axis=0, keepdims=True)
               + jnp.sum(p_o * v_own, axis=0, keepdims=True))
        rows.append(out / l)
    o_ref[...] = jnp.concatenate(rows, axis=0)


def _moba_sample(q, k_new, v_new, cache_k3, cache_v3, sel_flat, page_table_flat, *, n_seq, n_pages):
    m, d = q.shape
    t = m // n_seq
    n_h = d // HEAD_DIM
    n_sel = MOBA_TOP_K * MOBA_BLOCK
    row_spec = pl.BlockSpec((t, HEAD_DIM), lambda b, h, sel, pt: (b, h))
    grid_spec = pltpu.PrefetchScalarGridSpec(
        num_scalar_prefetch=2,
        grid=(n_seq, n_h),
        in_specs=[row_spec, row_spec, row_spec, pl.BlockSpec(memory_space=pl.ANY), pl.BlockSpec(memory_space=pl.ANY)],
        out_specs=row_spec,
        scratch_shapes=[
            pltpu.VMEM((2, t, n_sel, HEAD_DIM), F32),
            pltpu.VMEM((2, t, n_sel, HEAD_DIM), F32),
            pltpu.SemaphoreType.DMA((2, 2)),
        ],
    )
    return pl.pallas_call(
        functools.partial(_sample_attn_kernel, n_pages=n_pages),
        grid_spec=grid_spec,
        out_shape=jax.ShapeDtypeStruct((m, d), F32),
        compiler_params=_params(2),
        name="moba_sample",
    )(sel_flat, page_table_flat, q, k_new, v_new, cache_k3, cache_v3)


def _rope_tables(pos):
    half = HEAD_DIM // 2
    inv = 1.0 / (ROPE_THETA ** (jnp.arange(half, dtype=F32) * (2.0 / HEAD_DIM)))
    ang = pos.astype(F32)[:, None] * inv[None, :]
    cos = jnp.cos(ang)
    sin = jnp.sin(ang)
    return jnp.concatenate([cos, cos], axis=-1), jnp.concatenate([-sin, sin], axis=-1)


def kernel(x_prompt, x_sample, state_pool, cache_k, cache_v, page_table, g_pool, w_pool, s_pool, g_ffn, w_gate_up,
           w_down, g_kv, w_kv, g_attn, w_q, w_o, g_final):
    n_b, seq, d = x_prompt.shape
    n_db, t_dec, _ = x_sample.shape
    depth = g_ffn.shape[0]
    n_pool = g_pool.shape[0]
    n_h = d // HEAD_DIM
    n_pages = page_table.shape[1]
    n_past_blk = PAST_LEN // MOBA_BLOCK
    m_p = n_b * seq
    m_s = n_db * t_dec
    state_rows = state_pool.shape[2]
    assert n_pages == n_past_blk * PAGES_PER_BLOCK, "own MoBA block must hold only the new tokens"
    assert seq % MOBA_BLOCK == 0 and state_rows == POOL_HALO - 1

    tm_pool, tm_ffn, tf, tm_proj = 512, 512, 512, 512

    w_gu_b = w_gate_up.astype(BF16)
    w_dn_b = w_down.astype(BF16)

    cos_p, sin_p = _rope_tables(jnp.arange(seq))
    cos_s, sin_s = _rope_tables(PAST_LEN + jnp.arange(t_dec))
    cos_s = jnp.tile(cos_s, (n_db, 1))
    sin_s = jnp.tile(sin_s, (n_db, 1))

    pt_flat = page_table.reshape(-1)

    hp = x_prompt
    hs = x_sample
    pool_p, pool_s, bm_parts = [], [], []
    assert n_pool > 0 and (n_db * n_pages) % (n_pool * PAGES_PER_BLOCK) == 0
    for layer in range(depth):
        last = layer == depth - 1
        if layer < n_pool:
            hp3, zt_p = _pool_layer(hp.reshape(n_b, seq, d), hp.reshape(n_b, seq, d), g_pool[layer], w_pool[layer],
                                    s_pool[layer], tm=tm_pool, halo_normalized=False, pos0=0)
            state = state_pool[layer]
            halo_s = jnp.pad(state, ((0, 0), (POOL_HALO - state_rows, 0), (0, 0)))
            hs3, zt_s = _pool_layer(hs.reshape(n_db, t_dec, d), halo_s, g_pool[layer], w_pool[layer], s_pool[layer],
                                    tm=t_dec, halo_normalized=True, pos0=PAST_LEN)
            pool_p.append(zt_p[:, POOL_HALO - state_rows:])
            pool_s.append(jnp.concatenate([state, zt_s], axis=1)[:, -state_rows:])
            hp = hp3.reshape(m_p, d)
            hs = hs3.reshape(m_s, d)
        else:
            a = layer - n_pool
            if a == 0:
                k_p, k_pb, kmean, k_s = _k_proj(hp, hs, g_kv, w_kv, cos_p, sin_p, cos_s, sin_s, tm=tm_proj)
                v_p, vt_pb, v_s = _v_proj(hp, hs, g_kv, w_kv, tm=tm_proj, seq=seq)
                kmean = kmean.reshape(n_b, seq // MOBA_BLOCK, d)
                bm = jnp.concatenate(bm_parts, axis=0).reshape(n_db, n_past_blk, n_h, HEAD_DIM)
                bm_pad = jnp.pad(bm.transpose(0, 2, 1, 3), ((0, 0), (0, 0), (0, 128 - n_past_blk), (0, 0)))
            q_p, q_s = _q_proj(hp, hs, g_attn[a], w_q, a, cos_p, sin_p, cos_s, sin_s, tm=tm_proj)
            att_p = _moba_prompt(q_p, k_pb, vt_pb, kmean, batch=n_b, seq=seq)
            sel = _select_blocks(q_s, bm_pad, n_seq=n_db, n_valid=n_past_blk)
            sel_flat = sel[:, :, :n_h * MOBA_TOP_K].reshape(-1)
            att_s = _moba_sample(q_s, k_s, v_s, cache_k, cache_v, sel_flat, pt_flat, n_seq=n_db, n_pages=n_pages)
            hp, hs = _o_proj(att_p, att_s, w_o, a, hp, hs, tm=tm_proj)
        if layer < n_pool:
            pages_per_call = n_db * n_pages // n_pool
            hp, hs, bm_part = _ffn(hp, hs, g_ffn[layer], w_gu_b, w_dn_b, layer, g_final, tm=tm_ffn, tf=tf,
                                   final_norm=last, cache_means=(cache_k, pt_flat, layer * pages_per_call, pages_per_call))
            bm_parts.append(bm_part)
        else:
            hp, hs = _ffn(hp, hs, g_ffn[layer], w_gu_b, w_dn_b, layer, g_final, tm=tm_ffn, tf=tf, final_norm=last)

    y_prompt = hp.reshape(n_b, seq, d)
    y_sample = hs.reshape(n_db, t_dec, d)
    new_pool_prompt = jnp.stack(pool_p)
    new_pool_sample = jnp.stack(pool_s)
    shape_p = (n_b, seq, n_h, HEAD_DIM)
    shape_s = (n_db, t_dec, n_h, HEAD_DIM)
    return (y_prompt, y_sample, new_pool_prompt, new_pool_sample, k_p.reshape(shape_p), v_p.reshape(shape_p),
            k_s.reshape(shape_s), v_s.reshape(shape_s))
```

```python
import functools

import jax
import jax.numpy as jnp
from jax import lax
from jax.experimental import pallas as pl
from jax.experimental.pallas import tpu as pltpu

F32 = jnp.float32
BF16 = jnp.bfloat16

POOL_WINDOWS = (2, 4, 8, 16)
POOL_HALO = 16
HEAD_DIM = 128
MOBA_BLOCK = 256
MOBA_TOP_K = 3
PAGE_SIZE = 128
PAGES_PER_BLOCK = MOBA_BLOCK // PAGE_SIZE
PAST_LEN = 16384
ROPE_THETA = 10000.0
RMS_EPS = 1e-6
NEG_INF = -1e30
GATE_FLOOR = -3.0e38

VMEM_LIMIT_BYTES = 56 * 1024 * 1024


def _params(n_axes):
    return pltpu.CompilerParams(dimension_semantics=("arbitrary",) * n_axes, vmem_limit_bytes=VMEM_LIMIT_BYTES)


def _rmsnorm(x, g):
    ms = jnp.mean(x * x, axis=-1, keepdims=True)
    return x * lax.rsqrt(ms + RMS_EPS) * g


def _pool_kernel(x_ref, halo_ref, g_ref, w_ref, s_ref, h_ref, zt_ref, zs_ref, *, tm, halo_normalized, pos0):
    i = pl.program_id(1)
    d = x_ref.shape[-1]
    c = d // len(POOL_WINDOWS)
    x = x_ref[...]
    g = g_ref[...]
    z = _rmsnorm(x, g)
    if halo_normalized:
        zh = halo_ref[...]
    else:
        zh = _rmsnorm(halo_ref[...], g) * (i > 0).astype(F32)
    zs_ref[0:POOL_HALO, :] = zh
    zs_ref[POOL_HALO:POOL_HALO + tm, :] = z
    pos = lax.broadcasted_iota(jnp.int32, (tm, c), 0) + (i * tm + pos0)
    for gi, w in enumerate(POOL_WINDOWS):
        cols = slice(gi * c, (gi + 1) * c)
        zc = z[:, cols]
        tot = zc
        for j in range(1, w):
            tot = tot + zs_ref[POOL_HALO - j:POOL_HALO - j + tm, cols]
        cnt = jnp.minimum(pos + 1, w).astype(F32)
        pooled = tot / cnt - zc
        mixed = jnp.dot(pooled.astype(BF16), w_ref[gi].astype(BF16), preferred_element_type=F32)
        h_ref[:, cols] = x[:, cols] + mixed * s_ref[:, cols]

    nt = zt_ref.shape[0]

    @pl.when(i == pl.num_programs(1) - 1)
    def _():
        zt_ref[...] = z[tm - nt:, :]


def _pool_layer(x3, halo3, g, w_b, s, *, tm, halo_normalized, pos0):
    nb, length, d = x3.shape
    nt = min(POOL_HALO, tm)
    n_i = length // tm
    hb = tm // POOL_HALO
    if halo_normalized:
        halo_map = lambda b, i: (b, 0, 0)
    else:
        halo_map = lambda b, i: (b, jnp.maximum(i * hb - 1, 0), 0)
    kern = functools.partial(_pool_kernel, tm=tm, halo_normalized=halo_normalized, pos0=pos0)
    return pl.pallas_call(
        kern,
        grid=(nb, n_i),
        in_specs=[
            pl.BlockSpec((None, tm, d), lambda b, i: (b, i, 0)),
            pl.BlockSpec((None, POOL_HALO, d), halo_map),
            pl.BlockSpec((1, d), lambda b, i: (0, 0)),
            pl.BlockSpec(w_b.shape, lambda b, i: (0, 0, 0)),
            pl.BlockSpec((1, d), lambda b, i: (0, 0)),
        ],
        out_specs=[
            pl.BlockSpec((None, tm, d), lambda b, i: (b, i, 0)),
            pl.BlockSpec((None, nt, d), lambda b, i: (b, 0, 0)),
        ],
        out_shape=[jax.ShapeDtypeStruct((nb, length, d), F32), jax.ShapeDtypeStruct((nb, nt, d), F32)],
        scratch_shapes=[pltpu.VMEM((POOL_HALO + tm, d), F32)],
        compiler_params=_params(2),
        name="pool_layer",
    )(x3, halo3, g.reshape(1, d), w_b, s.reshape(1, d))


def _stream_block_means(pt_ref, cache_hbm, bm_ref, page_buf, page_sem, *, first_page, n_call_pages):
    step = pl.program_id(0) * pl.num_programs(1) + pl.program_id(1)
    pages_per_step = page_buf.shape[1]
    blocks_per_step = pages_per_step // PAGES_PER_BLOCK
    n_fetch = n_call_pages // pages_per_step

    def page_copy(at_step, p, slot):
        phys = pt_ref[first_page + at_step * pages_per_step + p]
        return pltpu.make_async_copy(cache_hbm.at[phys], page_buf.at[slot, p], page_sem.at[slot])

    def start_step(at_step, slot):
        for p in range(pages_per_step):
            page_copy(at_step, p, slot).start()

    @pl.when(step == 0)
    def _():
        start_step(0, 0)

    @pl.when(step + 1 < n_fetch)
    def _():
        start_step(step + 1, (step + 1) % 2)

    @pl.when(step < n_fetch)
    def _():
        for p in range(pages_per_step):
            page_copy(step, p, step % 2).wait()

    def add_pages():
        src = jnp.minimum(step, n_fetch - 1)
        slot = src % 2
        for b in range(blocks_per_step):
            rows = sum(jnp.sum(page_buf[slot, PAGES_PER_BLOCK * b + pg], axis=0) for pg in range(PAGES_PER_BLOCK))
            bm_ref[src * blocks_per_step + b] = rows * (1.0 / (PAGES_PER_BLOCK * PAGE_SIZE))

    return add_pages


def _ffn_kernel(*refs, final_norm, means, round_next):
    refs = list(refs)
    take = lambda n: [refs.pop(0) for _ in range(n)]
    pt_ref, = take(1) if means is not None else (None,)
    x_ref, xs_ref, g_ref, wg_ref, wu_ref, wd_ref, go_ref = take(7)
    cache_hbm, = take(1) if means is not None else (None,)
    next_in = take(2) if round_next else None
    o_ref, os_ref = take(2)
    bm_ref, = take(1) if means is not None else (None,)
    next_out = take(2) if round_next else None
    xn_ref, xns_ref = take(2)
    side_work = None
    if means is not None:
        page_buf, page_sem = take(2)
        side_work = _stream_block_means(pt_ref, cache_hbm, bm_ref, page_buf, page_sem, **means)
    j = pl.program_id(1)

    def rows(x_ref, o_ref, xn_ref, with_side_jobs):
        @pl.when(j == 0)
        def _():
            x = x_ref[...]
            xn_ref[...] = _rmsnorm(x, g_ref[...]).astype(BF16)
            o_ref[...] = x

        xn = xn_ref[...]
        gate = jnp.dot(xn, wg_ref[...], preferred_element_type=F32)
        up = jnp.dot(xn, wu_ref[...], preferred_element_type=F32)
        act = (gate * jax.nn.sigmoid(gate) * up).astype(BF16)
        o_ref[...] += jnp.dot(act, wd_ref[...], preferred_element_type=F32)
        if with_side_jobs and round_next:
            for src, dst in zip(next_in, next_out):
                dst[...] = src[...].astype(BF16)
        if with_side_jobs and side_work is not None:
            side_work()

        if final_norm:
            @pl.when(j == pl.num_programs(1) - 1)
            def _():
                o_ref[...] = _rmsnorm(o_ref[...], go_ref[...])

    rows(x_ref, o_ref, xn_ref, True)

    @pl.when(pl.program_id(0) == pl.num_programs(0) - 1)
    def _():
        rows(xs_ref, os_ref, xns_ref, False)


def _ffn(x, xs, g, w_gu_b, w_dn_b, g_out, *, tm, tf, final_norm, cache_means=None, next_weights=None):
    m, d = x.shape
    m_s = xs.shape[0]
    f = w_dn_b.shape[0]
    n_f = f // tf
    grid = (m // tm, n_f)
    in_specs = [
        pl.BlockSpec((tm, d), lambda i, j, *_: (i, 0)),
        pl.BlockSpec((m_s, d), lambda i, j, *_: (0, 0)),
        pl.BlockSpec((1, d), lambda i, j, *_: (0, 0)),
        pl.BlockSpec((d, tf), lambda i, j, *_: (0, j)),
        pl.BlockSpec((d, tf), lambda i, j, *_: (0, j + n_f)),
        pl.BlockSpec((tf, d), lambda i, j, *_: (j, 0)),
        pl.BlockSpec((1, d), lambda i, j, *_: (0, 0)),
    ]
    out_specs = [pl.BlockSpec((tm, d), lambda i, j, *_: (i, 0)), pl.BlockSpec((m_s, d), lambda i, j, *_: (0, 0))]
    out_shape = [jax.ShapeDtypeStruct((m, d), F32), jax.ShapeDtypeStruct((m_s, d), F32)]
    scratch = [pltpu.VMEM((tm, d), BF16), pltpu.VMEM((m_s, d), BF16)]
    operands = [x, xs, g.reshape(1, d), w_gu_b, w_gu_b, w_dn_b, g_out.reshape(1, d)]
    next_specs, next_shapes, next_operands = [], [], []
    if next_weights is not None:
        w_gu_all, w_dn_all, nxt = next_weights
        gu_chunk = (d // grid[0], 2 * f // grid[1])
        dn_rows = f // (grid[0] * grid[1])
        assert gu_chunk[0] * grid[0] == d and gu_chunk[1] * grid[1] == 2 * f and dn_rows * grid[0] * grid[1] == f
        next_operands = [w_gu_all, w_dn_all]
        next_specs = [pl.BlockSpec((None,) + gu_chunk, lambda i, j, *_: (nxt, i, j)),
                      pl.BlockSpec((None, dn_rows, d), lambda i, j, *_: (nxt, i * n_f + j, 0))]
        next_out_specs = [pl.BlockSpec(gu_chunk, lambda i, j, *_: (i, j)),
                          pl.BlockSpec((dn_rows, d), lambda i, j, *_: (i * n_f + j, 0))]
        next_shapes = [jax.ShapeDtypeStruct((d, 2 * f), BF16), jax.ShapeDtypeStruct((f, d), BF16)]
    if cache_means is None:
        means = None
        prefetch = []
    else:
        cache_k, page_table_flat, first_page, n_call_pages = cache_means
        _, page, n_h, dh = cache_k.shape
        blocks_per_step = -(-n_call_pages // (PAGES_PER_BLOCK * grid[0] * grid[1]))
        while n_call_pages % (PAGES_PER_BLOCK * blocks_per_step):
            blocks_per_step += 1
        pages_per_step = PAGES_PER_BLOCK * blocks_per_step
        means = dict(first_page=first_page, n_call_pages=n_call_pages)
        prefetch = [page_table_flat]
        in_specs.append(pl.BlockSpec(memory_space=pl.ANY))
        operands.append(cache_k)
        n_call_blk = n_call_pages // PAGES_PER_BLOCK
        out_specs.append(pl.BlockSpec((n_call_blk, n_h, dh), lambda i, j, *_: (0, 0, 0)))
        out_shape.append(jax.ShapeDtypeStruct((n_call_blk, n_h, dh), F32))
        scratch += [pltpu.VMEM((2, pages_per_step, page, n_h, dh), F32), pltpu.SemaphoreType.DMA((2,))]
    if next_weights is not None:
        in_specs += next_specs
        operands += next_operands
        out_specs += next_out_specs
        out_shape += next_shapes
    grid_spec = pltpu.PrefetchScalarGridSpec(
        num_scalar_prefetch=len(prefetch), grid=grid, in_specs=in_specs, out_specs=out_specs, scratch_shapes=scratch)
    outs = list(pl.pallas_call(
        functools.partial(_ffn_kernel, final_norm=final_norm, means=means, round_next=next_weights is not None),
        grid_spec=grid_spec,
        out_shape=out_shape,
        compiler_params=_params(2),
        name="swiglu",
    )(*prefetch, *operands))
    out, out_small = outs[:2]
    del outs[:2]
    block_means = outs.pop(0) if cache_means is not None else None
    next_b = tuple(outs) if next_weights is not None else None
    return out, out_small, block_means, next_b


def _rope(xh, cos, sin_signed):
    return xh * cos + pltpu.roll(xh, HEAD_DIM // 2, axis=1) * sin_signed


def _round_weight_once(w_ref, wb_ref):
    @pl.when(pl.program_id(0) == 0)
    def _():
        wb_ref[...] = w_ref[...].astype(BF16)


def _prompt_then_sample(n_tiles, prompt_fn, sample_fn):
    i = pl.program_id(0)
    pl.when(i < n_tiles)(prompt_fn)
    pl.when(i == n_tiles)(sample_fn)


def _proj_specs(tm, d, m_s, n_tiles):
    del d
    tile = lambda i: jnp.minimum(i, n_tiles - 1)
    prompt = lambda width: pl.BlockSpec((tm, width), lambda i: (tile(i), 0))
    sample = lambda width: pl.BlockSpec((m_s, width), lambda i: (0, 0))
    return tile, prompt, sample


def _q_kernel(xp_ref, xs_ref, g_ref, w_ref, cosp_ref, sinp_ref, coss_ref, sins_ref, qp_ref, qs_ref, wb_ref, *, n_tiles):
    _round_weight_once(w_ref, wb_ref)

    def rows(x_ref, cos_ref, sin_ref, q_ref):
        xn = _rmsnorm(x_ref[...], g_ref[...]).astype(BF16)
        acc = jnp.dot(xn, wb_ref[...], preferred_element_type=F32)
        cos = cos_ref[...]
        sin = sin_ref[...]
        for h in range(acc.shape[1] // HEAD_DIM):
            cols = slice(h * HEAD_DIM, (h + 1) * HEAD_DIM)
            q_ref[:, cols] = _rope(acc[:, cols], cos, sin)

    _prompt_then_sample(n_tiles, lambda: rows(xp_ref, cosp_ref, sinp_ref, qp_ref),
                        lambda: rows(xs_ref, coss_ref, sins_ref, qs_ref))


def _q_proj(xp, xs, g, w, layer, cos_p, sin_p, cos_s, sin_s, *, tm):
    m, d = xp.shape
    m_s = xs.shape[0]
    n_tiles = m // tm
    n_pos = cos_p.shape[0] // tm
    tile, prompt, sample = _proj_specs(tm, d, m_s, n_tiles)
    rope_p = pl.BlockSpec((tm, HEAD_DIM), lambda i: (tile(i) % n_pos, 0))
    return pl.pallas_call(
        functools.partial(_q_kernel, n_tiles=n_tiles),
        grid=(n_tiles + 1,),
        in_specs=[
            prompt(d), sample(d),
            pl.BlockSpec((1, d), lambda i: (0, 0)),
            pl.BlockSpec((None, d, d), lambda i: (layer, 0, 0), pipeline_mode=pl.Buffered(1)),
            rope_p, rope_p, sample(HEAD_DIM), sample(HEAD_DIM),
        ],
        out_specs=[prompt(d), sample(d)],
        out_shape=[jax.ShapeDtypeStruct((m, d), F32), jax.ShapeDtypeStruct((m_s, d), F32)],
        scratch_shapes=[pltpu.VMEM((d, d), BF16)],
        compiler_params=_params(1),
        name="q_proj",
    )(xp, xs, g.reshape(1, d), w, cos_p, sin_p, cos_s, sin_s)


def _k_kernel(xp_ref, xs_ref, g_ref, w_ref, cosp_ref, sinp_ref, coss_ref, sins_ref, kp_ref, kb_ref, km_ref, ks_ref,
              wb_ref, *, n_tiles):
    _round_weight_once(w_ref, wb_ref)

    def rows(x_ref, cos_ref, sin_ref, k_ref):
        xn = _rmsnorm(x_ref[...], g_ref[...]).astype(BF16)
        acc = jnp.dot(xn, wb_ref[...], preferred_element_type=F32)
        cos = cos_ref[...]
        sin = sin_ref[...]
        for h in range(acc.shape[1] // HEAD_DIM):
            cols = slice(h * HEAD_DIM, (h + 1) * HEAD_DIM)
            k_ref[:, cols] = _rope(acc[:, cols], cos, sin)

    def prompt_rows():
        rows(xp_ref, cosp_ref, sinp_ref, kp_ref)
        k = kp_ref[...]
        kb_ref[...] = k.astype(BF16)
        for r in range(k.shape[0] // MOBA_BLOCK):
            km_ref[r] = jnp.mean(k[r * MOBA_BLOCK:(r + 1) * MOBA_BLOCK, :], axis=0, keepdims=True)

    _prompt_then_sample(n_tiles, prompt_rows, lambda: rows(xs_ref, coss_ref, sins_ref, ks_ref))


def _k_proj(xp, xs, g, w_kv, cos_p, sin_p, cos_s, sin_s, *, tm):
    m, d = xp.shape
    m_s = xs.shape[0]
    n_tiles = m // tm
    n_pos = cos_p.shape[0] // tm
    r = tm // MOBA_BLOCK
    tile, prompt, sample = _proj_specs(tm, d, m_s, n_tiles)
    rope_p = pl.BlockSpec((tm, HEAD_DIM), lambda i: (tile(i) % n_pos, 0))
    return pl.pallas_call(
        functools.partial(_k_kernel, n_tiles=n_tiles),
        grid=(n_tiles + 1,),
        in_specs=[
            prompt(d), sample(d),
            pl.BlockSpec((1, d), lambda i: (0, 0)),
            pl.BlockSpec((d, d), lambda i: (0, 0), pipeline_mode=pl.Buffered(1)),
            rope_p, rope_p, sample(HEAD_DIM), sample(HEAD_DIM),
        ],
        out_specs=[prompt(d), prompt(d), pl.BlockSpec((r, 1, d), lambda i: (tile(i), 0, 0)), sample(d)],
        out_shape=[jax.ShapeDtypeStruct((m, d), F32), jax.ShapeDtypeStruct((m, d), BF16),
                   jax.ShapeDtypeStruct((m // MOBA_BLOCK, 1, d), F32), jax.ShapeDtypeStruct((m_s, d), F32)],
        scratch_shapes=[pltpu.VMEM((d, d), BF16)],
        compiler_params=_params(1),
        name="k_proj",
    )(xp, xs, g.reshape(1, d), w_kv, cos_p, sin_p, cos_s, sin_s)


def _v_kernel(xp_ref, xs_ref, g_ref, w_ref, vp_ref, vt_ref, vs_ref, wb_ref, *, n_tiles):
    _round_weight_once(w_ref, wb_ref)

    def values(x_ref):
        xn = _rmsnorm(x_ref[...], g_ref[...]).astype(BF16)
        return jnp.dot(xn, wb_ref[...], preferred_element_type=F32)

    def prompt_rows():
        v = values(xp_ref)
        vp_ref[...] = v
        for h in range(v.shape[1] // HEAD_DIM):
            for r in range(v.shape[0] // MOBA_BLOCK):
                blk = v[r * MOBA_BLOCK:(r + 1) * MOBA_BLOCK, h * HEAD_DIM:(h + 1) * HEAD_DIM]
                vt_ref[h, r] = blk.T.astype(BF16)

    def sample_rows():
        vs_ref[...] = values(xs_ref)

    _prompt_then_sample(n_tiles, prompt_rows, sample_rows)


def _v_proj(xp, xs, g, w_kv, *, tm, seq):
    m, d = xp.shape
    m_s = xs.shape[0]
    n_h = d // HEAD_DIM
    n_tiles = m // tm
    r = tm // MOBA_BLOCK
    tiles_per_seq = seq // tm
    tile, prompt, sample = _proj_specs(tm, d, m_s, n_tiles)
    vt_spec = pl.BlockSpec((None, n_h, r, HEAD_DIM, MOBA_BLOCK),
                           lambda i: (tile(i) // tiles_per_seq, 0, tile(i) % tiles_per_seq, 0, 0))
    return pl.pallas_call(
        functools.partial(_v_kernel, n_tiles=n_tiles),
        grid=(n_tiles + 1,),
        in_specs=[
            prompt(d), sample(d),
            pl.BlockSpec((1, d), lambda i: (0, 0)),
            pl.BlockSpec((d, d), lambda i: (0, 1), pipeline_mode=pl.Buffered(1)),
        ],
        out_specs=[prompt(d), vt_spec, sample(d)],
        out_shape=[jax.ShapeDtypeStruct((m, d), F32),
                   jax.ShapeDtypeStruct((m // seq, n_h, seq // MOBA_BLOCK, HEAD_DIM, MOBA_BLOCK), BF16),
                   jax.ShapeDtypeStruct((m_s, d), F32)],
        scratch_shapes=[pltpu.VMEM((d, d), BF16)],
        compiler_params=_params(1),
        name="v_proj",
    )(xp, xs, g.reshape(1, d), w_kv)


def _o_kernel(ap_ref, as_ref, w_ref, hp_ref, hs_ref, op_ref, os_ref, wb_ref, *, n_tiles):
    _round_weight_once(w_ref, wb_ref)

    def rows(a_ref, h_ref, o_ref):
        o_ref[...] = h_ref[...] + jnp.dot(a_ref[...].astype(BF16), wb_ref[...], preferred_element_type=F32)

    _prompt_then_sample(n_tiles, lambda: rows(ap_ref, hp_ref, op_ref), lambda: rows(as_ref, hs_ref, os_ref))


def _o_proj(ap, a_s, w, layer, hp, hs, *, tm):
    m, d = hp.shape
    m_s = hs.shape[0]
    n_tiles = m // tm
    _, prompt, sample = _proj_specs(tm, d, m_s, n_tiles)
    return pl.pallas_call(
        functools.partial(_o_kernel, n_tiles=n_tiles),
        grid=(n_tiles + 1,),
        in_specs=[
            prompt(d), sample(d),
            pl.BlockSpec((None, d, d), lambda i: (layer, 0, 0), pipeline_mode=pl.Buffered(1)),
            prompt(d), sample(d),
        ],
        out_specs=[prompt(d), sample(d)],
        out_shape=[jax.ShapeDtypeStruct((m, d), F32), jax.ShapeDtypeStruct((m_s, d), F32)],
        scratch_shapes=[pltpu.VMEM((d, d), BF16)],
        compiler_params=_params(1),
        name="o_proj",
    )(ap, a_s, w, hp, hs)


def _top_k_picks(gate, candidate, blk_f, axis):
    remaining = candidate
    for _ in range(MOBA_TOP_K):
        gm = jnp.where(remaining, gate, GATE_FLOOR)
        top = jnp.max(gm, axis=axis, keepdims=True)
        hit = jnp.logical_and(remaining, gm == top)
        idx = jnp.min(jnp.where(hit, blk_f, 1e9), axis=axis, keepdims=True)
        pick = blk_f == idx
        remaining = jnp.logical_and(remaining, jnp.logical_not(pick))
        yield pick, idx


def _attn_kernel(q_ref, k_ref, vt_ref, km_ref, o_ref, qt_ref, acc_ref, sa_ref, sb_ref):
    n_g, n_blk = qt_ref.shape[0], qt_ref.shape[1]
    tq = MOBA_BLOCK
    q_scale = HEAD_DIM ** -0.5 * 1.4426950408889634
    blk_i = lax.broadcasted_iota(jnp.int32, (n_blk, tq), 0)
    blk_f = blk_i.astype(F32)
    heads = [slice(g * HEAD_DIM, (g + 1) * HEAD_DIM) for g in range(n_g)]
    assert n_blk <= qt_ref.shape[2] - HEAD_DIM

    pad = jnp.zeros((qt_ref.shape[2] - HEAD_DIM - n_blk, tq), BF16)
    for g in range(n_g):
        km = km_ref[:, heads[g]]
        for i in range(n_blk):
            q_t = q_ref[i * tq:(i + 1) * tq, heads[g]].T
            qt_ref[g, i, 0:HEAD_DIM, :] = (q_t * q_scale).astype(BF16)
            if i <= MOBA_TOP_K:
                selected = blk_i < i
            else:
                gate_t = jnp.dot(km, q_t, preferred_element_type=F32, precision=lax.Precision.HIGHEST)
                selected = jnp.zeros((n_blk, tq), jnp.bool_)
                for pick, _ in _top_k_picks(gate_t, blk_i < i, blk_f, 0):
                    selected = jnp.logical_or(selected, pick)
            qt_ref[g, i, HEAD_DIM:HEAD_DIM + n_blk, :] = jnp.where(selected, 0.0, NEG_INF).astype(BF16)
            qt_ref[g, i, HEAD_DIM + n_blk:, :] = pad

    kpos = lax.broadcasted_iota(jnp.int32, (tq, tq), 0)
    qpos = lax.broadcasted_iota(jnp.int32, (tq, tq), 1)
    causal = kpos <= qpos

    last_pair = n_blk // 2 - 1
    pair_row = lax.broadcasted_iota(jnp.int32, (2 * tq, HEAD_DIM), 0)
    pair_lane = lax.broadcasted_iota(jnp.int32, (2 * tq, HEAD_DIM), 1)
    lane_minus_half = pair_lane - jnp.where(pair_row >= tq, 1, 0)
    sum_rows = acc_ref.shape[1] - HEAD_DIM
    ones_rows = jnp.where(lax.broadcasted_iota(jnp.int32, (sum_rows, tq), 0) == 0, 1.0, 0.0).astype(BF16)

    def values_aug(g, j):
        return jnp.concatenate([vt_ref[g, j], ones_rows], axis=0)

    def tile(i, _):
        start = pl.multiple_of(i * tq, tq)

        def score_pair(pair_idx, dst_ref):
            pair_c = jnp.minimum(pair_idx, last_pair)
            st = pl.multiple_of(pair_c * (2 * tq), 2 * tq)
            onehot = jnp.where(lane_minus_half == 2 * pair_c, 1.0, 0.0).astype(BF16)
            for g in range(n_g):
                keys_aug = jnp.concatenate([k_ref[pl.ds(st, 2 * tq), heads[g]], onehot], axis=1)
                dst_ref[g] = jnp.dot(keys_aug, qt_ref[g, i], preferred_element_type=F32)

        def fold_pair(src_ref, pair_idx, ms):
            j0 = 2 * jnp.minimum(pair_idx, last_pair)
            new_ms, alphas, ps = [], [], []
            for g in range(n_g):
                sa = src_ref[g, 0:tq, :]
                sb = src_ref[g, tq:2 * tq, :]
                m_blk = jnp.maximum(jnp.max(sa, axis=0, keepdims=True), jnp.max(sb, axis=0, keepdims=True))
                m_new = jnp.maximum(ms[g], m_blk)
                new_ms.append(m_new)
                alphas.append(jnp.exp2(ms[g] - m_new))
                ps.append((jnp.exp2(sa - m_new).astype(BF16), jnp.exp2(sb - m_new).astype(BF16)))
            for g in range(n_g):
                pv = (jnp.dot(values_aug(g, j0), ps[g][0], preferred_element_type=F32)
                      + jnp.dot(values_aug(g, j0 + 1), ps[g][1], preferred_element_type=F32))
                acc_ref[g] = alphas[g] * acc_ref[g] + pv
            return tuple(new_ms)

        ms = []
        s_own = [jnp.dot(k_ref[pl.ds(start, tq), heads[g]], qt_ref[g, i, 0:HEAD_DIM, :], preferred_element_type=F32)
                 for g in range(n_g)]
        score_pair(0, sa_ref)
        p_own = []
        for g in range(n_g):
            s = jnp.where(causal, s_own[g], NEG_INF)
            m0 = jnp.max(s, axis=0, keepdims=True)
            ms.append(m0)
            p_own.append(jnp.exp2(s - m0).astype(BF16))
        for g in range(n_g):
            acc_ref[g] = jnp.dot(values_aug(g, i), p_own[g], preferred_element_type=F32)

        def two_pairs(t, ms):
            score_pair(2 * t + 1, sb_ref)
            ms = fold_pair(sa_ref, 2 * t, ms)
            score_pair(2 * t + 2, sa_ref)
            return fold_pair(sb_ref, 2 * t + 1, ms)

        n_pairs = (i + 1) // 2
        ms = lax.fori_loop(0, n_pairs // 2, two_pairs, tuple(ms))

        @pl.when(n_pairs % 2 == 1)
        def _():
            fold_pair(sa_ref, n_pairs - 1, ms)

        for g in range(n_g):
            inv_l = 1.0 / acc_ref[g, HEAD_DIM:HEAD_DIM + 1, :]
            o_ref[pl.ds(start, tq), heads[g]] = (acc_ref[g, 0:HEAD_DIM, :] * inv_l).T.astype(o_ref.dtype)
        return 0

    lax.fori_loop(0, n_blk, tile, 0)


ATTN_HEADS_PER_STEP = 4


ATTN_SUM_ROWS = 16


def _moba_prompt(q, k_b, vt_b, kmean, *, batch, seq):
    m, d = q.shape
    n_g = ATTN_HEADS_PER_STEP
    n_blk = seq // MOBA_BLOCK
    width = n_g * HEAD_DIM
    return pl.pallas_call(
        _attn_kernel,
        grid=(batch, d // width),
        in_specs=[
            pl.BlockSpec((seq, width), lambda b, h: (b, h)),
            pl.BlockSpec((seq, width), lambda b, h: (b, h)),
            pl.BlockSpec((None, n_g, n_blk, HEAD_DIM, MOBA_BLOCK), lambda b, h: (b, h, 0, 0, 0)),
            pl.BlockSpec((None, n_blk, width), lambda b, h: (b, 0, h)),
        ],
        out_specs=pl.BlockSpec((seq, width), lambda b, h: (b, h), pipeline_mode=pl.Buffered(1)),
        out_shape=jax.ShapeDtypeStruct((m, d), BF16),
        scratch_shapes=[
            pltpu.VMEM((n_g, n_blk, 2 * HEAD_DIM, MOBA_BLOCK), BF16),
            pltpu.VMEM((n_g, HEAD_DIM + ATTN_SUM_ROWS, MOBA_BLOCK), F32),
            pltpu.VMEM((n_g, 2 * MOBA_BLOCK, MOBA_BLOCK), F32),
            pltpu.VMEM((n_g, 2 * MOBA_BLOCK, MOBA_BLOCK), F32),
        ],
        compiler_params=_params(2),
        name="moba_prompt",
    )(q, k_b, vt_b, kmean)


def _select_kernel(q_ref, bm_ref, sel_ref, *, n_valid):
    q = q_ref[...]
    t = q.shape[0]
    n_lane = bm_ref.shape[1]
    lane = lax.broadcasted_iota(jnp.int32, (t, n_lane), 1)
    lane_f = lane.astype(F32)
    out = jnp.zeros((t, n_lane), jnp.int32)
    for h in range(q.shape[1] // HEAD_DIM):
        cols = slice(h * HEAD_DIM, (h + 1) * HEAD_DIM)
        gate = lax.dot_general(q[:, cols], bm_ref[h], (((1,), (1,)), ((), ())),
                               preferred_element_type=F32, precision=lax.Precision.HIGHEST)
        for r, (_, idx) in enumerate(_top_k_picks(gate, lane < n_valid, lane_f, 1)):
            out = jnp.where(lane == h * MOBA_TOP_K + r, idx.astype(jnp.int32), out)
    sel_ref[...] = out


def _select_blocks(q, bm_pad, *, n_seq, n_valid):
    m, d = q.shape
    t = m // n_seq
    _, n_h, n_lane, dh = bm_pad.shape
    return pl.pallas_call(
        functools.partial(_select_kernel, n_valid=n_valid),
        grid=(n_seq,),
        in_specs=[
            pl.BlockSpec((t, d), lambda b: (b, 0)),
            pl.BlockSpec((None, n_h, n_lane, dh), lambda b: (b, 0, 0, 0)),
        ],
        out_specs=pl.BlockSpec((None, t, n_lane), lambda b: (b, 0, 0)),
        out_shape=jax.ShapeDtypeStruct((n_seq, t, n_lane), jnp.int32),
        compiler_params=_params(1),
        name="select_blocks",
    )(q, bm_pad)


def _sample_attn_kernel(sel_ref, pt_ref, q_ref, kn_ref, vn_ref, ck_hbm, cv_hbm, o_ref, kg_ref, vg_ref, sem,
                        *, n_pages):
    b = pl.program_id(0)
    h = pl.program_id(1)
    n_h = pl.num_programs(1)
    n_steps = pl.num_programs(0) * n_h
    step = b * n_h + h
    slot = step % 2
    t_len = q_ref.shape[0]
    n_sel = MOBA_TOP_K * MOBA_BLOCK
    scale = HEAD_DIM ** -0.5

    def gather_copies(bb, hh, sl):
        copies = []
        for t in range(t_len):
            for r in range(MOBA_TOP_K):
                blk = sel_ref[((bb * t_len + t) * n_h + hh) * MOBA_TOP_K + r]
                for pg in range(PAGES_PER_BLOCK):
                    phys = pt_ref[bb * n_pages + blk * PAGES_PER_BLOCK + pg]
                    rows = pl.ds((r * PAGES_PER_BLOCK + pg) * PAGE_SIZE, PAGE_SIZE)
                    copies.append(pltpu.make_async_copy(ck_hbm.at[phys, :, hh, :], kg_ref.at[sl, t, rows, :], sem.at[sl, 0]))
                    copies.append(pltpu.make_async_copy(cv_hbm.at[phys, :, hh, :], vg_ref.at[sl, t, rows, :], sem.at[sl, 1]))
        return copies

    @pl.when(step == 0)
    def _():
        for cp in gather_copies(b, h, slot):
            cp.start()

    @pl.when(step + 1 < n_steps)
    def _():
        nxt = step + 1
        for cp in gather_copies(nxt // n_h, nxt % n_h, 1 - slot):
            cp.start()

    for cp in gather_copies(b, h, slot):
        cp.wait()

    q = q_ref[...]
    k_own = kn_ref[...]
    v_own = vn_ref[...]
    own_i = lax.broadcasted_iota(jnp.int32, (t_len, 1), 0)
    rows = []
    for t in range(t_len):
        q_t = q[t:t + 1, :]
        s_sel = jnp.sum(kg_ref[slot, t] * q_t, axis=1, keepdims=True) * scale
        s_o = jnp.sum(k_own * q_t, axis=1, keepdims=True) * scale
        s_o = jnp.where(own_i <= t, s_o, NEG_INF)
        m = jnp.maximum(jnp.max(s_sel, axis=0, keepdims=True), jnp.max(s_o, axis=0, keepdims=True))
        p_sel = jnp.exp(s_sel - m)
        p_o = jnp.exp(s_o - m)
        l = jnp.sum(p_sel, axis=0, keepdims=True) + jnp.sum(p_o, axis=0, keepdims=True)
        out = (jnp.sum(p_sel * vg_ref[slot, t], axis=0, keepdims=True)
               + jnp.sum(p_o * v_own, axis=0, keepdims=True))
        rows.append(out / l)
    o_ref[...] = jnp.concatenate(rows, axis=0)


def _moba_sample(q, k_new, v_new, cache_k3, cache_v3, sel_flat, page_table_flat, *, n_seq, n_pages):
    m, d = q.shape
    t = m // n_seq
    n_h = d // HEAD_DIM
    n_sel = MOBA_TOP_K * MOBA_BLOCK
    row_spec = pl.BlockSpec((t, HEAD_DIM), lambda b, h, sel, pt: (b, h))
    grid_spec = pltpu.PrefetchScalarGridSpec(
        num_scalar_prefetch=2,
        grid=(n_seq, n_h),
        in_specs=[row_spec, row_spec, row_spec, pl.BlockSpec(memory_space=pl.ANY), pl.BlockSpec(memory_space=pl.ANY)],
        out_specs=row_spec,
        scratch_shapes=[
            pltpu.VMEM((2, t, n_sel, HEAD_DIM), F32),
            pltpu.VMEM((2, t, n_sel, HEAD_DIM), F32),
            pltpu.SemaphoreType.DMA((2, 2)),
        ],
    )
    return pl.pallas_call(
        functools.partial(_sample_attn_kernel, n_pages=n_pages),
        grid_spec=grid_spec,
        out_shape=jax.ShapeDtypeStruct((m, d), F32),
        compiler_params=_params(2),
        name="moba_sample",
    )(sel_flat, page_table_flat, q, k_new, v_new, cache_k3, cache_v3)


def _rope_tables(pos):
    half = HEAD_DIM // 2
    inv = 1.0 / (ROPE_THETA ** (jnp.arange(half, dtype=F32) * (2.0 / HEAD_DIM)))
    ang = pos.astype(F32)[:, None] * inv[None, :]
    cos = jnp.cos(ang)
    sin = jnp.sin(ang)
    return jnp.concatenate([cos, cos], axis=-1), jnp.concatenate([-sin, sin], axis=-1)


def kernel(x_prompt, x_sample, state_pool, cache_k, cache_v, page_table, g_pool, w_pool, s_pool, g_ffn, w_gate_up,
           w_down, g_kv, w_kv, g_attn, w_q, w_o, g_final):
    n_b, seq, d = x_prompt.shape
    n_db, t_dec, _ = x_sample.shape
    depth = g_ffn.shape[0]
    n_pool = g_pool.shape[0]
    n_h = d // HEAD_DIM
    n_pages = page_table.shape[1]
    n_past_blk = PAST_LEN // MOBA_BLOCK
    m_p = n_b * seq
    m_s = n_db * t_dec
    state_rows = state_pool.shape[2]
    assert n_pages == n_past_blk * PAGES_PER_BLOCK, "own MoBA block must hold only the new tokens"
    assert seq % MOBA_BLOCK == 0 and state_rows == POOL_HALO - 1

    tm_pool, tm_ffn, tf, tm_proj = 512, 512, 512, 512

    w_gu_b = w_gate_up[0].astype(BF16)
    w_dn_b = w_down[0].astype(BF16)

    cos_p, sin_p = _rope_tables(jnp.arange(seq))
    cos_s, sin_s = _rope_tables(PAST_LEN + jnp.arange(t_dec))
    cos_s = jnp.tile(cos_s, (n_db, 1))
    sin_s = jnp.tile(sin_s, (n_db, 1))

    pt_flat = page_table.reshape(-1)

    hp = x_prompt
    hs = x_sample
    pool_p, pool_s, bm_parts = [], [], []
    assert n_pool > 0 and (n_db * n_pages) % (n_pool * PAGES_PER_BLOCK) == 0
    for layer in range(depth):
        last = layer == depth - 1
        if layer < n_pool:
            hp3, zt_p = _pool_layer(hp.reshape(n_b, seq, d), hp.reshape(n_b, seq, d), g_pool[layer], w_pool[layer],
                                    s_pool[layer], tm=tm_pool, halo_normalized=False, pos0=0)
            state = state_pool[layer]
            halo_s = jnp.pad(state, ((0, 0), (POOL_HALO - state_rows, 0), (0, 0)))
            hs3, zt_s = _pool_layer(hs.reshape(n_db, t_dec, d), halo_s, g_pool[layer], w_pool[layer], s_pool[layer],
                                    tm=t_dec, halo_normalized=True, pos0=PAST_LEN)
            pool_p.append(zt_p[:, POOL_HALO - state_rows:])
            pool_s.append(jnp.concatenate([state, zt_s], axis=1)[:, -state_rows:])
            hp = hp3.reshape(m_p, d)
            hs = hs3.reshape(m_s, d)
        else:
            a = layer - n_pool
            if a == 0:
                k_p, k_pb, kmean, k_s = _k_proj(hp, hs, g_kv, w_kv, cos_p, sin_p, cos_s, sin_s, tm=tm_proj)
                v_p, vt_pb, v_s = _v_proj(hp, hs, g_kv, w_kv, tm=tm_proj, seq=seq)
                kmean = kmean.reshape(n_b, seq // MOBA_BLOCK, d)
                bm = jnp.concatenate(bm_parts, axis=0).reshape(n_db, n_past_blk, n_h, HEAD_DIM)
                bm_pad = jnp.pad(bm.transpose(0, 2, 1, 3), ((0, 0), (0, 0), (0, 128 - n_past_blk), (0, 0)))
            q_p, q_s = _q_proj(hp, hs, g_attn[a], w_q, a, cos_p, sin_p, cos_s, sin_s, tm=tm_proj)
            att_p = _moba_prompt(q_p, k_pb, vt_pb, kmean, batch=n_b, seq=seq)
            sel = _select_blocks(q_s, bm_pad, n_seq=n_db, n_valid=n_past_blk)
            sel_flat = sel[:, :, :n_h * MOBA_TOP_K].reshape(-1)
            att_s = _moba_sample(q_s, k_s, v_s, cache_k, cache_v, sel_flat, pt_flat, n_seq=n_db, n_pages=n_pages)
            hp, hs = _o_proj(att_p, att_s, w_o, a, hp, hs, tm=tm_proj)
        pages_per_call = n_db * n_pages // n_pool
        cache_means = (cache_k, pt_flat, layer * pages_per_call, pages_per_call) if layer < n_pool else None
        next_weights = None if last else (w_gate_up, w_down, layer + 1)
        hp, hs, bm_part, next_b = _ffn(hp, hs, g_ffn[layer], w_gu_b, w_dn_b, g_final, tm=tm_ffn, tf=tf, final_norm=last,
                                       cache_means=cache_means, next_weights=next_weights)
        if bm_part is not None:
            bm_parts.append(bm_part)
        if next_b is not None:
            w_gu_b, w_dn_b = next_b

    y_prompt = hp.reshape(n_b, seq, d)
    y_sample = hs.reshape(n_db, t_dec, d)
    new_pool_prompt = jnp.stack(pool_p)
    new_pool_sample = jnp.stack(pool_s)
    shape_p = (n_b, seq, n_h, HEAD_DIM)
    shape_s = (n_db, t_dec, n_h, HEAD_DIM)
    return (y_prompt, y_sample, new_pool_prompt, new_pool_sample, k_p.reshape(shape_p), v_p.reshape(shape_p),
            k_s.reshape(shape_s), v_s.reshape(shape_s))
```

```python
import functools

import jax
import jax.numpy as jnp
from jax import lax
from jax.experimental import pallas as pl
from jax.experimental.pallas import tpu as pltpu

F32 = jnp.float32
BF16 = jnp.bfloat16

POOL_WINDOWS = (2, 4, 8, 16)
POOL_HALO = 16
HEAD_DIM = 128
MOBA_BLOCK = 256
MOBA_TOP_K = 3
PAGE_SIZE = 128
PAGES_PER_BLOCK = MOBA_BLOCK // PAGE_SIZE
PAST_LEN = 16384
ROPE_THETA = 10000.0
RMS_EPS = 1e-6
NEG_INF = -1e30
GATE_FLOOR = -3.0e38

VMEM_LIMIT_BYTES = 56 * 1024 * 1024


def _params(n_axes):
    return pltpu.CompilerParams(dimension_semantics=("arbitrary",) * n_axes, vmem_limit_bytes=VMEM_LIMIT_BYTES)


def _rmsnorm(x, g):
    ms = jnp.mean(x * x, axis=-1, keepdims=True)
    return x * lax.rsqrt(ms + RMS_EPS) * g


def _pool_kernel(*refs, tm, halo_normalized, pos0, round_weights):
    if round_weights:
        x_ref, halo_ref, g_ref, w_ref, s_ref, gu_in, dn_in, h_ref, zt_ref, gu_out, dn_out, zs_ref = refs
        gu_out[...] = gu_in[...].astype(BF16)
        dn_out[...] = dn_in[...].astype(BF16)
    else:
        x_ref, halo_ref, g_ref, w_ref, s_ref, h_ref, zt_ref, zs_ref = refs
    i = pl.program_id(1)
    d = x_ref.shape[-1]
    c = d // len(POOL_WINDOWS)
    x = x_ref[...]
    g = g_ref[...]
    z = _rmsnorm(x, g)
    if halo_normalized:
        zh = halo_ref[...]
    else:
        zh = _rmsnorm(halo_ref[...], g) * (i > 0).astype(F32)
    zs_ref[0:POOL_HALO, :] = zh
    zs_ref[POOL_HALO:POOL_HALO + tm, :] = z
    pos = lax.broadcasted_iota(jnp.int32, (tm, c), 0) + (i * tm + pos0)
    for gi, w in enumerate(POOL_WINDOWS):
        cols = slice(gi * c, (gi + 1) * c)
        zc = z[:, cols]
        if w <= 8:
            tot = zc
            for j in range(1, w):
                tot = tot + zs_ref[POOL_HALO - j:POOL_HALO - j + tm, cols]
        else:
            assert w % 8 == 0 and w <= POOL_HALO
            lead = w - 8
            s8 = zs_ref[POOL_HALO - lead:POOL_HALO + tm, cols]
            for j in range(1, 8):
                s8 = s8 + zs_ref[POOL_HALO - lead - j:POOL_HALO + tm - j, cols]
            tot = s8[lead:, :]
            for k in range(1, w // 8):
                tot = tot + s8[lead - 8 * k:lead - 8 * k + tm, :]
        cnt = jnp.minimum(pos + 1, w).astype(F32)
        pooled = tot / cnt - zc
        mixed = jnp.dot(pooled.astype(BF16), w_ref[gi].astype(BF16), preferred_element_type=F32)
        h_ref[:, cols] = x[:, cols] + mixed * s_ref[:, cols]

    nt = zt_ref.shape[0]

    @pl.when(i == pl.num_programs(1) - 1)
    def _():
        zt_ref[...] = z[tm - nt:, :]


def _pool_layer(x3, halo3, g, w, s, *, tm, halo_normalized, pos0, round_weights=None):
    nb, length, d = x3.shape
    nt = min(POOL_HALO, tm)
    n_i = length // tm
    hb = tm // POOL_HALO
    if halo_normalized:
        halo_map = lambda b, i: (b, 0, 0)
    else:
        halo_map = lambda b, i: (b, jnp.maximum(i * hb - 1, 0), 0)
    in_specs = [
        pl.BlockSpec((None, tm, d), lambda b, i: (b, i, 0)),
        pl.BlockSpec((None, POOL_HALO, d), halo_map),
        pl.BlockSpec((1, d), lambda b, i: (0, 0)),
        pl.BlockSpec(w.shape, lambda b, i: (0, 0, 0), pipeline_mode=pl.Buffered(1)),
        pl.BlockSpec((1, d), lambda b, i: (0, 0)),
    ]
    out_specs = [
        pl.BlockSpec((None, tm, d), lambda b, i: (b, i, 0)),
        pl.BlockSpec((None, nt, d), lambda b, i: (b, 0, 0)),
    ]
    out_shape = [jax.ShapeDtypeStruct((nb, length, d), F32), jax.ShapeDtypeStruct((nb, nt, d), F32)]
    operands = [x3, halo3, g.reshape(1, d), w, s.reshape(1, d)]
    if round_weights is not None:
        w_gu_all, w_dn_all, layer = round_weights
        n_steps = nb * n_i
        two_f, f = w_gu_all.shape[2], w_dn_all.shape[1]
        gu_rows, dn_rows = d // n_steps, f // n_steps
        assert gu_rows * n_steps == d and dn_rows * n_steps == f
        in_specs += [pl.BlockSpec((None, gu_rows, two_f), lambda b, i: (layer, b * n_i + i, 0)),
                     pl.BlockSpec((None, dn_rows, d), lambda b, i: (layer, b * n_i + i, 0))]
        out_specs += [pl.BlockSpec((gu_rows, two_f), lambda b, i: (b * n_i + i, 0)),
                      pl.BlockSpec((dn_rows, d), lambda b, i: (b * n_i + i, 0))]
        out_shape += [jax.ShapeDtypeStruct((d, two_f), BF16), jax.ShapeDtypeStruct((f, d), BF16)]
        operands += [w_gu_all, w_dn_all]
    kern = functools.partial(_pool_kernel, tm=tm, halo_normalized=halo_normalized, pos0=pos0,
                             round_weights=round_weights is not None)
    return pl.pallas_call(
        kern,
        grid=(nb, n_i),
        in_specs=in_specs,
        out_specs=out_specs,
        out_shape=out_shape,
        scratch_shapes=[pltpu.VMEM((POOL_HALO + tm, d), F32)],
        compiler_params=_params(2),
        name="pool_layer",
    )(*operands)


def _stream_block_means(pt_ref, cache_hbm, bm_ref, page_buf, page_sem, *, first_page, n_call_pages):
    step = pl.program_id(0) * pl.num_programs(1) + pl.program_id(1)
    pages_per_step = page_buf.shape[1]
    blocks_per_step = pages_per_step // PAGES_PER_BLOCK
    n_fetch = n_call_pages // pages_per_step

    def page_copy(at_step, p, slot):
        phys = pt_ref[first_page + at_step * pages_per_step + p]
        return pltpu.make_async_copy(cache_hbm.at[phys], page_buf.at[slot, p], page_sem.at[slot])

    def start_step(at_step, slot):
        for p in range(pages_per_step):
            page_copy(at_step, p, slot).start(priority=1)

    @pl.when(step == 0)
    def _():
        start_step(0, 0)

    @pl.when(step + 1 < n_fetch)
    def _():
        start_step(step + 1, (step + 1) % 2)

    @pl.when(step < n_fetch)
    def _():
        for p in range(pages_per_step):
            page_copy(step, p, step % 2).wait()

    def add_pages():
        src = jnp.minimum(step, n_fetch - 1)
        slot = src % 2
        for b in range(blocks_per_step):
            rows = sum(jnp.sum(page_buf[slot, PAGES_PER_BLOCK * b + pg], axis=0) for pg in range(PAGES_PER_BLOCK))
            bm_ref[src * blocks_per_step + b] = rows * (1.0 / (PAGES_PER_BLOCK * PAGE_SIZE))

    return add_pages


def _ffn_kernel(*refs, final_norm, means, round_next):
    refs = list(refs)
    take = lambda n: [refs.pop(0) for _ in range(n)]
    pt_ref, = take(1) if means is not None else (None,)
    x_ref, xs_ref, g_ref, wg_ref, wu_ref, wd_ref, go_ref = take(7)
    cache_hbm, = take(1) if means is not None else (None,)
    next_in = take(2) if round_next else None
    o_ref, os_ref = take(2)
    bm_ref, = take(1) if means is not None else (None,)
    next_out = take(2) if round_next else None
    xn_ref, xns_ref = take(2)
    side_work = None
    if means is not None:
        page_buf, page_sem = take(2)
        side_work = _stream_block_means(pt_ref, cache_hbm, bm_ref, page_buf, page_sem, **means)
    j = pl.program_id(1)

    def rows(x_ref, o_ref, xn_ref, with_side_jobs):
        @pl.when(j == 0)
        def _():
            x = x_ref[...]
            xn_ref[...] = _rmsnorm(x, g_ref[...]).astype(BF16)
            o_ref[...] = x

        xn = xn_ref[...]
        gate = jnp.dot(xn, wg_ref[...], preferred_element_type=F32)
        up = jnp.dot(xn, wu_ref[...], preferred_element_type=F32)
        act = (gate * jax.nn.sigmoid(gate) * up).astype(BF16)
        o_ref[...] += jnp.dot(act, wd_ref[...], preferred_element_type=F32)
        if with_side_jobs and round_next:
            for src, dst in zip(next_in, next_out):
                dst[...] = src[...].astype(BF16)
        if with_side_jobs and side_work is not None:
            side_work()

        if final_norm:
            @pl.when(j == pl.num_programs(1) - 1)
            def _():
                o_ref[...] = _rmsnorm(o_ref[...], go_ref[...])

    rows(x_ref, o_ref, xn_ref, True)

    @pl.when(pl.program_id(0) == pl.num_programs(0) - 1)
    def _():
        rows(xs_ref, os_ref, xns_ref, False)


def _ffn(x, xs, g, w_gu_b, w_dn_b, g_out, *, tm, tf, final_norm, cache_means=None, next_weights=None):
    m, d = x.shape
    m_s = xs.shape[0]
    f = w_dn_b.shape[0]
    n_f = f // tf
    grid = (m // tm, n_f)
    in_specs = [
        pl.BlockSpec((tm, d), lambda i, j, *_: (i, 0)),
        pl.BlockSpec((m_s, d), lambda i, j, *_: (0, 0)),
        pl.BlockSpec((1, d), lambda i, j, *_: (0, 0)),
        pl.BlockSpec((d, tf), lambda i, j, *_: (0, j)),
        pl.BlockSpec((d, tf), lambda i, j, *_: (0, j + n_f)),
        pl.BlockSpec((tf, d), lambda i, j, *_: (j, 0)),
        pl.BlockSpec((1, d), lambda i, j, *_: (0, 0)),
    ]
    out_specs = [pl.BlockSpec((tm, d), lambda i, j, *_: (i, 0)), pl.BlockSpec((m_s, d), lambda i, j, *_: (0, 0))]
    out_shape = [jax.ShapeDtypeStruct((m, d), F32), jax.ShapeDtypeStruct((m_s, d), F32)]
    scratch = [pltpu.VMEM((tm, d), BF16), pltpu.VMEM((m_s, d), BF16)]
    operands = [x, xs, g.reshape(1, d), w_gu_b, w_gu_b, w_dn_b, g_out.reshape(1, d)]
    next_specs, next_shapes, next_operands = [], [], []
    if next_weights is not None:
        w_gu_all, w_dn_all, nxt = next_weights
        gu_chunk = (d // grid[0], 2 * f // grid[1])
        dn_rows = f // (grid[0] * grid[1])
        assert gu_chunk[0] * grid[0] == d and gu_chunk[1] * grid[1] == 2 * f and dn_rows * grid[0] * grid[1] == f
        next_operands = [w_gu_all, w_dn_all]
        next_specs = [pl.BlockSpec((None,) + gu_chunk, lambda i, j, *_: (nxt, i, j)),
                      pl.BlockSpec((None, dn_rows, d), lambda i, j, *_: (nxt, i * n_f + j, 0))]
        next_out_specs = [pl.BlockSpec(gu_chunk, lambda i, j, *_: (i, j)),
                          pl.BlockSpec((dn_rows, d), lambda i, j, *_: (i * n_f + j, 0))]
        next_shapes = [jax.ShapeDtypeStruct((d, 2 * f), BF16), jax.ShapeDtypeStruct((f, d), BF16)]
    if cache_means is None:
        means = None
        prefetch = []
    else:
        cache_k, page_table_flat, first_page, n_call_pages = cache_means
        _, page, n_h, dh = cache_k.shape
        blocks_per_step = -(-n_call_pages // (PAGES_PER_BLOCK * grid[0] * grid[1]))
        while n_call_pages % (PAGES_PER_BLOCK * blocks_per_step):
            blocks_per_step += 1
        pages_per_step = PAGES_PER_BLOCK * blocks_per_step
        means = dict(first_page=first_page, n_call_pages=n_call_pages)
        prefetch = [page_table_flat]
        in_specs.append(pl.BlockSpec(memory_space=pl.ANY))
        operands.append(cache_k)
        n_call_blk = n_call_pages // PAGES_PER_BLOCK
        out_specs.append(pl.BlockSpec((n_call_blk, n_h, dh), lambda i, j, *_: (0, 0, 0)))
        out_shape.append(jax.ShapeDtypeStruct((n_call_blk, n_h, dh), F32))
        scratch += [pltpu.VMEM((2, pages_per_step, page, n_h, dh), F32), pltpu.SemaphoreType.DMA((2,))]
    if next_weights is not None:
        in_specs += next_specs
        operands += next_operands
        out_specs += next_out_specs
        out_shape += next_shapes
    grid_spec = pltpu.PrefetchScalarGridSpec(
        num_scalar_prefetch=len(prefetch), grid=grid, in_specs=in_specs, out_specs=out_specs, scratch_shapes=scratch)
    outs = list(pl.pallas_call(
        functools.partial(_ffn_kernel, final_norm=final_norm, means=means, round_next=next_weights is not None),
        grid_spec=grid_spec,
        out_shape=out_shape,
        compiler_params=_params(2),
        name="swiglu",
    )(*prefetch, *operands))
    out, out_small = outs[:2]
    del outs[:2]
    block_means = outs.pop(0) if cache_means is not None else None
    next_b = tuple(outs) if next_weights is not None else None
    return out, out_small, block_means, next_b


def _rope(xh, cos, sin_signed):
    return xh * cos + pltpu.roll(xh, HEAD_DIM // 2, axis=1) * sin_signed


def _round_weight_once(w_ref, wb_ref):
    @pl.when(pl.program_id(0) == 0)
    def _():
        wb_ref[...] = w_ref[...].astype(BF16)


def _prompt_then_sample(n_tiles, prompt_fn, sample_fn):
    i = pl.program_id(0)
    pl.when(i < n_tiles)(prompt_fn)
    pl.when(i == n_tiles)(sample_fn)


def _proj_specs(tm, d, m_s, n_tiles):
    del d
    tile = lambda i: jnp.minimum(i, n_tiles - 1)
    prompt = lambda width: pl.BlockSpec((tm, width), lambda i: (tile(i), 0))
    sample = lambda width: pl.BlockSpec((m_s, width), lambda i: (0, 0))
    return tile, prompt, sample


def _q_kernel(xp_ref, xs_ref, g_ref, w_ref, cosp_ref, sinp_ref, coss_ref, sins_ref, qp_ref, qs_ref, wb_ref, *, n_tiles):
    _round_weight_once(w_ref, wb_ref)

    def rows(x_ref, cos_ref, sin_ref, q_ref):
        xn = _rmsnorm(x_ref[...], g_ref[...]).astype(BF16)
        acc = jnp.dot(xn, wb_ref[...], preferred_element_type=F32)
        cos = cos_ref[...]
        sin = sin_ref[...]
        for h in range(acc.shape[1] // HEAD_DIM):
            cols = slice(h * HEAD_DIM, (h + 1) * HEAD_DIM)
            q_ref[:, cols] = _rope(acc[:, cols], cos, sin)

    _prompt_then_sample(n_tiles, lambda: rows(xp_ref, cosp_ref, sinp_ref, qp_ref),
                        lambda: rows(xs_ref, coss_ref, sins_ref, qs_ref))


def _q_proj(xp, xs, g, w, layer, cos_p, sin_p, cos_s, sin_s, *, tm):
    m, d = xp.shape
    m_s = xs.shape[0]
    n_tiles = m // tm
    n_pos = cos_p.shape[0] // tm
    tile, prompt, sample = _proj_specs(tm, d, m_s, n_tiles)
    rope_p = pl.BlockSpec((tm, HEAD_DIM), lambda i: (tile(i) % n_pos, 0))
    return pl.pallas_call(
        functools.partial(_q_kernel, n_tiles=n_tiles),
        grid=(n_tiles + 1,),
        in_specs=[
            prompt(d), sample(d),
            pl.BlockSpec((1, d), lambda i: (0, 0)),
            pl.BlockSpec((None, d, d), lambda i: (layer, 0, 0), pipeline_mode=pl.Buffered(1)),
            rope_p, rope_p, sample(HEAD_DIM), sample(HEAD_DIM),
        ],
        out_specs=[prompt(d), sample(d)],
        out_shape=[jax.ShapeDtypeStruct((m, d), F32), jax.ShapeDtypeStruct((m_s, d), F32)],
        scratch_shapes=[pltpu.VMEM((d, d), BF16)],
        compiler_params=_params(1),
        name="q_proj",
    )(xp, xs, g.reshape(1, d), w, cos_p, sin_p, cos_s, sin_s)


def _k_kernel(xp_ref, xs_ref, g_ref, w_ref, cosp_ref, sinp_ref, coss_ref, sins_ref, kp_ref, kb_ref, km_ref, ks_ref,
              wb_ref, *, n_tiles):
    _round_weight_once(w_ref, wb_ref)

    def rows(x_ref, cos_ref, sin_ref, k_ref):
        xn = _rmsnorm(x_ref[...], g_ref[...]).astype(BF16)
        acc = jnp.dot(xn, wb_ref[...], preferred_element_type=F32)
        cos = cos_ref[...]
        sin = sin_ref[...]
        for h in range(acc.shape[1] // HEAD_DIM):
            cols = slice(h * HEAD_DIM, (h + 1) * HEAD_DIM)
            k_ref[:, cols] = _rope(acc[:, cols], cos, sin)

    def prompt_rows():
        rows(xp_ref, cosp_ref, sinp_ref, kp_ref)
        k = kp_ref[...]
        kb_ref[...] = k.astype(BF16)
        for r in range(k.shape[0] // MOBA_BLOCK):
            km_ref[r] = jnp.mean(k[r * MOBA_BLOCK:(r + 1) * MOBA_BLOCK, :], axis=0, keepdims=True)

    _prompt_then_sample(n_tiles, prompt_rows, lambda: rows(xs_ref, coss_ref, sins_ref, ks_ref))


def _k_proj(xp, xs, g, w_kv, cos_p, sin_p, cos_s, sin_s, *, tm):
    m, d = xp.shape
    m_s = xs.shape[0]
    n_tiles = m // tm
    n_pos = cos_p.shape[0] // tm
    r = tm // MOBA_BLOCK
    tile, prompt, sample = _proj_specs(tm, d, m_s, n_tiles)
    rope_p = pl.BlockSpec((tm, HEAD_DIM), lambda i: (tile(i) % n_pos, 0))
    return pl.pallas_call(
        functools.partial(_k_kernel, n_tiles=n_tiles),
        grid=(n_tiles + 1,),
        in_specs=[
            prompt(d), sample(d),
            pl.BlockSpec((1, d), lambda i: (0, 0)),
            pl.BlockSpec((d, d), lambda i: (0, 0), pipeline_mode=pl.Buffered(1)),
            rope_p, rope_p, sample(HEAD_DIM), sample(HEAD_DIM),
        ],
        out_specs=[prompt(d), prompt(d), pl.BlockSpec((r, 1, d), lambda i: (tile(i), 0, 0)), sample(d)],
        out_shape=[jax.ShapeDtypeStruct((m, d), F32), jax.ShapeDtypeStruct((m, d), BF16),
                   jax.ShapeDtypeStruct((m // MOBA_BLOCK, 1, d), F32), jax.ShapeDtypeStruct((m_s, d), F32)],
        scratch_shapes=[pltpu.VMEM((d, d), BF16)],
        compiler_params=_params(1),
        name="k_proj",
    )(xp, xs, g.reshape(1, d), w_kv, cos_p, sin_p, cos_s, sin_s)


def _v_kernel(xp_ref, xs_ref, g_ref, w_ref, vp_ref, vt_ref, vs_ref, wb_ref, *, n_tiles):
    _round_weight_once(w_ref, wb_ref)

    def values(x_ref):
        xn = _rmsnorm(x_ref[...], g_ref[...]).astype(BF16)
        return jnp.dot(xn, wb_ref[...], preferred_element_type=F32)

    def prompt_rows():
        v = values(xp_ref)
        vp_ref[...] = v
        for h in range(v.shape[1] // HEAD_DIM):
            for r in range(v.shape[0] // MOBA_BLOCK):
                blk = v[r * MOBA_BLOCK:(r + 1) * MOBA_BLOCK, h * HEAD_DIM:(h + 1) * HEAD_DIM]
                vt_ref[h, r] = blk.T.astype(BF16)

    def sample_rows():
        vs_ref[...] = values(xs_ref)

    _prompt_then_sample(n_tiles, prompt_rows, sample_rows)


def _v_proj(xp, xs, g, w_kv, *, tm, seq):
    m, d = xp.shape
    m_s = xs.shape[0]
    n_h = d // HEAD_DIM
    n_tiles = m // tm
    r = tm // MOBA_BLOCK
    tiles_per_seq = seq // tm
    tile, prompt, sample = _proj_specs(tm, d, m_s, n_tiles)
    vt_spec = pl.BlockSpec((None, n_h, r, HEAD_DIM, MOBA_BLOCK),
                           lambda i: (tile(i) // tiles_per_seq, 0, tile(i) % tiles_per_seq, 0, 0))
    return pl.pallas_call(
        functools.partial(_v_kernel, n_tiles=n_tiles),
        grid=(n_tiles + 1,),
        in_specs=[
            prompt(d), sample(d),
            pl.BlockSpec((1, d), lambda i: (0, 0)),
            pl.BlockSpec((d, d), lambda i: (0, 1), pipeline_mode=pl.Buffered(1)),
        ],
        out_specs=[prompt(d), vt_spec, sample(d)],
        out_shape=[jax.ShapeDtypeStruct((m, d), F32),
                   jax.ShapeDtypeStruct((m // seq, n_h, seq // MOBA_BLOCK, HEAD_DIM, MOBA_BLOCK), BF16),
                   jax.ShapeDtypeStruct((m_s, d), F32)],
        scratch_shapes=[pltpu.VMEM((d, d), BF16)],
        compiler_params=_params(1),
        name="v_proj",
    )(xp, xs, g.reshape(1, d), w_kv)


def _o_kernel(ap_ref, as_ref, w_ref, hp_ref, hs_ref, op_ref, os_ref, wb_ref, *, n_tiles):
    _round_weight_once(w_ref, wb_ref)

    def rows(a_ref, h_ref, o_ref):
        o_ref[...] = h_ref[...] + jnp.dot(a_ref[...].astype(BF16), wb_ref[...], preferred_element_type=F32)

    _prompt_then_sample(n_tiles, lambda: rows(ap_ref, hp_ref, op_ref), lambda: rows(as_ref, hs_ref, os_ref))


def _o_proj(ap, a_s, w, layer, hp, hs, *, tm):
    m, d = hp.shape
    m_s = hs.shape[0]
    n_tiles = m // tm
    _, prompt, sample = _proj_specs(tm, d, m_s, n_tiles)
    return pl.pallas_call(
        functools.partial(_o_kernel, n_tiles=n_tiles),
        grid=(n_tiles + 1,),
        in_specs=[
            prompt(d), sample(d),
            pl.BlockSpec((None, d, d), lambda i: (layer, 0, 0), pipeline_mode=pl.Buffered(1)),
            prompt(d), sample(d),
        ],
        out_specs=[prompt(d), sample(d)],
        out_shape=[jax.ShapeDtypeStruct((m, d), F32), jax.ShapeDtypeStruct((m_s, d), F32)],
        scratch_shapes=[pltpu.VMEM((d, d), BF16)],
        compiler_params=_params(1),
        name="o_proj",
    )(ap, a_s, w, hp, hs)


def _top_k_picks(gate, candidate, blk_f, axis):
    remaining = candidate
    for _ in range(MOBA_TOP_K):
        gm = jnp.where(remaining, gate, GATE_FLOOR)
        top = jnp.max(gm, axis=axis, keepdims=True)
        hit = jnp.logical_and(remaining, gm == top)
        idx = jnp.min(jnp.where(hit, blk_f, 1e9), axis=axis, keepdims=True)
        pick = blk_f == idx
        remaining = jnp.logical_and(remaining, jnp.logical_not(pick))
        yield pick, idx


def _attn_kernel(q_ref, k_ref, vt_ref, km_ref, o_ref, qt_ref, acc_ref, sa_ref, sb_ref):
    n_g, n_blk = qt_ref.shape[0], qt_ref.shape[1]
    tq = MOBA_BLOCK
    q_scale = HEAD_DIM ** -0.5 * 1.4426950408889634
    blk_i = lax.broadcasted_iota(jnp.int32, (n_blk, tq), 0)
    blk_f = blk_i.astype(F32)
    heads = [slice(g * HEAD_DIM, (g + 1) * HEAD_DIM) for g in range(n_g)]
    assert n_blk <= qt_ref.shape[2] - HEAD_DIM

    pad = jnp.zeros((qt_ref.shape[2] - HEAD_DIM - n_blk, tq), BF16)
    for g in range(n_g):
        km = km_ref[:, heads[g]]
        for i in range(n_blk):
            q_t = q_ref[i * tq:(i + 1) * tq, heads[g]].T
            qt_ref[g, i, 0:HEAD_DIM, :] = (q_t * q_scale).astype(BF16)
            if i <= MOBA_TOP_K:
                selected = blk_i < i
            else:
                gate_t = jnp.dot(km, q_t, preferred_element_type=F32, precision=lax.Precision.HIGHEST)
                selected = jnp.zeros((n_blk, tq), jnp.bool_)
                for pick, _ in _top_k_picks(gate_t, blk_i < i, blk_f, 0):
                    selected = jnp.logical_or(selected, pick)
            qt_ref[g, i, HEAD_DIM:HEAD_DIM + n_blk, :] = jnp.where(selected, 0.0, NEG_INF).astype(BF16)
            qt_ref[g, i, HEAD_DIM + n_blk:, :] = pad

    kpos = lax.broadcasted_iota(jnp.int32, (tq, tq), 0)
    qpos = lax.broadcasted_iota(jnp.int32, (tq, tq), 1)
    causal = kpos <= qpos

    last_pair = n_blk // 2 - 1
    pair_row = lax.broadcasted_iota(jnp.int32, (2 * tq, HEAD_DIM), 0)
    pair_lane = lax.broadcasted_iota(jnp.int32, (2 * tq, HEAD_DIM), 1)
    lane_minus_half = pair_lane - jnp.where(pair_row >= tq, 1, 0)
    sum_rows = acc_ref.shape[1] - HEAD_DIM
    ones_rows = jnp.where(lax.broadcasted_iota(jnp.int32, (sum_rows, tq), 0) == 0, 1.0, 0.0).astype(BF16)

    def values_aug(g, j):
        return jnp.concatenate([vt_ref[g, j], ones_rows], axis=0)

    def tile(i, _):
        start = pl.multiple_of(i * tq, tq)

        def score_pair(pair_idx, dst_ref):
            pair_c = jnp.minimum(pair_idx, last_pair)
            st = pl.multiple_of(pair_c * (2 * tq), 2 * tq)
            onehot = jnp.where(lane_minus_half == 2 * pair_c, 1.0, 0.0).astype(BF16)
            for g in range(n_g):
                keys_aug = jnp.concatenate([k_ref[pl.ds(st, 2 * tq), heads[g]], onehot], axis=1)
                dst_ref[g] = jnp.dot(keys_aug, qt_ref[g, i], preferred_element_type=F32)

        def fold_pair(src_ref, pair_idx, ms):
            j0 = 2 * jnp.minimum(pair_idx, last_pair)
            new_ms, alphas, ps = [], [], []
            for g in range(n_g):
                sa = src_ref[g, 0:tq, :]
                sb = src_ref[g, tq:2 * tq, :]
                m_blk = jnp.maximum(jnp.max(sa, axis=0, keepdims=True), jnp.max(sb, axis=0, keepdims=True))
                m_new = jnp.maximum(ms[g], m_blk)
                new_ms.append(m_new)
                alphas.append(jnp.exp2(ms[g] - m_new))
                ps.append((jnp.exp2(sa - m_new).astype(BF16), jnp.exp2(sb - m_new).astype(BF16)))
            for g in range(n_g):
                pv = (jnp.dot(values_aug(g, j0), ps[g][0], preferred_element_type=F32)
                      + jnp.dot(values_aug(g, j0 + 1), ps[g][1], preferred_element_type=F32))
                acc_ref[g] = alphas[g] * acc_ref[g] + pv
            return tuple(new_ms)

        ms = []
        s_own = [jnp.dot(k_ref[pl.ds(start, tq), heads[g]], qt_ref[g, i, 0:HEAD_DIM, :], preferred_element_type=F32)
                 for g in range(n_g)]
        score_pair(0, sa_ref)
        p_own = []
        for g in range(n_g):
            s = jnp.where(causal, s_own[g], NEG_INF)
            m0 = jnp.max(s, axis=0, keepdims=True)
            ms.append(m0)
            p_own.append(jnp.exp2(s - m0).astype(BF16))
        for g in range(n_g):
            acc_ref[g] = jnp.dot(values_aug(g, i), p_own[g], preferred_element_type=F32)

        def two_pairs(t, ms):
            score_pair(2 * t + 1, sb_ref)
            ms = fold_pair(sa_ref, 2 * t, ms)
            score_pair(2 * t + 2, sa_ref)
            return fold_pair(sb_ref, 2 * t + 1, ms)

        n_pairs = (i + 1) // 2
        ms = lax.fori_loop(0, n_pairs // 2, two_pairs, tuple(ms))

        @pl.when(n_pairs % 2 == 1)
        def _():
            fold_pair(sa_ref, n_pairs - 1, ms)

        for g in range(n_g):
            inv_l = 1.0 / acc_ref[g, HEAD_DIM:HEAD_DIM + 1, :]
            o_ref[pl.ds(start, tq), heads[g]] = (acc_ref[g, 0:HEAD_DIM, :] * inv_l).T.astype(o_ref.dtype)
        return 0

    lax.fori_loop(0, n_blk, tile, 0)


ATTN_HEADS_PER_STEP = 4


ATTN_SUM_ROWS = 16


def _moba_prompt(q, k_b, vt_b, kmean, *, batch, seq):
    m, d = q.shape
    n_g = ATTN_HEADS_PER_STEP
    n_blk = seq // MOBA_BLOCK
    width = n_g * HEAD_DIM
    return pl.pallas_call(
        _attn_kernel,
        grid=(batch, d // width),
        in_specs=[
            pl.BlockSpec((seq, width), lambda b, h: (b, h)),
            pl.BlockSpec((seq, width), lambda b, h: (b, h)),
            pl.BlockSpec((None, n_g, n_blk, HEAD_DIM, MOBA_BLOCK), lambda b, h: (b, h, 0, 0, 0)),
            pl.BlockSpec((None, n_blk, width), lambda b, h: (b, 0, h)),
        ],
        out_specs=pl.BlockSpec((seq, width), lambda b, h: (b, h), pipeline_mode=pl.Buffered(1)),
        out_shape=jax.ShapeDtypeStruct((m, d), BF16),
        scratch_shapes=[
            pltpu.VMEM((n_g, n_blk, 2 * HEAD_DIM, MOBA_BLOCK), BF16),
            pltpu.VMEM((n_g, HEAD_DIM + ATTN_SUM_ROWS, MOBA_BLOCK), F32),
            pltpu.VMEM((n_g, 2 * MOBA_BLOCK, MOBA_BLOCK), F32),
            pltpu.VMEM((n_g, 2 * MOBA_BLOCK, MOBA_BLOCK), F32),
        ],
        compiler_params=_params(2),
        name="moba_prompt",
    )(q, k_b, vt_b, kmean)


def _select_kernel(q_ref, bm_ref, sel_ref, *, n_valid):
    q = q_ref[...]
    t = q.shape[0]
    n_lane = bm_ref.shape[1]
    lane = lax.broadcasted_iota(jnp.int32, (t, n_lane), 1)
    lane_f = lane.astype(F32)
    out = jnp.zeros((t, n_lane), jnp.int32)
    for h in range(q.shape[1] // HEAD_DIM):
        cols = slice(h * HEAD_DIM, (h + 1) * HEAD_DIM)
        gate = lax.dot_general(q[:, cols], bm_ref[h], (((1,), (1,)), ((), ())),
                               preferred_element_type=F32, precision=lax.Precision.HIGHEST)
        for r, (_, idx) in enumerate(_top_k_picks(gate, lane < n_valid, lane_f, 1)):
            out = jnp.where(lane == h * MOBA_TOP_K + r, idx.astype(jnp.int32), out)
    sel_ref[...] = out


def _select_blocks(q, bm_pad, *, n_seq, n_valid):
    m, d = q.shape
    t = m // n_seq
    _, n_h, n_lane, dh = bm_pad.shape
    return pl.pallas_call(
        functools.partial(_select_kernel, n_valid=n_valid),
        grid=(n_seq,),
        in_specs=[
            pl.BlockSpec((t, d), lambda b: (b, 0)),
            pl.BlockSpec((None, n_h, n_lane, dh), lambda b: (b, 0, 0, 0)),
        ],
        out_specs=pl.BlockSpec((None, t, n_lane), lambda b: (b, 0, 0)),
        out_shape=jax.ShapeDtypeStruct((n_seq, t, n_lane), jnp.int32),
        compiler_params=_params(1),
        name="select_blocks",
    )(q, bm_pad)


def _sample_attn_kernel(sel_ref, pt_ref, q_ref, kn_ref, vn_ref, ck_hbm, cv_hbm, o_ref, kg_ref, vg_ref, sem,
                        *, n_pages):
    b = pl.program_id(0)
    h = pl.program_id(1)
    n_h = pl.num_programs(1)
    n_steps = pl.num_programs(0) * n_h
    step = b * n_h + h
    slot = step % 2
    t_len = q_ref.shape[0]
    n_sel = MOBA_TOP_K * MOBA_BLOCK
    scale = HEAD_DIM ** -0.5

    def gather_copies(bb, hh, sl):
        copies = []
        for t in range(t_len):
            for r in range(MOBA_TOP_K):
                blk = sel_ref[((bb * t_len + t) * n_h + hh) * MOBA_TOP_K + r]
                for pg in range(PAGES_PER_BLOCK):
                    phys = pt_ref[bb * n_pages + blk * PAGES_PER_BLOCK + pg]
                    rows = pl.ds((r * PAGES_PER_BLOCK + pg) * PAGE_SIZE, PAGE_SIZE)
                    copies.append(pltpu.make_async_copy(ck_hbm.at[phys, :, hh, :], kg_ref.at[sl, t, rows, :], sem.at[sl, 0]))
                    copies.append(pltpu.make_async_copy(cv_hbm.at[phys, :, hh, :], vg_ref.at[sl, t, rows, :], sem.at[sl, 1]))
        return copies

    @pl.when(step == 0)
    def _():
        for cp in gather_copies(b, h, slot):
            cp.start()

    @pl.when(step + 1 < n_steps)
    def _():
        nxt = step + 1
        for cp in gather_copies(nxt // n_h, nxt % n_h, 1 - slot):
            cp.start()

    for cp in gather_copies(b, h, slot):
        cp.wait()

    q = q_ref[...]
    k_own = kn_ref[...]
    v_own = vn_ref[...]
    own_i = lax.broadcasted_iota(jnp.int32, (t_len, 1), 0)
    rows = []
    for t in range(t_len):
        q_t = q[t:t + 1, :]
        s_sel = jnp.sum(kg_ref[slot, t] * q_t, axis=1, keepdims=True) * scale
        s_o = jnp.sum(k_own * q_t, axis=1, keepdims=True) * scale
        s_o = jnp.where(own_i <= t, s_o, NEG_INF)
        m = jnp.maximum(jnp.max(s_sel, axis=0, keepdims=True), jnp.max(s_o, axis=0, keepdims=True))
        p_sel = jnp.exp(s_sel - m)
        p_o = jnp.exp(s_o - m)
        l = jnp.sum(p_sel, axis=0, keepdims=True) + jnp.sum(p_o, axis=0, keepdims=True)
        out = (jnp.sum(p_sel * vg_ref[slot, t], axis=0, keepdims=True)
               + jnp.sum(p_o * v_own, axis=0, keepdims=True))
        rows.append(out / l)
    o_ref[...] = jnp.concatenate(rows, axis=0)


def _moba_sample(q, k_new, v_new, cache_k3, cache_v3, sel_flat, page_table_flat, *, n_seq, n_pages):
    m, d = q.shape
    t = m // n_seq
    n_h = d // HEAD_DIM
    n_sel = MOBA_TOP_K * MOBA_BLOCK
    row_spec = pl.BlockSpec((t, HEAD_DIM), lambda b, h, sel, pt: (b, h))
    grid_spec = pltpu.PrefetchScalarGridSpec(
        num_scalar_prefetch=2,
        grid=(n_seq, n_h),
        in_specs=[row_spec, row_spec, row_spec, pl.BlockSpec(memory_space=pl.ANY), pl.BlockSpec(memory_space=pl.ANY)],
        out_specs=row_spec,
        scratch_shapes=[
            pltpu.VMEM((2, t, n_sel, HEAD_DIM), F32),
            pltpu.VMEM((2, t, n_sel, HEAD_DIM), F32),
            pltpu.SemaphoreType.DMA((2, 2)),
        ],
    )
    return pl.pallas_call(
        functools.partial(_sample_attn_kernel, n_pages=n_pages),
        grid_spec=grid_spec,
        out_shape=jax.ShapeDtypeStruct((m, d), F32),
        compiler_params=_params(2),
        name="moba_sample",
    )(sel_flat, page_table_flat, q, k_new, v_new, cache_k3, cache_v3)


def _rope_tables(pos):
    half = HEAD_DIM // 2
    inv = 1.0 / (ROPE_THETA ** (jnp.arange(half, dtype=F32) * (2.0 / HEAD_DIM)))
    ang = pos.astype(F32)[:, None] * inv[None, :]
    cos = jnp.cos(ang)
    sin = jnp.sin(ang)
    return jnp.concatenate([cos, cos], axis=-1), jnp.concatenate([-sin, sin], axis=-1)


def kernel(x_prompt, x_sample, state_pool, cache_k, cache_v, page_table, g_pool, w_pool, s_pool, g_ffn, w_gate_up,
           w_down, g_kv, w_kv, g_attn, w_q, w_o, g_final):
    n_b, seq, d = x_prompt.shape
    n_db, t_dec, _ = x_sample.shape
    depth = g_ffn.shape[0]
    n_pool = g_pool.shape[0]
    n_h = d // HEAD_DIM
    n_pages = page_table.shape[1]
    n_past_blk = PAST_LEN // MOBA_BLOCK
    m_p = n_b * seq
    m_s = n_db * t_dec
    state_rows = state_pool.shape[2]
    assert n_pages == n_past_blk * PAGES_PER_BLOCK, "own MoBA block must hold only the new tokens"
    assert seq % MOBA_BLOCK == 0 and state_rows == POOL_HALO - 1

    tm_pool, tm_ffn, tf, tm_proj = 512, 512, 512, 512


    cos_p, sin_p = _rope_tables(jnp.arange(seq))
    cos_s, sin_s = _rope_tables(PAST_LEN + jnp.arange(t_dec))
    cos_s = jnp.tile(cos_s, (n_db, 1))
    sin_s = jnp.tile(sin_s, (n_db, 1))

    pt_flat = page_table.reshape(-1)

    hp = x_prompt
    hs = x_sample
    pool_p, pool_s, bm_parts = [], [], []
    assert n_pool > 0 and (n_db * n_pages) % (n_pool * PAGES_PER_BLOCK) == 0
    for layer in range(depth):
        last = layer == depth - 1
        if layer < n_pool:
            first = (w_gate_up, w_down, 0) if layer == 0 else None
            hp3, zt_p, *first_b = _pool_layer(hp.reshape(n_b, seq, d), hp.reshape(n_b, seq, d), g_pool[layer],
                                              w_pool[layer], s_pool[layer], tm=tm_pool, halo_normalized=False, pos0=0,
                                              round_weights=first)
            if first_b:
                w_gu_b, w_dn_b = first_b
            state = state_pool[layer]
            halo_s = jnp.pad(state, ((0, 0), (POOL_HALO - state_rows, 0), (0, 0)))
            hs3, zt_s = _pool_layer(hs.reshape(n_db, t_dec, d), halo_s, g_pool[layer], w_pool[layer], s_pool[layer],
                                    tm=t_dec, halo_normalized=True, pos0=PAST_LEN)
            pool_p.append(zt_p[:, POOL_HALO - state_rows:])
            pool_s.append(jnp.concatenate([state, zt_s], axis=1)[:, -state_rows:])
            hp = hp3.reshape(m_p, d)
            hs = hs3.reshape(m_s, d)
        else:
            a = layer - n_pool
            if a == 0:
                k_p, k_pb, kmean, k_s = _k_proj(hp, hs, g_kv, w_kv, cos_p, sin_p, cos_s, sin_s, tm=tm_proj)
                v_p, vt_pb, v_s = _v_proj(hp, hs, g_kv, w_kv, tm=tm_proj, seq=seq)
                kmean = kmean.reshape(n_b, seq // MOBA_BLOCK, d)
                bm = jnp.concatenate(bm_parts, axis=0).reshape(n_db, n_past_blk, n_h, HEAD_DIM)
                bm_pad = jnp.pad(bm.transpose(0, 2, 1, 3), ((0, 0), (0, 0), (0, 128 - n_past_blk), (0, 0)))
            q_p, q_s = _q_proj(hp, hs, g_attn[a], w_q, a, cos_p, sin_p, cos_s, sin_s, tm=tm_proj)
            att_p = _moba_prompt(q_p, k_pb, vt_pb, kmean, batch=n_b, seq=seq)
            sel = _select_blocks(q_s, bm_pad, n_seq=n_db, n_valid=n_past_blk)
            sel_flat = sel[:, :, :n_h * MOBA_TOP_K].reshape(-1)
            att_s = _moba_sample(q_s, k_s, v_s, cache_k, cache_v, sel_flat, pt_flat, n_seq=n_db, n_pages=n_pages)
            hp, hs = _o_proj(att_p, att_s, w_o, a, hp, hs, tm=tm_proj)
        pages_per_call = n_db * n_pages // n_pool
        cache_means = (cache_k, pt_flat, layer * pages_per_call, pages_per_call) if layer < n_pool else None
        next_weights = None if last else (w_gate_up, w_down, layer + 1)
        hp, hs, bm_part, next_b = _ffn(hp, hs, g_ffn[layer], w_gu_b, w_dn_b, g_final, tm=tm_ffn, tf=tf, final_norm=last,
                                       cache_means=cache_means, next_weights=next_weights)
        if bm_part is not None:
            bm_parts.append(bm_part)
        if next_b is not None:
            w_gu_b, w_dn_b = next_b

    y_prompt = hp.reshape(n_b, seq, d)
    y_sample = hs.reshape(n_db, t_dec, d)
    new_pool_prompt = jnp.stack(pool_p)
    new_pool_sample = jnp.stack(pool_s)
    shape_p = (n_b, seq, n_h, HEAD_DIM)
    shape_s = (n_db, t_dec, n_h, HEAD_DIM)
    return (y_prompt, y_sample, new_pool_prompt, new_pool_sample, k_p.reshape(shape_p), v_p.reshape(shape_p),
            k_s.reshape(shape_s), v_s.reshape(shape_s))
```

```python
import functools

import jax
import jax.numpy as jnp
from jax import lax
from jax.experimental import pallas as pl
from jax.experimental.pallas import tpu as pltpu

F32 = jnp.float32
BF16 = jnp.bfloat16

POOL_WINDOWS = (2, 4, 8, 16)
POOL_HALO = 16
HEAD_DIM = 128
MOBA_BLOCK = 256
MOBA_TOP_K = 3
PAGE_SIZE = 128
PAGES_PER_BLOCK = MOBA_BLOCK // PAGE_SIZE
PAST_LEN = 16384
ROPE_THETA = 10000.0
RMS_EPS = 1e-6
NEG_INF = -1e30
GATE_FLOOR = -3.0e38

VMEM_LIMIT_BYTES = 56 * 1024 * 1024
ROW_TILE = 512
FF_CHUNK = 512


def _params(n_axes):
    return pltpu.CompilerParams(dimension_semantics=("arbitrary",) * n_axes, vmem_limit_bytes=VMEM_LIMIT_BYTES)


def _rmsnorm(x, g):
    ms = jnp.mean(x * x, axis=-1, keepdims=True)
    return x * lax.rsqrt(ms + RMS_EPS) * g


def _pool_kernel(*refs, tm, halo_normalized, pos0, round_weights):
    if round_weights:
        x_ref, halo_ref, g_ref, w_ref, s_ref, gu_in, dn_in, h_ref, zt_ref, gu_out, dn_out, zs_ref = refs
        gu_out[...] = gu_in[...].astype(BF16)
        dn_out[...] = dn_in[...].astype(BF16)
    else:
        x_ref, halo_ref, g_ref, w_ref, s_ref, h_ref, zt_ref, zs_ref = refs
    i = pl.program_id(1)
    d = x_ref.shape[-1]
    c = d // len(POOL_WINDOWS)
    x = x_ref[...]
    g = g_ref[...]
    z = _rmsnorm(x, g)
    if halo_normalized:
        zh = halo_ref[...]
    else:
        zh = _rmsnorm(halo_ref[...], g) * (i > 0).astype(F32)
    zs_ref[0:POOL_HALO, :] = zh
    zs_ref[POOL_HALO:POOL_HALO + tm, :] = z
    pos = lax.broadcasted_iota(jnp.int32, (tm, c), 0) + (i * tm + pos0)
    for gi, w in enumerate(POOL_WINDOWS):
        cols = slice(gi * c, (gi + 1) * c)
        zc = z[:, cols]
        if w <= 8:
            tot = zc
            for j in range(1, w):
                tot = tot + zs_ref[POOL_HALO - j:POOL_HALO - j + tm, cols]
        else:
            assert w % 8 == 0 and w <= POOL_HALO
            lead = w - 8
            s8 = zs_ref[POOL_HALO - lead:POOL_HALO + tm, cols]
            for j in range(1, 8):
                s8 = s8 + zs_ref[POOL_HALO - lead - j:POOL_HALO + tm - j, cols]
            tot = s8[lead:, :]
            for k in range(1, w // 8):
                tot = tot + s8[lead - 8 * k:lead - 8 * k + tm, :]
        cnt = jnp.minimum(pos + 1, w).astype(F32)
        pooled = tot / cnt - zc
        mixed = jnp.dot(pooled.astype(BF16), w_ref[gi].astype(BF16), preferred_element_type=F32)
        h_ref[:, cols] = x[:, cols] + mixed * s_ref[:, cols]

    nt = zt_ref.shape[0]

    @pl.when(i == pl.num_programs(1) - 1)
    def _():
        zt_ref[...] = z[tm - nt:, :]


def _pool_layer(x3, halo3, g, w, s, *, tm, halo_normalized, pos0, round_weights=None):
    nb, length, d = x3.shape
    nt = min(POOL_HALO, tm)
    n_i = length // tm
    hb = tm // POOL_HALO
    if halo_normalized:
        halo_map = lambda b, i: (b, 0, 0)
    else:
        halo_map = lambda b, i: (b, jnp.maximum(i * hb - 1, 0), 0)
    in_specs = [
        pl.BlockSpec((None, tm, d), lambda b, i: (b, i, 0)),
        pl.BlockSpec((None, POOL_HALO, d), halo_map),
        pl.BlockSpec((1, d), lambda b, i: (0, 0)),
        pl.BlockSpec(w.shape, lambda b, i: (0, 0, 0), pipeline_mode=pl.Buffered(1)),
        pl.BlockSpec((1, d), lambda b, i: (0, 0)),
    ]
    out_specs = [
        pl.BlockSpec((None, tm, d), lambda b, i: (b, i, 0)),
        pl.BlockSpec((None, nt, d), lambda b, i: (b, 0, 0)),
    ]
    out_shape = [jax.ShapeDtypeStruct((nb, length, d), F32), jax.ShapeDtypeStruct((nb, nt, d), F32)]
    operands = [x3, halo3, g.reshape(1, d), w, s.reshape(1, d)]
    if round_weights is not None:
        w_gu_all, w_dn_all, layer = round_weights
        n_steps = nb * n_i
        two_f, f = w_gu_all.shape[2], w_dn_all.shape[1]
        gu_rows, dn_rows = d // n_steps, f // n_steps
        assert gu_rows * n_steps == d and dn_rows * n_steps == f
        in_specs += [pl.BlockSpec((None, gu_rows, two_f), lambda b, i: (layer, b * n_i + i, 0)),
                     pl.BlockSpec((None, dn_rows, d), lambda b, i: (layer, b * n_i + i, 0))]
        out_specs += [pl.BlockSpec((gu_rows, two_f), lambda b, i: (b * n_i + i, 0)),
                      pl.BlockSpec((dn_rows, d), lambda b, i: (b * n_i + i, 0))]
        out_shape += [jax.ShapeDtypeStruct((d, two_f), BF16), jax.ShapeDtypeStruct((f, d), BF16)]
        operands += [w_gu_all, w_dn_all]
    kern = functools.partial(_pool_kernel, tm=tm, halo_normalized=halo_normalized, pos0=pos0,
                             round_weights=round_weights is not None)
    return pl.pallas_call(
        kern,
        grid=(nb, n_i),
        in_specs=in_specs,
        out_specs=out_specs,
        out_shape=out_shape,
        scratch_shapes=[pltpu.VMEM((POOL_HALO + tm, d), F32)],
        compiler_params=_params(2),
        name="pool_layer",
    )(*operands)


def _stream_block_means(pt_ref, cache_hbm, bm_ref, page_buf, page_sem, *, first_page, n_call_pages):
    step = pl.program_id(0) * pl.num_programs(1) + pl.program_id(1)
    pages_per_step = page_buf.shape[1]
    blocks_per_step = pages_per_step // PAGES_PER_BLOCK
    n_fetch = n_call_pages // pages_per_step

    def page_copy(at_step, p, slot):
        phys = pt_ref[first_page + at_step * pages_per_step + p]
        return pltpu.make_async_copy(cache_hbm.at[phys], page_buf.at[slot, p], page_sem.at[slot])

    def start_step(at_step, slot):
        for p in range(pages_per_step):
            page_copy(at_step, p, slot).start(priority=1)

    @pl.when(step == 0)
    def _():
        start_step(0, 0)

    @pl.when(step + 1 < n_fetch)
    def _():
        start_step(step + 1, (step + 1) % 2)

    @pl.when(step < n_fetch)
    def _():
        for p in range(pages_per_step):
            page_copy(step, p, step % 2).wait()

    def add_pages():
        src = jnp.minimum(step, n_fetch - 1)
        slot = src % 2
        for b in range(blocks_per_step):
            rows = sum(jnp.sum(page_buf[slot, PAGES_PER_BLOCK * b + pg], axis=0) for pg in range(PAGES_PER_BLOCK))
            bm_ref[src * blocks_per_step + b] = rows * (1.0 / (PAGES_PER_BLOCK * PAGE_SIZE))

    return add_pages


def _ffn_kernel(*refs, final_norm, means, round_next):
    refs = list(refs)
    take = lambda n: [refs.pop(0) for _ in range(n)]
    pt_ref, = take(1) if means is not None else (None,)
    x_ref, xs_ref, g_ref, wg_ref, wu_ref, wd_ref, go_ref = take(7)
    cache_hbm, = take(1) if means is not None else (None,)
    next_in = take(2) if round_next else None
    o_ref, os_ref = take(2)
    bm_ref, = take(1) if means is not None else (None,)
    next_out = take(2) if round_next else None
    xn_ref, xns_ref = take(2)
    side_work = None
    if means is not None:
        page_buf, page_sem = take(2)
        side_work = _stream_block_means(pt_ref, cache_hbm, bm_ref, page_buf, page_sem, **means)
    j = pl.program_id(1)

    def rows(x_ref, o_ref, xn_ref, with_side_jobs):
        @pl.when(j == 0)
        def _():
            x = x_ref[...]
            xn_ref[...] = _rmsnorm(x, g_ref[...]).astype(BF16)
            o_ref[...] = x

        xn = xn_ref[...]
        gate = jnp.dot(xn, wg_ref[...], preferred_element_type=F32)
        up = jnp.dot(xn, wu_ref[...], preferred_element_type=F32)
        act = (gate * jax.nn.sigmoid(gate) * up).astype(BF16)
        o_ref[...] += jnp.dot(act, wd_ref[...], preferred_element_type=F32)
        if with_side_jobs and round_next:
            for src, dst in zip(next_in, next_out):
                dst[...] = src[...].astype(BF16)
        if with_side_jobs and side_work is not None:
            side_work()

        if final_norm:
            @pl.when(j == pl.num_programs(1) - 1)
            def _():
                o_ref[...] = _rmsnorm(o_ref[...], go_ref[...])

    rows(x_ref, o_ref, xn_ref, True)

    @pl.when(pl.program_id(0) == pl.num_programs(0) - 1)
    def _():
        rows(xs_ref, os_ref, xns_ref, False)


def _ffn(x, xs, g, w_gu_b, w_dn_b, g_out, *, tm, tf, final_norm, cache_means=None, next_weights=None):
    m, d = x.shape
    m_s = xs.shape[0]
    f = w_dn_b.shape[0]
    n_f = f // tf
    grid = (m // tm, n_f)
    in_specs = [
        pl.BlockSpec((tm, d), lambda i, j, *_: (i, 0)),
        pl.BlockSpec((m_s, d), lambda i, j, *_: (0, 0)),
        pl.BlockSpec((1, d), lambda i, j, *_: (0, 0)),
        pl.BlockSpec((d, tf), lambda i, j, *_: (0, j)),
        pl.BlockSpec((d, tf), lambda i, j, *_: (0, j + n_f)),
        pl.BlockSpec((tf, d), lambda i, j, *_: (j, 0)),
        pl.BlockSpec((1, d), lambda i, j, *_: (0, 0)),
    ]
    out_specs = [pl.BlockSpec((tm, d), lambda i, j, *_: (i, 0)), pl.BlockSpec((m_s, d), lambda i, j, *_: (0, 0))]
    out_shape = [jax.ShapeDtypeStruct((m, d), F32), jax.ShapeDtypeStruct((m_s, d), F32)]
    scratch = [pltpu.VMEM((tm, d), BF16), pltpu.VMEM((m_s, d), BF16)]
    operands = [x, xs, g.reshape(1, d), w_gu_b, w_gu_b, w_dn_b, g_out.reshape(1, d)]
    next_specs, next_shapes, next_operands = [], [], []
    if next_weights is not None:
        w_gu_all, w_dn_all, nxt = next_weights
        gu_chunk = (d // grid[0], 2 * f // grid[1])
        dn_rows = f // (grid[0] * grid[1])
        assert gu_chunk[0] * grid[0] == d and gu_chunk[1] * grid[1] == 2 * f and dn_rows * grid[0] * grid[1] == f
        next_operands = [w_gu_all, w_dn_all]
        next_specs = [pl.BlockSpec((None,) + gu_chunk, lambda i, j, *_: (nxt, i, j)),
                      pl.BlockSpec((None, dn_rows, d), lambda i, j, *_: (nxt, i * n_f + j, 0))]
        next_out_specs = [pl.BlockSpec(gu_chunk, lambda i, j, *_: (i, j)),
                          pl.BlockSpec((dn_rows, d), lambda i, j, *_: (i * n_f + j, 0))]
        next_shapes = [jax.ShapeDtypeStruct((d, 2 * f), BF16), jax.ShapeDtypeStruct((f, d), BF16)]
    if cache_means is None:
        means = None
        prefetch = []
    else:
        cache_k, page_table_flat, first_page, n_call_pages = cache_means
        _, page, n_h, dh = cache_k.shape
        blocks_per_step = -(-n_call_pages // (PAGES_PER_BLOCK * grid[0] * grid[1]))
        while n_call_pages % (PAGES_PER_BLOCK * blocks_per_step):
            blocks_per_step += 1
        pages_per_step = PAGES_PER_BLOCK * blocks_per_step
        means = dict(first_page=first_page, n_call_pages=n_call_pages)
        prefetch = [page_table_flat]
        in_specs.append(pl.BlockSpec(memory_space=pl.ANY))
        operands.append(cache_k)
        n_call_blk = n_call_pages // PAGES_PER_BLOCK
        out_specs.append(pl.BlockSpec((n_call_blk, n_h, dh), lambda i, j, *_: (0, 0, 0)))
        out_shape.append(jax.ShapeDtypeStruct((n_call_blk, n_h, dh), F32))
        scratch += [pltpu.VMEM((2, pages_per_step, page, n_h, dh), F32), pltpu.SemaphoreType.DMA((2,))]
    if next_weights is not None:
        in_specs += next_specs
        operands += next_operands
        out_specs += next_out_specs
        out_shape += next_shapes
    grid_spec = pltpu.PrefetchScalarGridSpec(
        num_scalar_prefetch=len(prefetch), grid=grid, in_specs=in_specs, out_specs=out_specs, scratch_shapes=scratch)
    outs = list(pl.pallas_call(
        functools.partial(_ffn_kernel, final_norm=final_norm, means=means, round_next=next_weights is not None),
        grid_spec=grid_spec,
        out_shape=out_shape,
        compiler_params=_params(2),
        name="swiglu",
    )(*prefetch, *operands))
    out, out_small = outs[:2]
    del outs[:2]
    block_means = outs.pop(0) if cache_means is not None else None
    next_b = tuple(outs) if next_weights is not None else None
    return out, out_small, block_means, next_b


def _rope(xh, cos, sin_signed):
    return xh * cos + pltpu.roll(xh, HEAD_DIM // 2, axis=1) * sin_signed


def _round_weight_once(w_ref, wb_ref):
    @pl.when(pl.program_id(0) == 0)
    def _():
        wb_ref[...] = w_ref[...].astype(BF16)


def _prompt_then_sample(n_tiles, prompt_fn, sample_fn):
    i = pl.program_id(0)
    pl.when(i < n_tiles)(prompt_fn)
    pl.when(i == n_tiles)(sample_fn)


def _proj_specs(tm, m_s, n_tiles):
    tile = lambda i: jnp.minimum(i, n_tiles - 1)
    prompt = lambda width: pl.BlockSpec((tm, width), lambda i: (tile(i), 0))
    sample = lambda width: pl.BlockSpec((m_s, width), lambda i: (0, 0))
    return tile, prompt, sample


def _q_kernel(xp_ref, xs_ref, g_ref, w_ref, cosp_ref, sinp_ref, coss_ref, sins_ref, qp_ref, qs_ref, wb_ref, *, n_tiles):
    _round_weight_once(w_ref, wb_ref)

    def rows(x_ref, cos_ref, sin_ref, q_ref):
        xn = _rmsnorm(x_ref[...], g_ref[...]).astype(BF16)
        acc = jnp.dot(xn, wb_ref[...], preferred_element_type=F32)
        cos = cos_ref[...]
        sin = sin_ref[...]
        for h in range(acc.shape[1] // HEAD_DIM):
            cols = slice(h * HEAD_DIM, (h + 1) * HEAD_DIM)
            q_ref[:, cols] = _rope(acc[:, cols], cos, sin)

    _prompt_then_sample(n_tiles, lambda: rows(xp_ref, cosp_ref, sinp_ref, qp_ref),
                        lambda: rows(xs_ref, coss_ref, sins_ref, qs_ref))


def _q_proj(xp, xs, g, w, layer, cos_p, sin_p, cos_s, sin_s, *, tm):
    m, d = xp.shape
    m_s = xs.shape[0]
    n_tiles = m // tm
    n_pos = cos_p.shape[0] // tm
    tile, prompt, sample = _proj_specs(tm, m_s, n_tiles)
    rope_p = pl.BlockSpec((tm, HEAD_DIM), lambda i: (tile(i) % n_pos, 0))
    return pl.pallas_call(
        functools.partial(_q_kernel, n_tiles=n_tiles),
        grid=(n_tiles + 1,),
        in_specs=[
            prompt(d), sample(d),
            pl.BlockSpec((1, d), lambda i: (0, 0)),
            pl.BlockSpec((None, d, d), lambda i: (layer, 0, 0), pipeline_mode=pl.Buffered(1)),
            rope_p, rope_p, sample(HEAD_DIM), sample(HEAD_DIM),
        ],
        out_specs=[prompt(d), sample(d)],
        out_shape=[jax.ShapeDtypeStruct((m, d), F32), jax.ShapeDtypeStruct((m_s, d), F32)],
        scratch_shapes=[pltpu.VMEM((d, d), BF16)],
        compiler_params=_params(1),
        name="q_proj",
    )(xp, xs, g.reshape(1, d), w, cos_p, sin_p, cos_s, sin_s)


def _k_kernel(xp_ref, xs_ref, g_ref, w_ref, cosp_ref, sinp_ref, coss_ref, sins_ref, kp_ref, kb_ref, km_ref, ks_ref,
              wb_ref, *, n_tiles):
    _round_weight_once(w_ref, wb_ref)

    def rows(x_ref, cos_ref, sin_ref, k_ref):
        xn = _rmsnorm(x_ref[...], g_ref[...]).astype(BF16)
        acc = jnp.dot(xn, wb_ref[...], preferred_element_type=F32)
        cos = cos_ref[...]
        sin = sin_ref[...]
        for h in range(acc.shape[1] // HEAD_DIM):
            cols = slice(h * HEAD_DIM, (h + 1) * HEAD_DIM)
            k_ref[:, cols] = _rope(acc[:, cols], cos, sin)

    def prompt_rows():
        rows(xp_ref, cosp_ref, sinp_ref, kp_ref)
        k = kp_ref[...]
        kb_ref[...] = k.astype(BF16)
        for r in range(k.shape[0] // MOBA_BLOCK):
            km_ref[r] = jnp.mean(k[r * MOBA_BLOCK:(r + 1) * MOBA_BLOCK, :], axis=0, keepdims=True)

    _prompt_then_sample(n_tiles, prompt_rows, lambda: rows(xs_ref, coss_ref, sins_ref, ks_ref))


def _k_proj(xp, xs, g, w_kv, cos_p, sin_p, cos_s, sin_s, *, tm):
    m, d = xp.shape
    m_s = xs.shape[0]
    n_tiles = m // tm
    n_pos = cos_p.shape[0] // tm
    r = tm // MOBA_BLOCK
    tile, prompt, sample = _proj_specs(tm, m_s, n_tiles)
    rope_p = pl.BlockSpec((tm, HEAD_DIM), lambda i: (tile(i) % n_pos, 0))
    return pl.pallas_call(
        functools.partial(_k_kernel, n_tiles=n_tiles),
        grid=(n_tiles + 1,),
        in_specs=[
            prompt(d), sample(d),
            pl.BlockSpec((1, d), lambda i: (0, 0)),
            pl.BlockSpec((d, d), lambda i: (0, 0), pipeline_mode=pl.Buffered(1)),
            rope_p, rope_p, sample(HEAD_DIM), sample(HEAD_DIM),
        ],
        out_specs=[prompt(d), prompt(d), pl.BlockSpec((r, 1, d), lambda i: (tile(i), 0, 0)), sample(d)],
        out_shape=[jax.ShapeDtypeStruct((m, d), F32), jax.ShapeDtypeStruct((m, d), BF16),
                   jax.ShapeDtypeStruct((m // MOBA_BLOCK, 1, d), F32), jax.ShapeDtypeStruct((m_s, d), F32)],
        scratch_shapes=[pltpu.VMEM((d, d), BF16)],
        compiler_params=_params(1),
        name="k_proj",
    )(xp, xs, g.reshape(1, d), w_kv, cos_p, sin_p, cos_s, sin_s)


def _v_kernel(xp_ref, xs_ref, g_ref, w_ref, vp_ref, vt_ref, vs_ref, wb_ref, *, n_tiles):
    _round_weight_once(w_ref, wb_ref)

    def values(x_ref):
        xn = _rmsnorm(x_ref[...], g_ref[...]).astype(BF16)
        return jnp.dot(xn, wb_ref[...], preferred_element_type=F32)

    def prompt_rows():
        v = values(xp_ref)
        vp_ref[...] = v
        for h in range(v.shape[1] // HEAD_DIM):
            for r in range(v.shape[0] // MOBA_BLOCK):
                blk = v[r * MOBA_BLOCK:(r + 1) * MOBA_BLOCK, h * HEAD_DIM:(h + 1) * HEAD_DIM]
                vt_ref[h, r] = blk.T.astype(BF16)

    def sample_rows():
        vs_ref[...] = values(xs_ref)

    _prompt_then_sample(n_tiles, prompt_rows, sample_rows)


def _v_proj(xp, xs, g, w_kv, *, tm, seq):
    m, d = xp.shape
    m_s = xs.shape[0]
    n_h = d // HEAD_DIM
    n_tiles = m // tm
    r = tm // MOBA_BLOCK
    tiles_per_seq = seq // tm
    tile, prompt, sample = _proj_specs(tm, m_s, n_tiles)
    vt_spec = pl.BlockSpec((None, n_h, r, HEAD_DIM, MOBA_BLOCK),
                           lambda i: (tile(i) // tiles_per_seq, 0, tile(i) % tiles_per_seq, 0, 0))
    return pl.pallas_call(
        functools.partial(_v_kernel, n_tiles=n_tiles),
        grid=(n_tiles + 1,),
        in_specs=[
            prompt(d), sample(d),
            pl.BlockSpec((1, d), lambda i: (0, 0)),
            pl.BlockSpec((d, d), lambda i: (0, 1), pipeline_mode=pl.Buffered(1)),
        ],
        out_specs=[prompt(d), vt_spec, sample(d)],
        out_shape=[jax.ShapeDtypeStruct((m, d), F32),
                   jax.ShapeDtypeStruct((m // seq, n_h, seq // MOBA_BLOCK, HEAD_DIM, MOBA_BLOCK), BF16),
                   jax.ShapeDtypeStruct((m_s, d), F32)],
        scratch_shapes=[pltpu.VMEM((d, d), BF16)],
        compiler_params=_params(1),
        name="v_proj",
    )(xp, xs, g.reshape(1, d), w_kv)


def _o_kernel(ap_ref, as_ref, w_ref, hp_ref, hs_ref, op_ref, os_ref, wb_ref, *, n_tiles):
    _round_weight_once(w_ref, wb_ref)

    def rows(a_ref, h_ref, o_ref):
        o_ref[...] = h_ref[...] + jnp.dot(a_ref[...].astype(BF16), wb_ref[...], preferred_element_type=F32)

    _prompt_then_sample(n_tiles, lambda: rows(ap_ref, hp_ref, op_ref), lambda: rows(as_ref, hs_ref, os_ref))


def _o_proj(ap, a_s, w, layer, hp, hs, *, tm):
    m, d = hp.shape
    m_s = hs.shape[0]
    n_tiles = m // tm
    _, prompt, sample = _proj_specs(tm, m_s, n_tiles)
    return pl.pallas_call(
        functools.partial(_o_kernel, n_tiles=n_tiles),
        grid=(n_tiles + 1,),
        in_specs=[
            prompt(d), sample(d),
            pl.BlockSpec((None, d, d), lambda i: (layer, 0, 0), pipeline_mode=pl.Buffered(1)),
            prompt(d), sample(d),
        ],
        out_specs=[prompt(d), sample(d)],
        out_shape=[jax.ShapeDtypeStruct((m, d), F32), jax.ShapeDtypeStruct((m_s, d), F32)],
        scratch_shapes=[pltpu.VMEM((d, d), BF16)],
        compiler_params=_params(1),
        name="o_proj",
    )(ap, a_s, w, hp, hs)


def _top_k_picks(gate, candidate, blk_f, axis):
    remaining = candidate
    for _ in range(MOBA_TOP_K):
        gm = jnp.where(remaining, gate, GATE_FLOOR)
        top = jnp.max(gm, axis=axis, keepdims=True)
        hit = jnp.logical_and(remaining, gm == top)
        idx = jnp.min(jnp.where(hit, blk_f, 1e9), axis=axis, keepdims=True)
        pick = blk_f == idx
        remaining = jnp.logical_and(remaining, jnp.logical_not(pick))
        yield pick, idx


def _attn_kernel(q_ref, k_ref, vt_ref, km_ref, o_ref, qt_ref, acc_ref, sa_ref, sb_ref):
    n_g, n_blk = qt_ref.shape[0], qt_ref.shape[1]
    tq = MOBA_BLOCK
    q_scale = HEAD_DIM ** -0.5 * 1.4426950408889634
    blk_i = lax.broadcasted_iota(jnp.int32, (n_blk, tq), 0)
    blk_f = blk_i.astype(F32)
    heads = [slice(g * HEAD_DIM, (g + 1) * HEAD_DIM) for g in range(n_g)]
    assert n_blk <= qt_ref.shape[2] - HEAD_DIM

    pad = jnp.zeros((qt_ref.shape[2] - HEAD_DIM - n_blk, tq), BF16)
    for g in range(n_g):
        km = km_ref[:, heads[g]]
        for i in range(n_blk):
            q_t = q_ref[i * tq:(i + 1) * tq, heads[g]].T
            qt_ref[g, i, 0:HEAD_DIM, :] = (q_t * q_scale).astype(BF16)
            if i <= MOBA_TOP_K:
                selected = blk_i < i
            else:
                gate_t = jnp.dot(km, q_t, preferred_element_type=F32, precision=lax.Precision.HIGHEST)
                selected = jnp.zeros((n_blk, tq), jnp.bool_)
                for pick, _ in _top_k_picks(gate_t, blk_i < i, blk_f, 0):
                    selected = jnp.logical_or(selected, pick)
            qt_ref[g, i, HEAD_DIM:HEAD_DIM + n_blk, :] = jnp.where(selected, 0.0, NEG_INF).astype(BF16)
            qt_ref[g, i, HEAD_DIM + n_blk:, :] = pad

    kpos = lax.broadcasted_iota(jnp.int32, (tq, tq), 0)
    qpos = lax.broadcasted_iota(jnp.int32, (tq, tq), 1)
    causal = kpos <= qpos

    last_pair = n_blk // 2 - 1
    pair_row = lax.broadcasted_iota(jnp.int32, (2 * tq, HEAD_DIM), 0)
    pair_lane = lax.broadcasted_iota(jnp.int32, (2 * tq, HEAD_DIM), 1)
    lane_minus_half = pair_lane - jnp.where(pair_row >= tq, 1, 0)
    sum_rows = acc_ref.shape[1] - HEAD_DIM
    ones_rows = jnp.where(lax.broadcasted_iota(jnp.int32, (sum_rows, tq), 0) == 0, 1.0, 0.0).astype(BF16)

    def values_aug(g, j):
        return jnp.concatenate([vt_ref[g, j], ones_rows], axis=0)

    def tile(i, _):
        start = pl.multiple_of(i * tq, tq)

        def score_pair(pair_idx, dst_ref):
            pair_c = jnp.minimum(pair_idx, last_pair)
            st = pl.multiple_of(pair_c * (2 * tq), 2 * tq)
            onehot = jnp.where(lane_minus_half == 2 * pair_c, 1.0, 0.0).astype(BF16)
            for g in range(n_g):
                keys_aug = jnp.concatenate([k_ref[pl.ds(st, 2 * tq), heads[g]], onehot], axis=1)
                dst_ref[g] = jnp.dot(keys_aug, qt_ref[g, i], preferred_element_type=F32)

        def fold_pair(src_ref, pair_idx, ms):
            j0 = 2 * jnp.minimum(pair_idx, last_pair)
            new_ms, alphas, ps = [], [], []
            for g in range(n_g):
                sa = src_ref[g, 0:tq, :]
                sb = src_ref[g, tq:2 * tq, :]
                m_blk = jnp.maximum(jnp.max(sa, axis=0, keepdims=True), jnp.max(sb, axis=0, keepdims=True))
                m_new = jnp.maximum(ms[g], m_blk)
                new_ms.append(m_new)
                alphas.append(jnp.exp2(ms[g] - m_new))
                ps.append((jnp.exp2(sa - m_new).astype(BF16), jnp.exp2(sb - m_new).astype(BF16)))
            for g in range(n_g):
                pv = (jnp.dot(values_aug(g, j0), ps[g][0], preferred_element_type=F32)
                      + jnp.dot(values_aug(g, j0 + 1), ps[g][1], preferred_element_type=F32))
                acc_ref[g] = alphas[g] * acc_ref[g] + pv
            return tuple(new_ms)

        ms = []
        s_own = [jnp.dot(k_ref[pl.ds(start, tq), heads[g]], qt_ref[g, i, 0:HEAD_DIM, :], preferred_element_type=F32)
                 for g in range(n_g)]
        score_pair(0, sa_ref)
        p_own = []
        for g in range(n_g):
            s = jnp.where(causal, s_own[g], NEG_INF)
            m0 = jnp.max(s, axis=0, keepdims=True)
            ms.append(m0)
            p_own.append(jnp.exp2(s - m0).astype(BF16))
        for g in range(n_g):
            acc_ref[g] = jnp.dot(values_aug(g, i), p_own[g], preferred_element_type=F32)

        def two_pairs(t, ms):
            score_pair(2 * t + 1, sb_ref)
            ms = fold_pair(sa_ref, 2 * t, ms)
            score_pair(2 * t + 2, sa_ref)
            return fold_pair(sb_ref, 2 * t + 1, ms)

        n_pairs = (i + 1) // 2
        ms = lax.fori_loop(0, n_pairs // 2, two_pairs, tuple(ms))

        @pl.when(n_pairs % 2 == 1)
        def _():
            fold_pair(sa_ref, n_pairs - 1, ms)

        for g in range(n_g):
            inv_l = 1.0 / acc_ref[g, HEAD_DIM:HEAD_DIM + 1, :]
            o_ref[pl.ds(start, tq), heads[g]] = (acc_ref[g, 0:HEAD_DIM, :] * inv_l).T.astype(o_ref.dtype)
        return 0

    lax.fori_loop(0, n_blk, tile, 0)


ATTN_HEADS_PER_STEP = 4


ATTN_SUM_ROWS = 16


def _moba_prompt(q, k_b, vt_b, kmean, *, batch, seq):
    m, d = q.shape
    n_g = ATTN_HEADS_PER_STEP
    n_blk = seq // MOBA_BLOCK
    width = n_g * HEAD_DIM
    return pl.pallas_call(
        _attn_kernel,
        grid=(batch, d // width),
        in_specs=[
            pl.BlockSpec((seq, width), lambda b, h: (b, h)),
            pl.BlockSpec((seq, width), lambda b, h: (b, h)),
            pl.BlockSpec((None, n_g, n_blk, HEAD_DIM, MOBA_BLOCK), lambda b, h: (b, h, 0, 0, 0)),
            pl.BlockSpec((None, n_blk, width), lambda b, h: (b, 0, h)),
        ],
        out_specs=pl.BlockSpec((seq, width), lambda b, h: (b, h), pipeline_mode=pl.Buffered(1)),
        out_shape=jax.ShapeDtypeStruct((m, d), BF16),
        scratch_shapes=[
            pltpu.VMEM((n_g, n_blk, 2 * HEAD_DIM, MOBA_BLOCK), BF16),
            pltpu.VMEM((n_g, HEAD_DIM + ATTN_SUM_ROWS, MOBA_BLOCK), F32),
            pltpu.VMEM((n_g, 2 * MOBA_BLOCK, MOBA_BLOCK), F32),
            pltpu.VMEM((n_g, 2 * MOBA_BLOCK, MOBA_BLOCK), F32),
        ],
        compiler_params=_params(2),
        name="moba_prompt",
    )(q, k_b, vt_b, kmean)


SELECT_LANES = 128


def _select_kernel(q_ref, bm_ref, sel_ref):
    q = q_ref[...]
    t = q.shape[0]
    n_valid = bm_ref.shape[0]
    lane = lax.broadcasted_iota(jnp.int32, (t, SELECT_LANES), 1)
    lane_f = lane.astype(F32)
    out = jnp.zeros((t, SELECT_LANES), jnp.int32)
    no_block = jnp.zeros((SELECT_LANES - n_valid, HEAD_DIM), F32)
    for h in range(q.shape[1] // HEAD_DIM):
        cols = slice(h * HEAD_DIM, (h + 1) * HEAD_DIM)
        means_h = jnp.concatenate([bm_ref[:, h, :], no_block], axis=0)
        gate = lax.dot_general(q[:, cols], means_h, (((1,), (1,)), ((), ())),
                               preferred_element_type=F32, precision=lax.Precision.HIGHEST)
        for r, (_, idx) in enumerate(_top_k_picks(gate, lane < n_valid, lane_f, 1)):
            out = jnp.where(lane == h * MOBA_TOP_K + r, idx.astype(jnp.int32), out)
    sel_ref[...] = out


def _select_blocks(q, bm, *, n_seq):
    m, d = q.shape
    t = m // n_seq
    _, n_blk, n_h, dh = bm.shape
    assert n_blk <= SELECT_LANES and n_h * MOBA_TOP_K <= SELECT_LANES
    return pl.pallas_call(
        _select_kernel,
        grid=(n_seq,),
        in_specs=[
            pl.BlockSpec((t, d), lambda b: (b, 0)),
            pl.BlockSpec((None, n_blk, n_h, dh), lambda b: (b, 0, 0, 0)),
        ],
        out_specs=pl.BlockSpec((None, t, SELECT_LANES), lambda b: (b, 0, 0)),
        out_shape=jax.ShapeDtypeStruct((n_seq, t, SELECT_LANES), jnp.int32),
        compiler_params=_params(1),
        name="select_blocks",
    )(q, bm)


def _sample_attn_kernel(sel_ref, pt_ref, q_ref, kn_ref, vn_ref, ck_hbm, cv_hbm, o_ref, kg_ref, vg_ref, sem,
                        *, n_pages):
    b = pl.program_id(0)
    h = pl.program_id(1)
    n_h = pl.num_programs(1)
    n_steps = pl.num_programs(0) * n_h
    step = b * n_h + h
    slot = step % 2
    t_len = q_ref.shape[0]
    n_sel = MOBA_TOP_K * MOBA_BLOCK
    scale = HEAD_DIM ** -0.5

    def gather_copies(bb, hh, sl):
        copies = []
        for t in range(t_len):
            for r in range(MOBA_TOP_K):
                blk = sel_ref[((bb * t_len + t) * n_h + hh) * MOBA_TOP_K + r]
                for pg in range(PAGES_PER_BLOCK):
                    phys = pt_ref[bb * n_pages + blk * PAGES_PER_BLOCK + pg]
                    rows = pl.ds((r * PAGES_PER_BLOCK + pg) * PAGE_SIZE, PAGE_SIZE)
                    copies.append(pltpu.make_async_copy(ck_hbm.at[phys, :, hh, :], kg_ref.at[sl, t, rows, :], sem.at[sl, 0]))
                    copies.append(pltpu.make_async_copy(cv_hbm.at[phys, :, hh, :], vg_ref.at[sl, t, rows, :], sem.at[sl, 1]))
        return copies

    @pl.when(step == 0)
    def _():
        for cp in gather_copies(b, h, slot):
            cp.start()

    @pl.when(step + 1 < n_steps)
    def _():
        nxt = step + 1
        for cp in gather_copies(nxt // n_h, nxt % n_h, 1 - slot):
            cp.start()

    for cp in gather_copies(b, h, slot):
        cp.wait()

    q = q_ref[...]
    k_own = kn_ref[...]
    v_own = vn_ref[...]
    own_i = lax.broadcasted_iota(jnp.int32, (t_len, 1), 0)
    rows = []
    for t in range(t_len):
        q_t = q[t:t + 1, :]
        s_sel = jnp.sum(kg_ref[slot, t] * q_t, axis=1, keepdims=True) * scale
        s_o = jnp.sum(k_own * q_t, axis=1, keepdims=True) * scale
        s_o = jnp.where(own_i <= t, s_o, NEG_INF)
        m = jnp.maximum(jnp.max(s_sel, axis=0, keepdims=True), jnp.max(s_o, axis=0, keepdims=True))
        p_sel = jnp.exp(s_sel - m)
        p_o = jnp.exp(s_o - m)
        l = jnp.sum(p_sel, axis=0, keepdims=True) + jnp.sum(p_o, axis=0, keepdims=True)
        out = (jnp.sum(p_sel * vg_ref[slot, t], axis=0, keepdims=True)
               + jnp.sum(p_o * v_own, axis=0, keepdims=True))
        rows.append(out / l)
    o_ref[...] = jnp.concatenate(rows, axis=0)


def _moba_sample(q, k_new, v_new, cache_k3, cache_v3, sel_flat, page_table_flat, *, n_seq, n_pages):
    m, d = q.shape
    t = m // n_seq
    n_h = d // HEAD_DIM
    n_sel = MOBA_TOP_K * MOBA_BLOCK
    row_spec = pl.BlockSpec((t, HEAD_DIM), lambda b, h, sel, pt: (b, h))
    grid_spec = pltpu.PrefetchScalarGridSpec(
        num_scalar_prefetch=2,
        grid=(n_seq, n_h),
        in_specs=[row_spec, row_spec, row_spec, pl.BlockSpec(memory_space=pl.ANY), pl.BlockSpec(memory_space=pl.ANY)],
        out_specs=row_spec,
        scratch_shapes=[
            pltpu.VMEM((2, t, n_sel, HEAD_DIM), F32),
            pltpu.VMEM((2, t, n_sel, HEAD_DIM), F32),
            pltpu.SemaphoreType.DMA((2, 2)),
        ],
    )
    return pl.pallas_call(
        functools.partial(_sample_attn_kernel, n_pages=n_pages),
        grid_spec=grid_spec,
        out_shape=jax.ShapeDtypeStruct((m, d), F32),
        compiler_params=_params(2),
        name="moba_sample",
    )(sel_flat, page_table_flat, q, k_new, v_new, cache_k3, cache_v3)


def _rope_tables(pos):
    half = HEAD_DIM // 2
    inv = 1.0 / (ROPE_THETA ** (jnp.arange(half, dtype=F32) * (2.0 / HEAD_DIM)))
    ang = pos.astype(F32)[:, None] * inv[None, :]
    cos = jnp.cos(ang)
    sin = jnp.sin(ang)
    return jnp.concatenate([cos, cos], axis=-1), jnp.concatenate([-sin, sin], axis=-1)


def kernel(x_prompt, x_sample, state_pool, cache_k, cache_v, page_table, g_pool, w_pool, s_pool, g_ffn, w_gate_up,
           w_down, g_kv, w_kv, g_attn, w_q, w_o, g_final):
    n_b, seq, d = x_prompt.shape
    n_db, t_dec, _ = x_sample.shape
    depth = g_ffn.shape[0]
    n_pool = g_pool.shape[0]
    n_h = d // HEAD_DIM
    n_pages = page_table.shape[1]
    n_past_blk = PAST_LEN // MOBA_BLOCK
    m_p = n_b * seq
    m_s = n_db * t_dec
    state_rows = state_pool.shape[2]
    assert n_pages == n_past_blk * PAGES_PER_BLOCK, "own MoBA block must hold only the new tokens"
    assert seq % MOBA_BLOCK == 0 and state_rows == POOL_HALO - 1

    tm_pool, tm_ffn, tf, tm_proj = ROW_TILE, ROW_TILE, FF_CHUNK, ROW_TILE
    assert seq % ROW_TILE == 0 and w_down.shape[1] % FF_CHUNK == 0


    cos_p, sin_p = _rope_tables(jnp.arange(seq))
    cos_s, sin_s = _rope_tables(PAST_LEN + jnp.arange(t_dec))
    cos_s = jnp.tile(cos_s, (n_db, 1))
    sin_s = jnp.tile(sin_s, (n_db, 1))

    pt_flat = page_table.reshape(-1)

    hp = x_prompt
    hs = x_sample
    pool_p, pool_s, bm_parts = [], [], []
    assert n_pool > 0 and (n_db * n_pages) % (n_pool * PAGES_PER_BLOCK) == 0
    for layer in range(depth):
        last = layer == depth - 1
        if layer < n_pool:
            first = (w_gate_up, w_down, 0) if layer == 0 else None
            hp3, zt_p, *first_b = _pool_layer(hp.reshape(n_b, seq, d), hp.reshape(n_b, seq, d), g_pool[layer],
                                              w_pool[layer], s_pool[layer], tm=tm_pool, halo_normalized=False, pos0=0,
                                              round_weights=first)
            if first_b:
                w_gu_b, w_dn_b = first_b
            state = state_pool[layer]
            halo_s = jnp.pad(state, ((0, 0), (POOL_HALO - state_rows, 0), (0, 0)))
            hs3, zt_s = _pool_layer(hs.reshape(n_db, t_dec, d), halo_s, g_pool[layer], w_pool[layer], s_pool[layer],
                                    tm=t_dec, halo_normalized=True, pos0=PAST_LEN)
            pool_p.append(zt_p[:, POOL_HALO - state_rows:])
            pool_s.append(jnp.concatenate([state, zt_s], axis=1)[:, -state_rows:])
            hp = hp3.reshape(m_p, d)
            hs = hs3.reshape(m_s, d)
        else:
            a = layer - n_pool
            if a == 0:
                k_p, k_pb, kmean, k_s = _k_proj(hp, hs, g_kv, w_kv, cos_p, sin_p, cos_s, sin_s, tm=tm_proj)
                v_p, vt_pb, v_s = _v_proj(hp, hs, g_kv, w_kv, tm=tm_proj, seq=seq)
                kmean = kmean.reshape(n_b, seq // MOBA_BLOCK, d)
                bm = jnp.concatenate(bm_parts, axis=0).reshape(n_db, n_past_blk, n_h, HEAD_DIM)
            q_p, q_s = _q_proj(hp, hs, g_attn[a], w_q, a, cos_p, sin_p, cos_s, sin_s, tm=tm_proj)
            att_p = _moba_prompt(q_p, k_pb, vt_pb, kmean, batch=n_b, seq=seq)
            sel = _select_blocks(q_s, bm, n_seq=n_db)
            sel_flat = sel[:, :, :n_h * MOBA_TOP_K].reshape(-1)
            att_s = _moba_sample(q_s, k_s, v_s, cache_k, cache_v, sel_flat, pt_flat, n_seq=n_db, n_pages=n_pages)
            hp, hs = _o_proj(att_p, att_s, w_o, a, hp, hs, tm=tm_proj)
        pages_per_call = n_db * n_pages // n_pool
        cache_means = (cache_k, pt_flat, layer * pages_per_call, pages_per_call) if layer < n_pool else None
        next_weights = None if last else (w_gate_up, w_down, layer + 1)
        hp, hs, bm_part, next_b = _ffn(hp, hs, g_ffn[layer], w_gu_b, w_dn_b, g_final, tm=tm_ffn, tf=tf, final_norm=last,
                                       cache_means=cache_means, next_weights=next_weights)
        if bm_part is not None:
            bm_parts.append(bm_part)
        if next_b is not None:
            w_gu_b, w_dn_b = next_b

    y_prompt = hp.reshape(n_b, seq, d)
    y_sample = hs.reshape(n_db, t_dec, d)
    new_pool_prompt = jnp.stack(pool_p)
    new_pool_sample = jnp.stack(pool_s)
    shape_p = (n_b, seq, n_h, HEAD_DIM)
    shape_s = (n_db, t_dec, n_h, HEAD_DIM)
    return (y_prompt, y_sample, new_pool_prompt, new_pool_sample, k_p.reshape(shape_p), v_p.reshape(shape_p),
            k_s.reshape(shape_s), v_s.reshape(shape_s))
```

```python
import functools

import jax
import jax.numpy as jnp
from jax import lax
from jax.experimental import pallas as pl
from jax.experimental.pallas import tpu as pltpu

F32 = jnp.float32
BF16 = jnp.bfloat16

POOL_WINDOWS = (2, 4, 8, 16)
POOL_HALO = 16
HEAD_DIM = 128
MOBA_BLOCK = 256
MOBA_TOP_K = 3
PAGE_SIZE = 128
PAGES_PER_BLOCK = MOBA_BLOCK // PAGE_SIZE
PAST_LEN = 16384
ROPE_THETA = 10000.0
RMS_EPS = 1e-6
NEG_INF = -1e30
GATE_FLOOR = -3.0e38

VMEM_LIMIT_BYTES = 56 * 1024 * 1024
ROW_TILE = 512
FF_CHUNK = 512


def _params(n_axes):
    return pltpu.CompilerParams(dimension_semantics=("arbitrary",) * n_axes, vmem_limit_bytes=VMEM_LIMIT_BYTES)


def _rmsnorm(x, g):
    ms = jnp.mean(x * x, axis=-1, keepdims=True)
    return x * lax.rsqrt(ms + RMS_EPS) * g


def _pool_kernel(*refs, tm, halo_normalized, pos0, round_weights):
    if round_weights:
        x_ref, halo_ref, g_ref, w_ref, s_ref, gu_in, dn_in, h_ref, zt_ref, gu_out, dn_out, zs_ref = refs
        gu_out[...] = gu_in[...].astype(BF16)
        dn_out[...] = dn_in[...].astype(BF16)
    else:
        x_ref, halo_ref, g_ref, w_ref, s_ref, h_ref, zt_ref, zs_ref = refs
    i = pl.program_id(1)
    d = x_ref.shape[-1]
    c = d // len(POOL_WINDOWS)
    x = x_ref[...]
    g = g_ref[...]
    z = _rmsnorm(x, g)
    if halo_normalized:
        zh = halo_ref[...]
    else:
        zh = _rmsnorm(halo_ref[...], g) * (i > 0).astype(F32)
    zs_ref[0:POOL_HALO, :] = zh
    zs_ref[POOL_HALO:POOL_HALO + tm, :] = z
    pos = lax.broadcasted_iota(jnp.int32, (tm, c), 0) + (i * tm + pos0)
    for gi, w in enumerate(POOL_WINDOWS):
        cols = slice(gi * c, (gi + 1) * c)
        zc = z[:, cols]
        if w <= 8:
            tot = zc
            for j in range(1, w):
                tot = tot + zs_ref[POOL_HALO - j:POOL_HALO - j + tm, cols]
        else:
            assert w % 8 == 0 and w <= POOL_HALO
            lead = w - 8
            s8 = zs_ref[POOL_HALO - lead:POOL_HALO + tm, cols]
            for j in range(1, 8):
                s8 = s8 + zs_ref[POOL_HALO - lead - j:POOL_HALO + tm - j, cols]
            tot = s8[lead:, :]
            for k in range(1, w // 8):
                tot = tot + s8[lead - 8 * k:lead - 8 * k + tm, :]
        cnt = jnp.minimum(pos + 1, w).astype(F32)
        pooled = tot / cnt - zc
        mixed = jnp.dot(pooled.astype(BF16), w_ref[gi].astype(BF16), preferred_element_type=F32)
        h_ref[:, cols] = x[:, cols] + mixed * s_ref[:, cols]

    nt = zt_ref.shape[0]

    @pl.when(i == pl.num_programs(1) - 1)
    def _():
        zt_ref[...] = z[tm - nt:, :]


def _pool_layer(x3, halo3, g, w, s, *, tm, halo_normalized, pos0, round_weights=None):
    nb, length, d = x3.shape
    nt = min(POOL_HALO, tm)
    n_i = length // tm
    hb = tm // POOL_HALO
    if halo_normalized:
        halo_map = lambda b, i: (b, 0, 0)
    else:
        halo_map = lambda b, i: (b, jnp.maximum(i * hb - 1, 0), 0)
    in_specs = [
        pl.BlockSpec((None, tm, d), lambda b, i: (b, i, 0)),
        pl.BlockSpec((None, POOL_HALO, d), halo_map),
        pl.BlockSpec((1, d), lambda b, i: (0, 0)),
        pl.BlockSpec(w.shape, lambda b, i: (0, 0, 0), pipeline_mode=pl.Buffered(1)),
        pl.BlockSpec((1, d), lambda b, i: (0, 0)),
    ]
    out_specs = [
        pl.BlockSpec((None, tm, d), lambda b, i: (b, i, 0)),
        pl.BlockSpec((None, nt, d), lambda b, i: (b, 0, 0)),
    ]
    out_shape = [jax.ShapeDtypeStruct((nb, length, d), F32), jax.ShapeDtypeStruct((nb, nt, d), F32)]
    operands = [x3, halo3, g.reshape(1, d), w, s.reshape(1, d)]
    if round_weights is not None:
        w_gu_all, w_dn_all, layer = round_weights
        n_steps = nb * n_i
        two_f, f = w_gu_all.shape[2], w_dn_all.shape[1]
        gu_rows, dn_rows = d // n_steps, f // n_steps
        assert gu_rows * n_steps == d and dn_rows * n_steps == f
        in_specs += [pl.BlockSpec((None, gu_rows, two_f), lambda b, i: (layer, b * n_i + i, 0)),
                     pl.BlockSpec((None, dn_rows, d), lambda b, i: (layer, b * n_i + i, 0))]
        out_specs += [pl.BlockSpec((gu_rows, two_f), lambda b, i: (b * n_i + i, 0)),
                      pl.BlockSpec((dn_rows, d), lambda b, i: (b * n_i + i, 0))]
        out_shape += [jax.ShapeDtypeStruct((d, two_f), BF16), jax.ShapeDtypeStruct((f, d), BF16)]
        operands += [w_gu_all, w_dn_all]
    kern = functools.partial(_pool_kernel, tm=tm, halo_normalized=halo_normalized, pos0=pos0,
                             round_weights=round_weights is not None)
    return pl.pallas_call(
        kern,
        grid=(nb, n_i),
        in_specs=in_specs,
        out_specs=out_specs,
        out_shape=out_shape,
        scratch_shapes=[pltpu.VMEM((POOL_HALO + tm, d), F32)],
        compiler_params=_params(2),
        name="pool_layer",
    )(*operands)


def _stream_block_means(pt_ref, cache_hbm, bm_ref, page_buf, page_sem, *, first_page, n_call_pages):
    step = pl.program_id(0) * pl.num_programs(1) + pl.program_id(1)
    pages_per_step = page_buf.shape[1]
    blocks_per_step = pages_per_step // PAGES_PER_BLOCK
    n_fetch = n_call_pages // pages_per_step

    def page_copy(at_step, p, slot):
        phys = pt_ref[first_page + at_step * pages_per_step + p]
        return pltpu.make_async_copy(cache_hbm.at[phys], page_buf.at[slot, p], page_sem.at[slot])

    def start_step(at_step, slot):
        for p in range(pages_per_step):
            page_copy(at_step, p, slot).start(priority=1)

    @pl.when(step == 0)
    def _():
        start_step(0, 0)

    @pl.when(step + 1 < n_fetch)
    def _():
        start_step(step + 1, (step + 1) % 2)

    @pl.when(step < n_fetch)
    def _():
        for p in range(pages_per_step):
            page_copy(step, p, step % 2).wait()

    def add_pages():
        src = jnp.minimum(step, n_fetch - 1)
        slot = src % 2
        for b in range(blocks_per_step):
            rows = sum(jnp.sum(page_buf[slot, PAGES_PER_BLOCK * b + pg], axis=0) for pg in range(PAGES_PER_BLOCK))
            bm_ref[src * blocks_per_step + b] = rows * (1.0 / (PAGES_PER_BLOCK * PAGE_SIZE))

    return add_pages


def _ffn_kernel(*refs, final_norm, means, round_next):
    refs = list(refs)
    take = lambda n: [refs.pop(0) for _ in range(n)]
    pt_ref, = take(1) if means is not None else (None,)
    x_ref, xs_ref, g_ref, wg_ref, wu_ref, wd_ref, go_ref = take(7)
    cache_hbm, = take(1) if means is not None else (None,)
    next_in = take(2) if round_next else None
    o_ref, os_ref = take(2)
    bm_ref, = take(1) if means is not None else (None,)
    next_out = take(2) if round_next else None
    xn_ref, xns_ref = take(2)
    side_work = None
    if means is not None:
        page_buf, page_sem = take(2)
        side_work = _stream_block_means(pt_ref, cache_hbm, bm_ref, page_buf, page_sem, **means)
    j = pl.program_id(1)

    def rows(x_ref, o_ref, xn_ref, with_side_jobs):
        @pl.when(j == 0)
        def _():
            x = x_ref[...]
            xn_ref[...] = _rmsnorm(x, g_ref[...]).astype(BF16)
            o_ref[...] = x

        xn = xn_ref[...]
        gate = jnp.dot(xn, wg_ref[...], preferred_element_type=F32)
        up = jnp.dot(xn, wu_ref[...], preferred_element_type=F32)
        act = (gate * jax.nn.sigmoid(gate) * up).astype(BF16)
        o_ref[...] += jnp.dot(act, wd_ref[...], preferred_element_type=F32)
        if with_side_jobs and round_next:
            for src, dst in zip(next_in, next_out):
                dst[...] = src[...].astype(BF16)
        if with_side_jobs and side_work is not None:
            side_work()

        if final_norm:
            @pl.when(j == pl.num_programs(1) - 1)
            def _():
                o_ref[...] = _rmsnorm(o_ref[...], go_ref[...])

    rows(x_ref, o_ref, xn_ref, True)

    @pl.when(pl.program_id(0) == pl.num_programs(0) - 1)
    def _():
        rows(xs_ref, os_ref, xns_ref, False)


def _ffn(x, xs, g, w_gu_b, w_dn_b, g_out, *, tm, tf, final_norm, cache_means=None, next_weights=None):
    m, d = x.shape
    m_s = xs.shape[0]
    f = w_dn_b.shape[0]
    n_f = f // tf
    grid = (m // tm, n_f)
    in_specs = [
        pl.BlockSpec((tm, d), lambda i, j, *_: (i, 0)),
        pl.BlockSpec((m_s, d), lambda i, j, *_: (0, 0)),
        pl.BlockSpec((1, d), lambda i, j, *_: (0, 0)),
        pl.BlockSpec((d, tf), lambda i, j, *_: (0, j)),
        pl.BlockSpec((d, tf), lambda i, j, *_: (0, j + n_f)),
        pl.BlockSpec((tf, d), lambda i, j, *_: (j, 0)),
        pl.BlockSpec((1, d), lambda i, j, *_: (0, 0)),
    ]
    out_specs = [pl.BlockSpec((tm, d), lambda i, j, *_: (i, 0)), pl.BlockSpec((m_s, d), lambda i, j, *_: (0, 0))]
    out_shape = [jax.ShapeDtypeStruct((m, d), F32), jax.ShapeDtypeStruct((m_s, d), F32)]
    scratch = [pltpu.VMEM((tm, d), BF16), pltpu.VMEM((m_s, d), BF16)]
    operands = [x, xs, g.reshape(1, d), w_gu_b, w_gu_b, w_dn_b, g_out.reshape(1, d)]
    next_specs, next_shapes, next_operands = [], [], []
    if next_weights is not None:
        w_gu_all, w_dn_all, nxt = next_weights
        gu_chunk = (d // grid[0], 2 * f // grid[1])
        dn_rows = f // (grid[0] * grid[1])
        assert gu_chunk[0] * grid[0] == d and gu_chunk[1] * grid[1] == 2 * f and dn_rows * grid[0] * grid[1] == f
        next_operands = [w_gu_all, w_dn_all]
        next_specs = [pl.BlockSpec((None,) + gu_chunk, lambda i, j, *_: (nxt, i, j)),
                      pl.BlockSpec((None, dn_rows, d), lambda i, j, *_: (nxt, i * n_f + j, 0))]
        next_out_specs = [pl.BlockSpec(gu_chunk, lambda i, j, *_: (i, j)),
                          pl.BlockSpec((dn_rows, d), lambda i, j, *_: (i * n_f + j, 0))]
        next_shapes = [jax.ShapeDtypeStruct((d, 2 * f), BF16), jax.ShapeDtypeStruct((f, d), BF16)]
    if cache_means is None:
        means = None
        prefetch = []
    else:
        cache_k, page_table_flat, first_page, n_call_pages = cache_means
        _, page, n_h, dh = cache_k.shape
        blocks_per_step = -(-n_call_pages // (PAGES_PER_BLOCK * grid[0] * grid[1]))
        while n_call_pages % (PAGES_PER_BLOCK * blocks_per_step):
            blocks_per_step += 1
        pages_per_step = PAGES_PER_BLOCK * blocks_per_step
        means = dict(first_page=first_page, n_call_pages=n_call_pages)
        prefetch = [page_table_flat]
        in_specs.append(pl.BlockSpec(memory_space=pl.ANY))
        operands.append(cache_k)
        n_call_blk = n_call_pages // PAGES_PER_BLOCK
        out_specs.append(pl.BlockSpec((n_call_blk, n_h, dh), lambda i, j, *_: (0, 0, 0)))
        out_shape.append(jax.ShapeDtypeStruct((n_call_blk, n_h, dh), F32))
        scratch += [pltpu.VMEM((2, pages_per_step, page, n_h, dh), F32), pltpu.SemaphoreType.DMA((2,))]
    if next_weights is not None:
        in_specs += next_specs
        operands += next_operands
        out_specs += next_out_specs
        out_shape += next_shapes
    grid_spec = pltpu.PrefetchScalarGridSpec(
        num_scalar_prefetch=len(prefetch), grid=grid, in_specs=in_specs, out_specs=out_specs, scratch_shapes=scratch)
    outs = list(pl.pallas_call(
        functools.partial(_ffn_kernel, final_norm=final_norm, means=means, round_next=next_weights is not None),
        grid_spec=grid_spec,
        out_shape=out_shape,
        compiler_params=_params(2),
        name="swiglu",
    )(*prefetch, *operands))
    out, out_small = outs[:2]
    del outs[:2]
    block_means = outs.pop(0) if cache_means is not None else None
    next_b = tuple(outs) if next_weights is not None else None
    return out, out_small, block_means, next_b


def _rope(xh, cos, sin_signed):
    return xh * cos + pltpu.roll(xh, HEAD_DIM // 2, axis=1) * sin_signed


def _round_weight_once(w_ref, wb_ref):
    @pl.when(pl.program_id(0) == 0)
    def _():
        wb_ref[...] = w_ref[...].astype(BF16)


def _prompt_then_sample(n_tiles, prompt_fn, sample_fn):
    i = pl.program_id(0)
    pl.when(i < n_tiles)(prompt_fn)
    pl.when(i == n_tiles)(sample_fn)


def _proj_specs(tm, m_s, n_tiles):
    tile = lambda i: jnp.minimum(i, n_tiles - 1)
    prompt = lambda width: pl.BlockSpec((tm, width), lambda i: (tile(i), 0))
    sample = lambda width: pl.BlockSpec((m_s, width), lambda i: (0, 0))
    return tile, prompt, sample


def _q_kernel(xp_ref, xs_ref, g_ref, w_ref, cosp_ref, sinp_ref, coss_ref, sins_ref, qp_ref, qs_ref, wb_ref, *, n_tiles):
    _round_weight_once(w_ref, wb_ref)

    def rows(x_ref, cos_ref, sin_ref, q_ref):
        xn = _rmsnorm(x_ref[...], g_ref[...]).astype(BF16)
        acc = jnp.dot(xn, wb_ref[...], preferred_element_type=F32)
        cos = cos_ref[...]
        sin = sin_ref[...]
        for h in range(acc.shape[1] // HEAD_DIM):
            cols = slice(h * HEAD_DIM, (h + 1) * HEAD_DIM)
            q_ref[:, cols] = _rope(acc[:, cols], cos, sin)

    _prompt_then_sample(n_tiles, lambda: rows(xp_ref, cosp_ref, sinp_ref, qp_ref),
                        lambda: rows(xs_ref, coss_ref, sins_ref, qs_ref))


def _q_proj(xp, xs, g, w, layer, cos_p, sin_p, cos_s, sin_s, *, tm):
    m, d = xp.shape
    m_s = xs.shape[0]
    n_tiles = m // tm
    n_pos = cos_p.shape[0] // tm
    tile, prompt, sample = _proj_specs(tm, m_s, n_tiles)
    rope_p = pl.BlockSpec((tm, HEAD_DIM), lambda i: (tile(i) % n_pos, 0))
    return pl.pallas_call(
        functools.partial(_q_kernel, n_tiles=n_tiles),
        grid=(n_tiles + 1,),
        in_specs=[
            prompt(d), sample(d),
            pl.BlockSpec((1, d), lambda i: (0, 0)),
            pl.BlockSpec((None, d, d), lambda i: (layer, 0, 0), pipeline_mode=pl.Buffered(1)),
            rope_p, rope_p, sample(HEAD_DIM), sample(HEAD_DIM),
        ],
        out_specs=[prompt(d), sample(d)],
        out_shape=[jax.ShapeDtypeStruct((m, d), F32), jax.ShapeDtypeStruct((m_s, d), F32)],
        scratch_shapes=[pltpu.VMEM((d, d), BF16)],
        compiler_params=_params(1),
        name="q_proj",
    )(xp, xs, g.reshape(1, d), w, cos_p, sin_p, cos_s, sin_s)


def _k_kernel(xp_ref, xs_ref, g_ref, w_ref, cosp_ref, sinp_ref, coss_ref, sins_ref, kp_ref, kb_ref, km_ref, ks_ref,
              wb_ref, *, n_tiles):
    _round_weight_once(w_ref, wb_ref)

    def rows(x_ref, cos_ref, sin_ref, k_ref):
        xn = _rmsnorm(x_ref[...], g_ref[...]).astype(BF16)
        acc = jnp.dot(xn, wb_ref[...], preferred_element_type=F32)
        cos = cos_ref[...]
        sin = sin_ref[...]
        for h in range(acc.shape[1] // HEAD_DIM):
            cols = slice(h * HEAD_DIM, (h + 1) * HEAD_DIM)
            k_ref[:, cols] = _rope(acc[:, cols], cos, sin)

    def prompt_rows():
        rows(xp_ref, cosp_ref, sinp_ref, kp_ref)
        k = kp_ref[...]
        kb_ref[...] = k.astype(BF16)
        for r in range(k.shape[0] // MOBA_BLOCK):
            km_ref[r] = jnp.mean(k[r * MOBA_BLOCK:(r + 1) * MOBA_BLOCK, :], axis=0, keepdims=True)

    _prompt_then_sample(n_tiles, prompt_rows, lambda: rows(xs_ref, coss_ref, sins_ref, ks_ref))


def _k_proj(xp, xs, g, w_kv, cos_p, sin_p, cos_s, sin_s, *, tm):
    m, d = xp.shape
    m_s = xs.shape[0]
    n_tiles = m // tm
    n_pos = cos_p.shape[0] // tm
    r = tm // MOBA_BLOCK
    tile, prompt, sample = _proj_specs(tm, m_s, n_tiles)
    rope_p = pl.BlockSpec((tm, HEAD_DIM), lambda i: (tile(i) % n_pos, 0))
    return pl.pallas_call(
        functools.partial(_k_kernel, n_tiles=n_tiles),
        grid=(n_tiles + 1,),
        in_specs=[
            prompt(d), sample(d),
            pl.BlockSpec((1, d), lambda i: (0, 0)),
            pl.BlockSpec((d, d), lambda i: (0, 0), pipeline_mode=pl.Buffered(1)),
            rope_p, rope_p, sample(HEAD_DIM), sample(HEAD_DIM),
        ],
        out_specs=[prompt(d), prompt(d), pl.BlockSpec((r, 1, d), lambda i: (tile(i), 0, 0)), sample(d)],
        out_shape=[jax.ShapeDtypeStruct((m, d), F32), jax.ShapeDtypeStruct((m, d), BF16),
                   jax.ShapeDtypeStruct((m // MOBA_BLOCK, 1, d), F32), jax.ShapeDtypeStruct((m_s, d), F32)],
        scratch_shapes=[pltpu.VMEM((d, d), BF16)],
        compiler_params=_params(1),
        name="k_proj",
    )(xp, xs, g.reshape(1, d), w_kv, cos_p, sin_p, cos_s, sin_s)


def _v_kernel(xp_ref, xs_ref, g_ref, w_ref, vp_ref, vt_ref, vs_ref, wb_ref, *, n_tiles):
    _round_weight_once(w_ref, wb_ref)

    def values(x_ref):
        xn = _rmsnorm(x_ref[...], g_ref[...]).astype(BF16)
        return jnp.dot(xn, wb_ref[...], preferred_element_type=F32)

    def prompt_rows():
        v = values(xp_ref)
        vp_ref[...] = v
        for h in range(v.shape[1] // HEAD_DIM):
            for r in range(v.shape[0] // MOBA_BLOCK):
                blk = v[r * MOBA_BLOCK:(r + 1) * MOBA_BLOCK, h * HEAD_DIM:(h + 1) * HEAD_DIM]
                vt_ref[h, r] = blk.T.astype(BF16)

    def sample_rows():
        vs_ref[...] = values(xs_ref)

    _prompt_then_sample(n_tiles, prompt_rows, sample_rows)


def _v_proj(xp, xs, g, w_kv, *, tm, seq):
    m, d = xp.shape
    m_s = xs.shape[0]
    n_h = d // HEAD_DIM
    n_tiles = m // tm
    r = tm // MOBA_BLOCK
    tiles_per_seq = seq // tm
    tile, prompt, sample = _proj_specs(tm, m_s, n_tiles)
    vt_spec = pl.BlockSpec((None, n_h, r, HEAD_DIM, MOBA_BLOCK),
                           lambda i: (tile(i) // tiles_per_seq, 0, tile(i) % tiles_per_seq, 0, 0))
    return pl.pallas_call(
        functools.partial(_v_kernel, n_tiles=n_tiles),
        grid=(n_tiles + 1,),
        in_specs=[
            prompt(d), sample(d),
            pl.BlockSpec((1, d), lambda i: (0, 0)),
            pl.BlockSpec((d, d), lambda i: (0, 1), pipeline_mode=pl.Buffered(1)),
        ],
        out_specs=[prompt(d), vt_spec, sample(d)],
        out_shape=[jax.ShapeDtypeStruct((m, d), F32),
                   jax.ShapeDtypeStruct((m // seq, n_h, seq // MOBA_BLOCK, HEAD_DIM, MOBA_BLOCK), BF16),
                   jax.ShapeDtypeStruct((m_s, d), F32)],
        scratch_shapes=[pltpu.VMEM((d, d), BF16)],
        compiler_params=_params(1),
        name="v_proj",
    )(xp, xs, g.reshape(1, d), w_kv)


def _o_kernel(ap_ref, as_ref, w_ref, hp_ref, hs_ref, op_ref, os_ref, wb_ref, *, n_tiles):
    _round_weight_once(w_ref, wb_ref)

    def rows(a_ref, h_ref, o_ref):
        o_ref[...] = h_ref[...] + jnp.dot(a_ref[...].astype(BF16), wb_ref[...], preferred_element_type=F32)

    _prompt_then_sample(n_tiles, lambda: rows(ap_ref, hp_ref, op_ref), lambda: rows(as_ref, hs_ref, os_ref))


def _o_proj(ap, a_s, w, layer, hp, hs, *, tm):
    m, d = hp.shape
    m_s = hs.shape[0]
    n_tiles = m // tm
    _, prompt, sample = _proj_specs(tm, m_s, n_tiles)
    return pl.pallas_call(
        functools.partial(_o_kernel, n_tiles=n_tiles),
        grid=(n_tiles + 1,),
        in_specs=[
            prompt(d), sample(d),
            pl.BlockSpec((None, d, d), lambda i: (layer, 0, 0), pipeline_mode=pl.Buffered(1)),
            prompt(d), sample(d),
        ],
        out_specs=[prompt(d), sample(d)],
        out_shape=[jax.ShapeDtypeStruct((m, d), F32), jax.ShapeDtypeStruct((m_s, d), F32)],
        scratch_shapes=[pltpu.VMEM((d, d), BF16)],
        compiler_params=_params(1),
        name="o_proj",
    )(ap, a_s, w, hp, hs)


def _top_k_picks(gate, candidate, blk_f, axis):
    remaining = candidate
    for _ in range(MOBA_TOP_K):
        gm = jnp.where(remaining, gate, GATE_FLOOR)
        top = jnp.max(gm, axis=axis, keepdims=True)
        hit = jnp.logical_and(remaining, gm == top)
        idx = jnp.min(jnp.where(hit, blk_f, 1e9), axis=axis, keepdims=True)
        pick = blk_f == idx
        remaining = jnp.logical_and(remaining, jnp.logical_not(pick))
        yield pick, idx


def _attn_kernel(q_ref, k_ref, vt_ref, km_ref, o_ref, qt_ref, acc_ref, sa_ref, sb_ref):
    n_g, n_blk = qt_ref.shape[0], qt_ref.shape[1]
    tq = MOBA_BLOCK
    q_scale = HEAD_DIM ** -0.5 * 1.4426950408889634
    blk_i = lax.broadcasted_iota(jnp.int32, (n_blk, tq), 0)
    blk_f = blk_i.astype(F32)
    heads = [slice(g * HEAD_DIM, (g + 1) * HEAD_DIM) for g in range(n_g)]
    assert n_blk <= qt_ref.shape[2] - HEAD_DIM

    pad = jnp.zeros((qt_ref.shape[2] - HEAD_DIM - n_blk, tq), BF16)
    for g in range(n_g):
        km = km_ref[:, heads[g]]
        for i in range(n_blk):
            q_t = q_ref[i * tq:(i + 1) * tq, heads[g]].T
            qt_ref[g, i, 0:HEAD_DIM, :] = (q_t * q_scale).astype(BF16)
            if i <= MOBA_TOP_K:
                selected = blk_i < i
            else:
                gate_t = jnp.dot(km, q_t, preferred_element_type=F32, precision=lax.Precision.HIGHEST)
                selected = jnp.zeros((n_blk, tq), jnp.bool_)
                for pick, _ in _top_k_picks(gate_t, blk_i < i, blk_f, 0):
                    selected = jnp.logical_or(selected, pick)
            qt_ref[g, i, HEAD_DIM:HEAD_DIM + n_blk, :] = jnp.where(selected, 0.0, NEG_INF).astype(BF16)
            qt_ref[g, i, HEAD_DIM + n_blk:, :] = pad

    kpos = lax.broadcasted_iota(jnp.int32, (tq, tq), 0)
    qpos = lax.broadcasted_iota(jnp.int32, (tq, tq), 1)
    causal = kpos <= qpos

    last_pair = n_blk // 2 - 1
    pair_row = lax.broadcasted_iota(jnp.int32, (2 * tq, HEAD_DIM), 0)
    pair_lane = lax.broadcasted_iota(jnp.int32, (2 * tq, HEAD_DIM), 1)
    lane_minus_half = pair_lane - jnp.where(pair_row >= tq, 1, 0)
    sum_rows = acc_ref.shape[1] - HEAD_DIM
    ones_rows = jnp.where(lax.broadcasted_iota(jnp.int32, (sum_rows, tq), 0) == 0, 1.0, 0.0).astype(BF16)

    def values_aug(g, j):
        return jnp.concatenate([vt_ref[g, j], ones_rows], axis=0)

    def tile(i, _):
        start = pl.multiple_of(i * tq, tq)

        def score_pair(pair_idx, dst_ref):
            pair_c = jnp.minimum(pair_idx, last_pair)
            st = pl.multiple_of(pair_c * (2 * tq), 2 * tq)
            onehot = jnp.where(lane_minus_half == 2 * pair_c, 1.0, 0.0).astype(BF16)
            for g in range(n_g):
                keys_aug = jnp.concatenate([k_ref[pl.ds(st, 2 * tq), heads[g]], onehot], axis=1)
                dst_ref[g] = jnp.dot(keys_aug, qt_ref[g, i], preferred_element_type=F32)

        def fold_pair(src_ref, pair_idx, ms):
            j0 = 2 * jnp.minimum(pair_idx, last_pair)
            new_ms, alphas, ps = [], [], []
            for g in range(n_g):
                sa = src_ref[g, 0:tq, :]
                sb = src_ref[g, tq:2 * tq, :]
                m_blk = jnp.maximum(jnp.max(sa, axis=0, keepdims=True), jnp.max(sb, axis=0, keepdims=True))
                m_new = jnp.maximum(ms[g], m_blk)
                new_ms.append(m_new)
                alphas.append(jnp.exp2(ms[g] - m_new))
                ps.append((jnp.exp2(sa - m_new).astype(BF16), jnp.exp2(sb - m_new).astype(BF16)))
            for g in range(n_g):
                pv = (jnp.dot(values_aug(g, j0), ps[g][0], preferred_element_type=F32)
                      + jnp.dot(values_aug(g, j0 + 1), ps[g][1], preferred_element_type=F32))
                acc_ref[g] = alphas[g] * acc_ref[g] + pv
            return tuple(new_ms)

        ms = []
        s_own = [jnp.dot(k_ref[pl.ds(start, tq), heads[g]], qt_ref[g, i, 0:HEAD_DIM, :], preferred_element_type=F32)
                 for g in range(n_g)]
        score_pair(0, sa_ref)
        p_own = []
        for g in range(n_g):
            s = jnp.where(causal, s_own[g], NEG_INF)
            m0 = jnp.max(s, axis=0, keepdims=True)
            ms.append(m0)
            p_own.append(jnp.exp2(s - m0).astype(BF16))
        for g in range(n_g):
            acc_ref[g] = jnp.dot(values_aug(g, i), p_own[g], preferred_element_type=F32)

        def two_pairs(t, ms):
            score_pair(2 * t + 1, sb_ref)
            ms = fold_pair(sa_ref, 2 * t, ms)
            score_pair(2 * t + 2, sa_ref)
            return fold_pair(sb_ref, 2 * t + 1, ms)

        n_pairs = (i + 1) // 2
        ms = lax.fori_loop(0, n_pairs // 2, two_pairs, tuple(ms))

        @pl.when(n_pairs % 2 == 1)
        def _():
            fold_pair(sa_ref, n_pairs - 1, ms)

        for g in range(n_g):
            inv_l = 1.0 / acc_ref[g, HEAD_DIM:HEAD_DIM + 1, :]
            o_ref[pl.ds(start, tq), heads[g]] = (acc_ref[g, 0:HEAD_DIM, :] * inv_l).T.astype(o_ref.dtype)
        return 0

    lax.fori_loop(0, n_blk, tile, 0)


ATTN_HEADS_PER_STEP = 4


ATTN_SUM_ROWS = 16


def _moba_prompt(q, k_b, vt_b, kmean, *, batch, seq):
    m, d = q.shape
    n_g = ATTN_HEADS_PER_STEP
    n_blk = seq // MOBA_BLOCK
    width = n_g * HEAD_DIM
    return pl.pallas_call(
        _attn_kernel,
        grid=(batch, d // width),
        in_specs=[
            pl.BlockSpec((seq, width), lambda b, h: (b, h)),
            pl.BlockSpec((seq, width), lambda b, h: (b, h)),
            pl.BlockSpec((None, n_g, n_blk, HEAD_DIM, MOBA_BLOCK), lambda b, h: (b, h, 0, 0, 0)),
            pl.BlockSpec((None, n_blk, width), lambda b, h: (b, 0, h)),
        ],
        out_specs=pl.BlockSpec((seq, width), lambda b, h: (b, h), pipeline_mode=pl.Buffered(1)),
        out_shape=jax.ShapeDtypeStruct((m, d), BF16),
        scratch_shapes=[
            pltpu.VMEM((n_g, n_blk, 2 * HEAD_DIM, MOBA_BLOCK), BF16),
            pltpu.VMEM((n_g, HEAD_DIM + ATTN_SUM_ROWS, MOBA_BLOCK), F32),
            pltpu.VMEM((n_g, 2 * MOBA_BLOCK, MOBA_BLOCK), F32),
            pltpu.VMEM((n_g, 2 * MOBA_BLOCK, MOBA_BLOCK), F32),
        ],
        compiler_params=_params(2),
        name="moba_prompt",
    )(q, k_b, vt_b, kmean)


SELECT_LANES = 128


def _select_kernel(q_ref, bm_ref, sel_ref):
    q = q_ref[...]
    t = q.shape[0]
    n_valid = bm_ref.shape[0]
    lane = lax.broadcasted_iota(jnp.int32, (t, SELECT_LANES), 1)
    lane_f = lane.astype(F32)
    out = jnp.zeros((t, SELECT_LANES), jnp.int32)
    no_block = jnp.zeros((SELECT_LANES - n_valid, HEAD_DIM), F32)
    for h in range(q.shape[1] // HEAD_DIM):
        cols = slice(h * HEAD_DIM, (h + 1) * HEAD_DIM)
        means_h = jnp.concatenate([bm_ref[:, h, :], no_block], axis=0)
        gate = lax.dot_general(q[:, cols], means_h, (((1,), (1,)), ((), ())),
                               preferred_element_type=F32, precision=lax.Precision.HIGHEST)
        for r, (_, idx) in enumerate(_top_k_picks(gate, lane < n_valid, lane_f, 1)):
            out = jnp.where(lane == h * MOBA_TOP_K + r, idx.astype(jnp.int32), out)
    sel_ref[...] = out


def _select_blocks(q, bm, *, n_seq):
    m, d = q.shape
    t = m // n_seq
    _, n_blk, n_h, dh = bm.shape
    assert n_blk <= SELECT_LANES and n_h * MOBA_TOP_K <= SELECT_LANES
    return pl.pallas_call(
        _select_kernel,
        grid=(n_seq,),
        in_specs=[
            pl.BlockSpec((t, d), lambda b: (b, 0)),
            pl.BlockSpec((None, n_blk, n_h, dh), lambda b: (b, 0, 0, 0)),
        ],
        out_specs=pl.BlockSpec((None, t, SELECT_LANES), lambda b: (b, 0, 0)),
        out_shape=jax.ShapeDtypeStruct((n_seq, t, SELECT_LANES), jnp.int32),
        compiler_params=_params(1),
        name="select_blocks",
    )(q, bm)


def _sample_attn_kernel(sel_ref, pt_ref, q_ref, kn_ref, vn_ref, ck_hbm, cv_hbm, o_ref, kg_ref, vg_ref, sem,
                        *, n_pages):
    b = pl.program_id(0)
    h = pl.program_id(1)
    n_h = pl.num_programs(1)
    n_steps = pl.num_programs(0) * n_h
    step = b * n_h + h
    slot = step % 2
    t_len = q_ref.shape[0]
    n_sel = MOBA_TOP_K * MOBA_BLOCK
    scale = HEAD_DIM ** -0.5

    def gather_copies(bb, hh, sl):
        copies = []
        for t in range(t_len):
            for r in range(MOBA_TOP_K):
                blk = sel_ref[((bb * t_len + t) * n_h + hh) * MOBA_TOP_K + r]
                for pg in range(PAGES_PER_BLOCK):
                    phys = pt_ref[bb * n_pages + blk * PAGES_PER_BLOCK + pg]
                    rows = pl.ds((r * PAGES_PER_BLOCK + pg) * PAGE_SIZE, PAGE_SIZE)
                    copies.append(pltpu.make_async_copy(ck_hbm.at[phys, :, hh, :], kg_ref.at[sl, t, rows, :], sem.at[sl, 0]))
                    copies.append(pltpu.make_async_copy(cv_hbm.at[phys, :, hh, :], vg_ref.at[sl, t, rows, :], sem.at[sl, 1]))
        return copies

    def start_all(copies):
        for n, cp in enumerate(copies):
            cp.start(priority=n % 2)

    @pl.when(step == 0)
    def _():
        start_all(gather_copies(b, h, slot))

    @pl.when(step + 1 < n_steps)
    def _():
        nxt = step + 1
        start_all(gather_copies(nxt // n_h, nxt % n_h, 1 - slot))

    for cp in gather_copies(b, h, slot):
        cp.wait()

    q = q_ref[...]
    k_own = kn_ref[...]
    v_own = vn_ref[...]
    own_i = lax.broadcasted_iota(jnp.int32, (t_len, 1), 0)
    rows = []
    for t in range(t_len):
        q_t = q[t:t + 1, :]
        s_sel = jnp.sum(kg_ref[slot, t] * q_t, axis=1, keepdims=True) * scale
        s_o = jnp.sum(k_own * q_t, axis=1, keepdims=True) * scale
        s_o = jnp.where(own_i <= t, s_o, NEG_INF)
        m = jnp.maximum(jnp.max(s_sel, axis=0, keepdims=True), jnp.max(s_o, axis=0, keepdims=True))
        p_sel = jnp.exp(s_sel - m)
        p_o = jnp.exp(s_o - m)
        l = jnp.sum(p_sel, axis=0, keepdims=True) + jnp.sum(p_o, axis=0, keepdims=True)
        out = (jnp.sum(p_sel * vg_ref[slot, t], axis=0, keepdims=True)
               + jnp.sum(p_o * v_own, axis=0, keepdims=True))
        rows.append(out / l)
    o_ref[...] = jnp.concatenate(rows, axis=0)


def _moba_sample(q, k_new, v_new, cache_k3, cache_v3, sel_flat, page_table_flat, *, n_seq, n_pages):
    m, d = q.shape
    t = m // n_seq
    n_h = d // HEAD_DIM
    n_sel = MOBA_TOP_K * MOBA_BLOCK
    row_spec = pl.BlockSpec((t, HEAD_DIM), lambda b, h, sel, pt: (b, h))
    grid_spec = pltpu.PrefetchScalarGridSpec(
        num_scalar_prefetch=2,
        grid=(n_seq, n_h),
        in_specs=[row_spec, row_spec, row_spec, pl.BlockSpec(memory_space=pl.ANY), pl.BlockSpec(memory_space=pl.ANY)],
        out_specs=row_spec,
        scratch_shapes=[
            pltpu.VMEM((2, t, n_sel, HEAD_DIM), F32),
            pltpu.VMEM((2, t, n_sel, HEAD_DIM), F32),
            pltpu.SemaphoreType.DMA((2, 2)),
        ],
    )
    return pl.pallas_call(
        functools.partial(_sample_attn_kernel, n_pages=n_pages),
        grid_spec=grid_spec,
        out_shape=jax.ShapeDtypeStruct((m, d), F32),
        compiler_params=_params(2),
        name="moba_sample",
    )(sel_flat, page_table_flat, q, k_new, v_new, cache_k3, cache_v3)


def _rope_tables(pos):
    half = HEAD_DIM // 2
    inv = 1.0 / (ROPE_THETA ** (jnp.arange(half, dtype=F32) * (2.0 / HEAD_DIM)))
    ang = pos.astype(F32)[:, None] * inv[None, :]
    cos = jnp.cos(ang)
    sin = jnp.sin(ang)
    return jnp.concatenate([cos, cos], axis=-1), jnp.concatenate([-sin, sin], axis=-1)


def kernel(x_prompt, x_sample, state_pool, cache_k, cache_v, page_table, g_pool, w_pool, s_pool, g_ffn, w_gate_up,
           w_down, g_kv, w_kv, g_attn, w_q, w_o, g_final):
    n_b, seq, d = x_prompt.shape
    n_db, t_dec, _ = x_sample.shape
    depth = g_ffn.shape[0]
    n_pool = g_pool.shape[0]
    n_h = d // HEAD_DIM
    n_pages = page_table.shape[1]
    n_past_blk = PAST_LEN // MOBA_BLOCK
    m_p = n_b * seq
    m_s = n_db * t_dec
    state_rows = state_pool.shape[2]
    assert n_pages == n_past_blk * PAGES_PER_BLOCK, "own MoBA block must hold only the new tokens"
    assert seq % MOBA_BLOCK == 0 and state_rows == POOL_HALO - 1

    tm_pool, tm_ffn, tf, tm_proj = ROW_TILE, ROW_TILE, FF_CHUNK, ROW_TILE
    assert seq % ROW_TILE == 0 and w_down.shape[1] % FF_CHUNK == 0


    cos_p, sin_p = _rope_tables(jnp.arange(seq))
    cos_s, sin_s = _rope_tables(PAST_LEN + jnp.arange(t_dec))
    cos_s = jnp.tile(cos_s, (n_db, 1))
    sin_s = jnp.tile(sin_s, (n_db, 1))

    pt_flat = page_table.reshape(-1)

    hp = x_prompt
    hs = x_sample
    pool_p, pool_s, bm_parts = [], [], []
    assert n_pool > 0 and (n_db * n_pages) % (n_pool * PAGES_PER_BLOCK) == 0
    for layer in range(depth):
        last = layer == depth - 1
        if layer < n_pool:
            first = (w_gate_up, w_down, 0) if layer == 0 else None
            hp3, zt_p, *first_b = _pool_layer(hp.reshape(n_b, seq, d), hp.reshape(n_b, seq, d), g_pool[layer],
                                              w_pool[layer], s_pool[layer], tm=tm_pool, halo_normalized=False, pos0=0,
                                              round_weights=first)
            if first_b:
                w_gu_b, w_dn_b = first_b
            state = state_pool[layer]
            halo_s = jnp.pad(state, ((0, 0), (POOL_HALO - state_rows, 0), (0, 0)))
            hs3, zt_s = _pool_layer(hs.reshape(n_db, t_dec, d), halo_s, g_pool[layer], w_pool[layer], s_pool[layer],
                                    tm=t_dec, halo_normalized=True, pos0=PAST_LEN)
            pool_p.append(zt_p[:, POOL_HALO - state_rows:])
            pool_s.append(jnp.concatenate([state, zt_s], axis=1)[:, -state_rows:])
            hp = hp3.reshape(m_p, d)
            hs = hs3.reshape(m_s, d)
        else:
            a = layer - n_pool
            if a == 0:
                k_p, k_pb, kmean, k_s = _k_proj(hp, hs, g_kv, w_kv, cos_p, sin_p, cos_s, sin_s, tm=tm_proj)
                v_p, vt_pb, v_s = _v_proj(hp, hs, g_kv, w_kv, tm=tm_proj, seq=seq)
                kmean = kmean.reshape(n_b, seq // MOBA_BLOCK, d)
                bm = jnp.concatenate(bm_parts, axis=0).reshape(n_db, n_past_blk, n_h, HEAD_DIM)
            q_p, q_s = _q_proj(hp, hs, g_attn[a], w_q, a, cos_p, sin_p, cos_s, sin_s, tm=tm_proj)
            att_p = _moba_prompt(q_p, k_pb, vt_pb, kmean, batch=n_b, seq=seq)
            sel = _select_blocks(q_s, bm, n_seq=n_db)
            sel_flat = sel[:, :, :n_h * MOBA_TOP_K].reshape(-1)
            att_s = _moba_sample(q_s, k_s, v_s, cache_k, cache_v, sel_flat, pt_flat, n_seq=n_db, n_pages=n_pages)
            hp, hs = _o_proj(att_p, att_s, w_o, a, hp, hs, tm=tm_proj)
        pages_per_call = n_db * n_pages // n_pool
        cache_means = (cache_k, pt_flat, layer * pages_per_call, pages_per_call) if layer < n_pool else None
        next_weights = None if last else (w_gate_up, w_down, layer + 1)
        hp, hs, bm_part, next_b = _ffn(hp, hs, g_ffn[layer], w_gu_b, w_dn_b, g_final, tm=tm_ffn, tf=tf, final_norm=last,
                                       cache_means=cache_means, next_weights=next_weights)
        if bm_part is not None:
            bm_parts.append(bm_part)
        if next_b is not None:
            w_gu_b, w_dn_b = next_b

    y_prompt = hp.reshape(n_b, seq, d)
    y_sample = hs.reshape(n_db, t_dec, d)
    new_pool_prompt = jnp.stack(pool_p)
    new_pool_sample = jnp.stack(pool_s)
    shape_p = (n_b, seq, n_h, HEAD_DIM)
    shape_s = (n_db, t_dec, n_h, HEAD_DIM)
    return (y_prompt, y_sample, new_pool_prompt, new_pool_sample, k_p.reshape(shape_p), v_p.reshape(shape_p),
            k_s.reshape(shape_s), v_s.reshape(shape_s))
```

```python
import functools

import jax
import jax.numpy as jnp
from jax import lax
from jax.experimental import pallas as pl
from jax.experimental.pallas import tpu as pltpu

F32 = jnp.float32
BF16 = jnp.bfloat16

POOL_WINDOWS = (2, 4, 8, 16)
POOL_HALO = 16
HEAD_DIM = 128
MOBA_BLOCK = 256
MOBA_TOP_K = 3
PAGE_SIZE = 128
PAGES_PER_BLOCK = MOBA_BLOCK // PAGE_SIZE
PAST_LEN = 16384
ROPE_THETA = 10000.0
RMS_EPS = 1e-6
NEG_INF = -1e30
GATE_FLOOR = -3.0e38

VMEM_LIMIT_BYTES = 56 * 1024 * 1024
ROW_TILE = 512
FF_CHUNK = 512


def _params(n_axes):
    return pltpu.CompilerParams(dimension_semantics=("arbitrary",) * n_axes, vmem_limit_bytes=VMEM_LIMIT_BYTES)


def _rmsnorm(x, g):
    ms = jnp.mean(x * x, axis=-1, keepdims=True)
    return x * lax.rsqrt(ms + RMS_EPS) * g


def _pool_kernel(*refs, tm, halo_normalized, pos0, round_weights):
    if round_weights:
        x_ref, halo_ref, g_ref, w_ref, s_ref, gu_in, dn_in, h_ref, zt_ref, gu_out, dn_out, zs_ref = refs
        gu_out[...] = gu_in[...].astype(BF16)
        dn_out[...] = dn_in[...].astype(BF16)
    else:
        x_ref, halo_ref, g_ref, w_ref, s_ref, h_ref, zt_ref, zs_ref = refs
    i = pl.program_id(1)
    d = x_ref.shape[-1]
    c = d // len(POOL_WINDOWS)
    x = x_ref[...]
    g = g_ref[...]
    z = _rmsnorm(x, g)
    if halo_normalized:
        zh = halo_ref[...]
    else:
        zh = _rmsnorm(halo_ref[...], g) * (i > 0).astype(F32)
    zs_ref[0:POOL_HALO, :] = zh
    zs_ref[POOL_HALO:POOL_HALO + tm, :] = z
    pos = lax.broadcasted_iota(jnp.int32, (tm, c), 0) + (i * tm + pos0)
    for gi, w in enumerate(POOL_WINDOWS):
        cols = slice(gi * c, (gi + 1) * c)
        zc = z[:, cols]
        if w <= 8:
            tot = zc
            for j in range(1, w):
                tot = tot + zs_ref[POOL_HALO - j:POOL_HALO - j + tm, cols]
        else:
            assert w % 8 == 0 and w <= POOL_HALO
            lead = w - 8
            s8 = zs_ref[POOL_HALO - lead:POOL_HALO + tm, cols]
            for j in range(1, 8):
                s8 = s8 + zs_ref[POOL_HALO - lead - j:POOL_HALO + tm - j, cols]
            tot = s8[lead:, :]
            for k in range(1, w // 8):
                tot = tot + s8[lead - 8 * k:lead - 8 * k + tm, :]
        cnt = jnp.minimum(pos + 1, w).astype(F32)
        pooled = tot / cnt - zc
        mixed = jnp.dot(pooled.astype(BF16), w_ref[gi].astype(BF16), preferred_element_type=F32)
        h_ref[:, cols] = x[:, cols] + mixed * s_ref[:, cols]

    nt = zt_ref.shape[0]

    @pl.when(i == pl.num_programs(1) - 1)
    def _():
        zt_ref[...] = z[tm - nt:, :]


def _pool_layer(x3, halo3, g, w, s, *, tm, halo_normalized, pos0, round_weights=None):
    nb, length, d = x3.shape
    nt = min(POOL_HALO, tm)
    n_i = length // tm
    hb = tm // POOL_HALO
    if halo_normalized:
        halo_map = lambda b, i: (b, 0, 0)
    else:
        halo_map = lambda b, i: (b, jnp.maximum(i * hb - 1, 0), 0)
    in_specs = [
        pl.BlockSpec((None, tm, d), lambda b, i: (b, i, 0)),
        pl.BlockSpec((None, POOL_HALO, d), halo_map),
        pl.BlockSpec((1, d), lambda b, i: (0, 0)),
        pl.BlockSpec(w.shape, lambda b, i: (0, 0, 0), pipeline_mode=pl.Buffered(1)),
        pl.BlockSpec((1, d), lambda b, i: (0, 0)),
    ]
    out_specs = [
        pl.BlockSpec((None, tm, d), lambda b, i: (b, i, 0)),
        pl.BlockSpec((None, nt, d), lambda b, i: (b, 0, 0)),
    ]
    out_shape = [jax.ShapeDtypeStruct((nb, length, d), F32), jax.ShapeDtypeStruct((nb, nt, d), F32)]
    operands = [x3, halo3, g.reshape(1, d), w, s.reshape(1, d)]
    if round_weights is not None:
        w_gu_all, w_dn_all, layer = round_weights
        n_steps = nb * n_i
        two_f, f = w_gu_all.shape[2], w_dn_all.shape[1]
        gu_rows, dn_rows = d // n_steps, f // n_steps
        assert gu_rows * n_steps == d and dn_rows * n_steps == f
        in_specs += [pl.BlockSpec((None, gu_rows, two_f), lambda b, i: (layer, b * n_i + i, 0)),
                     pl.BlockSpec((None, dn_rows, d), lambda b, i: (layer, b * n_i + i, 0))]
        out_specs += [pl.BlockSpec((gu_rows, two_f), lambda b, i: (b * n_i + i, 0)),
                      pl.BlockSpec((dn_rows, d), lambda b, i: (b * n_i + i, 0))]
        out_shape += [jax.ShapeDtypeStruct((d, two_f), BF16), jax.ShapeDtypeStruct((f, d), BF16)]
        operands += [w_gu_all, w_dn_all]
    kern = functools.partial(_pool_kernel, tm=tm, halo_normalized=halo_normalized, pos0=pos0,
                             round_weights=round_weights is not None)
    return pl.pallas_call(
        kern,
        grid=(nb, n_i),
        in_specs=in_specs,
        out_specs=out_specs,
        out_shape=out_shape,
        scratch_shapes=[pltpu.VMEM((POOL_HALO + tm, d), F32)],
        compiler_params=_params(2),
        name="pool_layer",
    )(*operands)


def _stream_block_means(pt_ref, cache_hbm, bm_ref, page_buf, page_sem, *, first_page, n_call_pages):
    step = pl.program_id(0) * pl.num_programs(1) + pl.program_id(1)
    pages_per_step = page_buf.shape[1]
    blocks_per_step = pages_per_step // PAGES_PER_BLOCK
    n_fetch = n_call_pages // pages_per_step

    def page_copy(at_step, p, slot):
        phys = pt_ref[first_page + at_step * pages_per_step + p]
        return pltpu.make_async_copy(cache_hbm.at[phys], page_buf.at[slot, p], page_sem.at[slot])

    def start_step(at_step, slot):
        for p in range(pages_per_step):
            page_copy(at_step, p, slot).start(priority=1)

    @pl.when(step == 0)
    def _():
        start_step(0, 0)

    @pl.when(step + 1 < n_fetch)
    def _():
        start_step(step + 1, (step + 1) % 2)

    @pl.when(step < n_fetch)
    def _():
        for p in range(pages_per_step):
            page_copy(step, p, step % 2).wait()

    def add_pages():
        src = jnp.minimum(step, n_fetch - 1)
        slot = src % 2
        for b in range(blocks_per_step):
            rows = sum(jnp.sum(page_buf[slot, PAGES_PER_BLOCK * b + pg], axis=0) for pg in range(PAGES_PER_BLOCK))
            bm_ref[src * blocks_per_step + b] = rows * (1.0 / (PAGES_PER_BLOCK * PAGE_SIZE))

    return add_pages


WEIGHT_SLOTS = 3


def _weight_ring(wg_hbm, wu_hbm, wd_hbm, wg_buf, wu_buf, wd_buf, w_sem):
    n_f = pl.num_programs(1)
    n_steps = pl.num_programs(0) * n_f
    step = pl.program_id(0) * n_f + pl.program_id(1)
    tf = wd_buf.shape[1]

    def copies(at_step):
        at_step = jnp.asarray(at_step, jnp.int32)
        slot = at_step % WEIGHT_SLOTS
        j = at_step % n_f
        cols = lambda c: pl.ds(pl.multiple_of(c * tf, tf), tf)
        return [pltpu.make_async_copy(wg_hbm.at[:, cols(j)], wg_buf.at[slot], w_sem.at[slot]),
                pltpu.make_async_copy(wu_hbm.at[:, cols(j + n_f)], wu_buf.at[slot], w_sem.at[slot]),
                pltpu.make_async_copy(wd_hbm.at[cols(j), :], wd_buf.at[slot], w_sem.at[slot])]

    @pl.when(step == 0)
    def _():
        for t in range(WEIGHT_SLOTS - 1):
            for cp in copies(t):
                cp.start()

    @pl.when(step + (WEIGHT_SLOTS - 1) < n_steps)
    def _():
        for cp in copies(step + (WEIGHT_SLOTS - 1)):
            cp.start()

    for cp in copies(step):
        cp.wait()
    slot = step % WEIGHT_SLOTS
    return wg_buf.at[slot], wu_buf.at[slot], wd_buf.at[slot]


def _ffn_kernel(*refs, final_norm, means, round_next):
    refs = list(refs)
    take = lambda n: [refs.pop(0) for _ in range(n)]
    pt_ref, = take(1) if means is not None else (None,)
    x_ref, xs_ref, g_ref, wg_ref, wu_ref, wd_ref, go_ref = take(7)
    cache_hbm, = take(1) if means is not None else (None,)
    next_in = take(2) if round_next else None
    o_ref, os_ref = take(2)
    bm_ref, = take(1) if means is not None else (None,)
    next_out = take(2) if round_next else None
    xn_ref, xns_ref = take(2)
    wg_ref, wu_ref, wd_ref = _weight_ring(wg_ref, wu_ref, wd_ref, *take(4))
    side_work = None
    if means is not None:
        page_buf, page_sem = take(2)
        side_work = _stream_block_means(pt_ref, cache_hbm, bm_ref, page_buf, page_sem, **means)
    j = pl.program_id(1)

    def rows(x_ref, o_ref, xn_ref, with_side_jobs):
        @pl.when(j == 0)
        def _():
            x = x_ref[...]
            xn_ref[...] = _rmsnorm(x, g_ref[...]).astype(BF16)
            o_ref[...] = x

        xn = xn_ref[...]
        gate = jnp.dot(xn, wg_ref[...], preferred_element_type=F32)
        up = jnp.dot(xn, wu_ref[...], preferred_element_type=F32)
        act = (gate * jax.nn.sigmoid(gate) * up).astype(BF16)
        o_ref[...] += jnp.dot(act, wd_ref[...], preferred_element_type=F32)
        if with_side_jobs and round_next:
            for src, dst in zip(next_in, next_out):
                dst[...] = src[...].astype(BF16)
        if with_side_jobs and side_work is not None:
            side_work()

        if final_norm:
            @pl.when(j == pl.num_programs(1) - 1)
            def _():
                o_ref[...] = _rmsnorm(o_ref[...], go_ref[...])

    rows(x_ref, o_ref, xn_ref, True)

    @pl.when(pl.program_id(0) == pl.num_programs(0) - 1)
    def _():
        rows(xs_ref, os_ref, xns_ref, False)


def _ffn(x, xs, g, w_gu_b, w_dn_b, g_out, *, tm, tf, final_norm, cache_means=None, next_weights=None):
    m, d = x.shape
    m_s = xs.shape[0]
    f = w_dn_b.shape[0]
    n_f = f // tf
    grid = (m // tm, n_f)
    in_specs = [
        pl.BlockSpec((tm, d), lambda i, j, *_: (i, 0)),
        pl.BlockSpec((m_s, d), lambda i, j, *_: (0, 0)),
        pl.BlockSpec((1, d), lambda i, j, *_: (0, 0)),
        pl.BlockSpec(memory_space=pl.ANY),
        pl.BlockSpec(memory_space=pl.ANY),
        pl.BlockSpec(memory_space=pl.ANY),
        pl.BlockSpec((1, d), lambda i, j, *_: (0, 0)),
    ]
    out_specs = [pl.BlockSpec((tm, d), lambda i, j, *_: (i, 0)), pl.BlockSpec((m_s, d), lambda i, j, *_: (0, 0))]
    out_shape = [jax.ShapeDtypeStruct((m, d), F32), jax.ShapeDtypeStruct((m_s, d), F32)]
    scratch = [pltpu.VMEM((tm, d), BF16), pltpu.VMEM((m_s, d), BF16),
               pltpu.VMEM((WEIGHT_SLOTS, d, tf), BF16), pltpu.VMEM((WEIGHT_SLOTS, d, tf), BF16),
               pltpu.VMEM((WEIGHT_SLOTS, tf, d), BF16), pltpu.SemaphoreType.DMA((WEIGHT_SLOTS,))]
    operands = [x, xs, g.reshape(1, d), w_gu_b, w_gu_b, w_dn_b, g_out.reshape(1, d)]
    next_specs, next_shapes, next_operands = [], [], []
    if next_weights is not None:
        w_gu_all, w_dn_all, nxt = next_weights
        gu_chunk = (d // grid[0], 2 * f // grid[1])
        dn_rows = f // (grid[0] * grid[1])
        assert gu_chunk[0] * grid[0] == d and gu_chunk[1] * grid[1] == 2 * f and dn_rows * grid[0] * grid[1] == f
        next_operands = [w_gu_all, w_dn_all]
        next_specs = [pl.BlockSpec((None,) + gu_chunk, lambda i, j, *_: (nxt, i, j)),
                      pl.BlockSpec((None, dn_rows, d), lambda i, j, *_: (nxt, i * n_f + j, 0))]
        next_out_specs = [pl.BlockSpec(gu_chunk, lambda i, j, *_: (i, j)),
                          pl.BlockSpec((dn_rows, d), lambda i, j, *_: (i * n_f + j, 0))]
        next_shapes = [jax.ShapeDtypeStruct((d, 2 * f), BF16), jax.ShapeDtypeStruct((f, d), BF16)]
    if cache_means is None:
        means = None
        prefetch = []
    else:
        cache_k, page_table_flat, first_page, n_call_pages = cache_means
        _, page, n_h, dh = cache_k.shape
        blocks_per_step = -(-n_call_pages // (PAGES_PER_BLOCK * grid[0] * grid[1]))
        while n_call_pages % (PAGES_PER_BLOCK * blocks_per_step):
            blocks_per_step += 1
        pages_per_step = PAGES_PER_BLOCK * blocks_per_step
        means = dict(first_page=first_page, n_call_pages=n_call_pages)
        prefetch = [page_table_flat]
        in_specs.append(pl.BlockSpec(memory_space=pl.ANY))
        operands.append(cache_k)
        n_call_blk = n_call_pages // PAGES_PER_BLOCK
        out_specs.append(pl.BlockSpec((n_call_blk, n_h, dh), lambda i, j, *_: (0, 0, 0)))
        out_shape.append(jax.ShapeDtypeStruct((n_call_blk, n_h, dh), F32))
        scratch += [pltpu.VMEM((2, pages_per_step, page, n_h, dh), F32), pltpu.SemaphoreType.DMA((2,))]
    if next_weights is not None:
        in_specs += next_specs
        operands += next_operands
        out_specs += next_out_specs
        out_shape += next_shapes
    grid_spec = pltpu.PrefetchScalarGridSpec(
        num_scalar_prefetch=len(prefetch), grid=grid, in_specs=in_specs, out_specs=out_specs, scratch_shapes=scratch)
    outs = list(pl.pallas_call(
        functools.partial(_ffn_kernel, final_norm=final_norm, means=means, round_next=next_weights is not None),
        grid_spec=grid_spec,
        out_shape=out_shape,
        compiler_params=_params(2),
        name="swiglu",
    )(*prefetch, *operands))
    out, out_small = outs[:2]
    del outs[:2]
    block_means = outs.pop(0) if cache_means is not None else None
    next_b = tuple(outs) if next_weights is not None else None
    return out, out_small, block_means, next_b


def _rope(xh, cos, sin_signed):
    return xh * cos + pltpu.roll(xh, HEAD_DIM // 2, axis=1) * sin_signed


def _round_weight_once(w_ref, wb_ref):
    @pl.when(pl.program_id(0) == 0)
    def _():
        wb_ref[...] = w_ref[...].astype(BF16)


def _prompt_then_sample(n_tiles, prompt_fn, sample_fn):
    i = pl.program_id(0)
    pl.when(i < n_tiles)(prompt_fn)
    pl.when(i == n_tiles)(sample_fn)


def _proj_specs(tm, m_s, n_tiles):
    tile = lambda i: jnp.minimum(i, n_tiles - 1)
    prompt = lambda width: pl.BlockSpec((tm, width), lambda i: (tile(i), 0))
    sample = lambda width: pl.BlockSpec((m_s, width), lambda i: (0, 0))
    return tile, prompt, sample


def _q_kernel(xp_ref, xs_ref, g_ref, w_ref, cosp_ref, sinp_ref, coss_ref, sins_ref, qp_ref, qs_ref, wb_ref, *, n_tiles):
    _round_weight_once(w_ref, wb_ref)

    def rows(x_ref, cos_ref, sin_ref, q_ref):
        xn = _rmsnorm(x_ref[...], g_ref[...]).astype(BF16)
        acc = jnp.dot(xn, wb_ref[...], preferred_element_type=F32)
        cos = cos_ref[...]
        sin = sin_ref[...]
        for h in range(acc.shape[1] // HEAD_DIM):
            cols = slice(h * HEAD_DIM, (h + 1) * HEAD_DIM)
            q_ref[:, cols] = _rope(acc[:, cols], cos, sin)

    _prompt_then_sample(n_tiles, lambda: rows(xp_ref, cosp_ref, sinp_ref, qp_ref),
                        lambda: rows(xs_ref, coss_ref, sins_ref, qs_ref))


def _q_proj(xp, xs, g, w, layer, cos_p, sin_p, cos_s, sin_s, *, tm):
    m, d = xp.shape
    m_s = xs.shape[0]
    n_tiles = m // tm
    n_pos = cos_p.shape[0] // tm
    tile, prompt, sample = _proj_specs(tm, m_s, n_tiles)
    rope_p = pl.BlockSpec((tm, HEAD_DIM), lambda i: (tile(i) % n_pos, 0))
    return pl.pallas_call(
        functools.partial(_q_kernel, n_tiles=n_tiles),
        grid=(n_tiles + 1,),
        in_specs=[
            prompt(d), sample(d),
            pl.BlockSpec((1, d), lambda i: (0, 0)),
            pl.BlockSpec((None, d, d), lambda i: (layer, 0, 0), pipeline_mode=pl.Buffered(1)),
            rope_p, rope_p, sample(HEAD_DIM), sample(HEAD_DIM),
        ],
        out_specs=[prompt(d), sample(d)],
        out_shape=[jax.ShapeDtypeStruct((m, d), F32), jax.ShapeDtypeStruct((m_s, d), F32)],
        scratch_shapes=[pltpu.VMEM((d, d), BF16)],
        compiler_params=_params(1),
        name="q_proj",
    )(xp, xs, g.reshape(1, d), w, cos_p, sin_p, cos_s, sin_s)


def _k_kernel(xp_ref, xs_ref, g_ref, w_ref, cosp_ref, sinp_ref, coss_ref, sins_ref, kp_ref, kb_ref, km_ref, ks_ref,
              wb_ref, *, n_tiles):
    _round_weight_once(w_ref, wb_ref)

    def rows(x_ref, cos_ref, sin_ref, k_ref):
        xn = _rmsnorm(x_ref[...], g_ref[...]).astype(BF16)
        acc = jnp.dot(xn, wb_ref[...], preferred_element_type=F32)
        cos = cos_ref[...]
        sin = sin_ref[...]
        for h in range(acc.shape[1] // HEAD_DIM):
            cols = slice(h * HEAD_DIM, (h + 1) * HEAD_DIM)
            k_ref[:, cols] = _rope(acc[:, cols], cos, sin)

    def prompt_rows():
        rows(xp_ref, cosp_ref, sinp_ref, kp_ref)
        k = kp_ref[...]
        kb_ref[...] = k.astype(BF16)
        for r in range(k.shape[0] // MOBA_BLOCK):
            km_ref[r] = jnp.mean(k[r * MOBA_BLOCK:(r + 1) * MOBA_BLOCK, :], axis=0, keepdims=True)

    _prompt_then_sample(n_tiles, prompt_rows, lambda: rows(xs_ref, coss_ref, sins_ref, ks_ref))


def _k_proj(xp, xs, g, w_kv, cos_p, sin_p, cos_s, sin_s, *, tm):
    m, d = xp.shape
    m_s = xs.shape[0]
    n_tiles = m // tm
    n_pos = cos_p.shape[0] // tm
    r = tm // MOBA_BLOCK
    tile, prompt, sample = _proj_specs(tm, m_s, n_tiles)
    rope_p = pl.BlockSpec((tm, HEAD_DIM), lambda i: (tile(i) % n_pos, 0))
    return pl.pallas_call(
        functools.partial(_k_kernel, n_tiles=n_tiles),
        grid=(n_tiles + 1,),
        in_specs=[
            prompt(d), sample(d),
            pl.BlockSpec((1, d), lambda i: (0, 0)),
            pl.BlockSpec((d, d), lambda i: (0, 0), pipeline_mode=pl.Buffered(1)),
            rope_p, rope_p, sample(HEAD_DIM), sample(HEAD_DIM),
        ],
        out_specs=[prompt(d), prompt(d), pl.BlockSpec((r, 1, d), lambda i: (tile(i), 0, 0)), sample(d)],
        out_shape=[jax.ShapeDtypeStruct((m, d), F32), jax.ShapeDtypeStruct((m, d), BF16),
                   jax.ShapeDtypeStruct((m // MOBA_BLOCK, 1, d), F32), jax.ShapeDtypeStruct((m_s, d), F32)],
        scratch_shapes=[pltpu.VMEM((d, d), BF16)],
        compiler_params=_params(1),
        name="k_proj",
    )(xp, xs, g.reshape(1, d), w_kv, cos_p, sin_p, cos_s, sin_s)


def _v_kernel(xp_ref, xs_ref, g_ref, w_ref, vp_ref, vt_ref, vs_ref, wb_ref, *, n_tiles):
    _round_weight_once(w_ref, wb_ref)

    def values(x_ref):
        xn = _rmsnorm(x_ref[...], g_ref[...]).astype(BF16)
        return jnp.dot(xn, wb_ref[...], preferred_element_type=F32)

    def prompt_rows():
        v = values(xp_ref)
        vp_ref[...] = v
        for h in range(v.shape[1] // HEAD_DIM):
            for r in range(v.shape[0] // MOBA_BLOCK):
                blk = v[r * MOBA_BLOCK:(r + 1) * MOBA_BLOCK, h * HEAD_DIM:(h + 1) * HEAD_DIM]
                vt_ref[h, r] = blk.T.astype(BF16)

    def sample_rows():
        vs_ref[...] = values(xs_ref)

    _prompt_then_sample(n_tiles, prompt_rows, sample_rows)


def _v_proj(xp, xs, g, w_kv, *, tm, seq):
    m, d = xp.shape
    m_s = xs.shape[0]
    n_h = d // HEAD_DIM
    n_tiles = m // tm
    r = tm // MOBA_BLOCK
    tiles_per_seq = seq // tm
    tile, prompt, sample = _proj_specs(tm, m_s, n_tiles)
    vt_spec = pl.BlockSpec((None, n_h, r, HEAD_DIM, MOBA_BLOCK),
                           lambda i: (tile(i) // tiles_per_seq, 0, tile(i) % tiles_per_seq, 0, 0))
    return pl.pallas_call(
        functools.partial(_v_kernel, n_tiles=n_tiles),
        grid=(n_tiles + 1,),
        in_specs=[
            prompt(d), sample(d),
            pl.BlockSpec((1, d), lambda i: (0, 0)),
            pl.BlockSpec((d, d), lambda i: (0, 1), pipeline_mode=pl.Buffered(1)),
        ],
        out_specs=[prompt(d), vt_spec, sample(d)],
        out_shape=[jax.ShapeDtypeStruct((m, d), F32),
                   jax.ShapeDtypeStruct((m // seq, n_h, seq // MOBA_BLOCK, HEAD_DIM, MOBA_BLOCK), BF16),
                   jax.ShapeDtypeStruct((m_s, d), F32)],
        scratch_shapes=[pltpu.VMEM((d, d), BF16)],
        compiler_params=_params(1),
        name="v_proj",
    )(xp, xs, g.reshape(1, d), w_kv)


def _o_kernel(ap_ref, as_ref, w_ref, hp_ref, hs_ref, op_ref, os_ref, wb_ref, *, n_tiles):
    _round_weight_once(w_ref, wb_ref)

    def rows(a_ref, h_ref, o_ref):
        o_ref[...] = h_ref[...] + jnp.dot(a_ref[...].astype(BF16), wb_ref[...], preferred_element_type=F32)

    _prompt_then_sample(n_tiles, lambda: rows(ap_ref, hp_ref, op_ref), lambda: rows(as_ref, hs_ref, os_ref))


def _o_proj(ap, a_s, w, layer, hp, hs, *, tm):
    m, d = hp.shape
    m_s = hs.shape[0]
    n_tiles = m // tm
    _, prompt, sample = _proj_specs(tm, m_s, n_tiles)
    return pl.pallas_call(
        functools.partial(_o_kernel, n_tiles=n_tiles),
        grid=(n_tiles + 1,),
        in_specs=[
            prompt(d), sample(d),
            pl.BlockSpec((None, d, d), lambda i: (layer, 0, 0), pipeline_mode=pl.Buffered(1)),
            prompt(d), sample(d),
        ],
        out_specs=[prompt(d), sample(d)],
        out_shape=[jax.ShapeDtypeStruct((m, d), F32), jax.ShapeDtypeStruct((m_s, d), F32)],
        scratch_shapes=[pltpu.VMEM((d, d), BF16)],
        compiler_params=_params(1),
        name="o_proj",
    )(ap, a_s, w, hp, hs)


def _top_k_picks(gate, candidate, blk_f, axis):
    remaining = candidate
    for _ in range(MOBA_TOP_K):
        gm = jnp.where(remaining, gate, GATE_FLOOR)
        top = jnp.max(gm, axis=axis, keepdims=True)
        hit = jnp.logical_and(remaining, gm == top)
        idx = jnp.min(jnp.where(hit, blk_f, 1e9), axis=axis, keepdims=True)
        pick = blk_f == idx
        remaining = jnp.logical_and(remaining, jnp.logical_not(pick))
        yield pick, idx


def _attn_kernel(q_ref, k_ref, vt_ref, km_ref, o_ref, qt_ref, acc_ref, sa_ref, sb_ref):
    n_g, n_blk = qt_ref.shape[0], qt_ref.shape[1]
    tq = MOBA_BLOCK
    q_scale = HEAD_DIM ** -0.5 * 1.4426950408889634
    blk_i = lax.broadcasted_iota(jnp.int32, (n_blk, tq), 0)
    blk_f = blk_i.astype(F32)
    heads = [slice(g * HEAD_DIM, (g + 1) * HEAD_DIM) for g in range(n_g)]
    assert n_blk <= qt_ref.shape[2] - HEAD_DIM

    pad = jnp.zeros((qt_ref.shape[2] - HEAD_DIM - n_blk, tq), BF16)
    for g in range(n_g):
        km = km_ref[:, heads[g]]
        for i in range(n_blk):
            q_t = q_ref[i * tq:(i + 1) * tq, heads[g]].T
            qt_ref[g, i, 0:HEAD_DIM, :] = (q_t * q_scale).astype(BF16)
            if i <= MOBA_TOP_K:
                selected = blk_i < i
            else:
                gate_t = jnp.dot(km, q_t, preferred_element_type=F32, precision=lax.Precision.HIGHEST)
                selected = jnp.zeros((n_blk, tq), jnp.bool_)
                for pick, _ in _top_k_picks(gate_t, blk_i < i, blk_f, 0):
                    selected = jnp.logical_or(selected, pick)
            qt_ref[g, i, HEAD_DIM:HEAD_DIM + n_blk, :] = jnp.where(selected, 0.0, NEG_INF).astype(BF16)
            qt_ref[g, i, HEAD_DIM + n_blk:, :] = pad

    kpos = lax.broadcasted_iota(jnp.int32, (tq, tq), 0)
    qpos = lax.broadcasted_iota(jnp.int32, (tq, tq), 1)
    causal = kpos <= qpos

    last_pair = n_blk // 2 - 1
    pair_row = lax.broadcasted_iota(jnp.int32, (2 * tq, HEAD_DIM), 0)
    pair_lane = lax.broadcasted_iota(jnp.int32, (2 * tq, HEAD_DIM), 1)
    lane_minus_half = pair_lane - jnp.where(pair_row >= tq, 1, 0)
    sum_rows = acc_ref.shape[1] - HEAD_DIM
    ones_rows = jnp.where(lax.broadcasted_iota(jnp.int32, (sum_rows, tq), 0) == 0, 1.0, 0.0).astype(BF16)

    def values_aug(g, j):
        return jnp.concatenate([vt_ref[g, j], ones_rows], axis=0)

    def tile(i, _):
        start = pl.multiple_of(i * tq, tq)

        def score_pair(pair_idx, dst_ref):
            pair_c = jnp.minimum(pair_idx, last_pair)
            st = pl.multiple_of(pair_c * (2 * tq), 2 * tq)
            onehot = jnp.where(lane_minus_half == 2 * pair_c, 1.0, 0.0).astype(BF16)
            for g in range(n_g):
                keys_aug = jnp.concatenate([k_ref[pl.ds(st, 2 * tq), heads[g]], onehot], axis=1)
                dst_ref[g] = jnp.dot(keys_aug, qt_ref[g, i], preferred_element_type=F32)

        def fold_pair(src_ref, pair_idx, ms):
            j0 = 2 * jnp.minimum(pair_idx, last_pair)
            new_ms, alphas, ps = [], [], []
            for g in range(n_g):
                sa = src_ref[g, 0:tq, :]
                sb = src_ref[g, tq:2 * tq, :]
                m_blk = jnp.maximum(jnp.max(sa, axis=0, keepdims=True), jnp.max(sb, axis=0, keepdims=True))
                m_new = jnp.maximum(ms[g], m_blk)
                new_ms.append(m_new)
                alphas.append(jnp.exp2(ms[g] - m_new))
                ps.append((jnp.exp2(sa - m_new).astype(BF16), jnp.exp2(sb - m_new).astype(BF16)))
            for g in range(n_g):
                pv = (jnp.dot(values_aug(g, j0), ps[g][0], preferred_element_type=F32)
                      + jnp.dot(values_aug(g, j0 + 1), ps[g][1], preferred_element_type=F32))
                acc_ref[g] = alphas[g] * acc_ref[g] + pv
            return tuple(new_ms)

        ms = []
        s_own = [jnp.dot(k_ref[pl.ds(start, tq), heads[g]], qt_ref[g, i, 0:HEAD_DIM, :], preferred_element_type=F32)
                 for g in range(n_g)]
        score_pair(0, sa_ref)
        p_own = []
        for g in range(n_g):
            s = jnp.where(causal, s_own[g], NEG_INF)
            m0 = jnp.max(s, axis=0, keepdims=True)
            ms.append(m0)
            p_own.append(jnp.exp2(s - m0).astype(BF16))
        for g in range(n_g):
            acc_ref[g] = jnp.dot(values_aug(g, i), p_own[g], preferred_element_type=F32)

        def two_pairs(t, ms):
            score_pair(2 * t + 1, sb_ref)
            ms = fold_pair(sa_ref, 2 * t, ms)
            score_pair(2 * t + 2, sa_ref)
            return fold_pair(sb_ref, 2 * t + 1, ms)

        n_pairs = (i + 1) // 2
        ms = lax.fori_loop(0, n_pairs // 2, two_pairs, tuple(ms))

        @pl.when(n_pairs % 2 == 1)
        def _():
            fold_pair(sa_ref, n_pairs - 1, ms)

        for g in range(n_g):
            inv_l = 1.0 / acc_ref[g, HEAD_DIM:HEAD_DIM + 1, :]
            o_ref[pl.ds(start, tq), heads[g]] = (acc_ref[g, 0:HEAD_DIM, :] * inv_l).T.astype(o_ref.dtype)
        return 0

    lax.fori_loop(0, n_blk, tile, 0)


ATTN_HEADS_PER_STEP = 4


ATTN_SUM_ROWS = 16


def _moba_prompt(q, k_b, vt_b, kmean, *, batch, seq):
    m, d = q.shape
    n_g = ATTN_HEADS_PER_STEP
    n_blk = seq // MOBA_BLOCK
    width = n_g * HEAD_DIM
    return pl.pallas_call(
        _attn_kernel,
        grid=(batch, d // width),
        in_specs=[
            pl.BlockSpec((seq, width), lambda b, h: (b, h)),
            pl.BlockSpec((seq, width), lambda b, h: (b, h)),
            pl.BlockSpec((None, n_g, n_blk, HEAD_DIM, MOBA_BLOCK), lambda b, h: (b, h, 0, 0, 0)),
            pl.BlockSpec((None, n_blk, width), lambda b, h: (b, 0, h)),
        ],
        out_specs=pl.BlockSpec((seq, width), lambda b, h: (b, h), pipeline_mode=pl.Buffered(1)),
        out_shape=jax.ShapeDtypeStruct((m, d), BF16),
        scratch_shapes=[
            pltpu.VMEM((n_g, n_blk, 2 * HEAD_DIM, MOBA_BLOCK), BF16),
            pltpu.VMEM((n_g, HEAD_DIM + ATTN_SUM_ROWS, MOBA_BLOCK), F32),
            pltpu.VMEM((n_g, 2 * MOBA_BLOCK, MOBA_BLOCK), F32),
            pltpu.VMEM((n_g, 2 * MOBA_BLOCK, MOBA_BLOCK), F32),
        ],
        compiler_params=_params(2),
        name="moba_prompt",
    )(q, k_b, vt_b, kmean)


SELECT_LANES = 128


def _select_kernel(q_ref, bm_ref, sel_ref):
    q = q_ref[...]
    t = q.shape[0]
    n_valid = bm_ref.shape[0]
    lane = lax.broadcasted_iota(jnp.int32, (t, SELECT_LANES), 1)
    lane_f = lane.astype(F32)
    out = jnp.zeros((t, SELECT_LANES), jnp.int32)
    no_block = jnp.zeros((SELECT_LANES - n_valid, HEAD_DIM), F32)
    for h in range(q.shape[1] // HEAD_DIM):
        cols = slice(h * HEAD_DIM, (h + 1) * HEAD_DIM)
        means_h = jnp.concatenate([bm_ref[:, h, :], no_block], axis=0)
        gate = lax.dot_general(q[:, cols], means_h, (((1,), (1,)), ((), ())),
                               preferred_element_type=F32, precision=lax.Precision.HIGHEST)
        for r, (_, idx) in enumerate(_top_k_picks(gate, lane < n_valid, lane_f, 1)):
            out = jnp.where(lane == h * MOBA_TOP_K + r, idx.astype(jnp.int32), out)
    sel_ref[...] = out


def _select_blocks(q, bm, *, n_seq):
    m, d = q.shape
    t = m // n_seq
    _, n_blk, n_h, dh = bm.shape
    assert n_blk <= SELECT_LANES and n_h * MOBA_TOP_K <= SELECT_LANES
    return pl.pallas_call(
        _select_kernel,
        grid=(n_seq,),
        in_specs=[
            pl.BlockSpec((t, d), lambda b: (b, 0)),
            pl.BlockSpec((None, n_blk, n_h, dh), lambda b: (b, 0, 0, 0)),
        ],
        out_specs=pl.BlockSpec((None, t, SELECT_LANES), lambda b: (b, 0, 0)),
        out_shape=jax.ShapeDtypeStruct((n_seq, t, SELECT_LANES), jnp.int32),
        compiler_params=_params(1),
        name="select_blocks",
    )(q, bm)


def _sample_attn_kernel(sel_ref, pt_ref, q_ref, kn_ref, vn_ref, ck_hbm, cv_hbm, o_ref, kg_ref, vg_ref, sem,
                        *, n_pages):
    b = pl.program_id(0)
    h = pl.program_id(1)
    n_h = pl.num_programs(1)
    n_steps = pl.num_programs(0) * n_h
    step = b * n_h + h
    slot = step % 2
    t_len = q_ref.shape[0]
    n_sel = MOBA_TOP_K * MOBA_BLOCK
    scale = HEAD_DIM ** -0.5

    def gather_copies(bb, hh, sl):
        copies = []
        for t in range(t_len):
            for r in range(MOBA_TOP_K):
                blk = sel_ref[((bb * t_len + t) * n_h + hh) * MOBA_TOP_K + r]
                for pg in range(PAGES_PER_BLOCK):
                    phys = pt_ref[bb * n_pages + blk * PAGES_PER_BLOCK + pg]
                    rows = pl.ds((r * PAGES_PER_BLOCK + pg) * PAGE_SIZE, PAGE_SIZE)
                    copies.append(pltpu.make_async_copy(ck_hbm.at[phys, :, hh, :], kg_ref.at[sl, t, rows, :], sem.at[sl, 0]))
                    copies.append(pltpu.make_async_copy(cv_hbm.at[phys, :, hh, :], vg_ref.at[sl, t, rows, :], sem.at[sl, 1]))
        return copies

    @pl.when(step == 0)
    def _():
        for cp in gather_copies(b, h, slot):
            cp.start()

    @pl.when(step + 1 < n_steps)
    def _():
        nxt = step + 1
        for cp in gather_copies(nxt // n_h, nxt % n_h, 1 - slot):
            cp.start()

    for cp in gather_copies(b, h, slot):
        cp.wait()

    q = q_ref[...]
    k_own = kn_ref[...]
    v_own = vn_ref[...]
    own_i = lax.broadcasted_iota(jnp.int32, (t_len, 1), 0)
    rows = []
    for t in range(t_len):
        q_t = q[t:t + 1, :]
        s_sel = jnp.sum(kg_ref[slot, t] * q_t, axis=1, keepdims=True) * scale
        s_o = jnp.sum(k_own * q_t, axis=1, keepdims=True) * scale
        s_o = jnp.where(own_i <= t, s_o, NEG_INF)
        m = jnp.maximum(jnp.max(s_sel, axis=0, keepdims=True), jnp.max(s_o, axis=0, keepdims=True))
        p_sel = jnp.exp(s_sel - m)
        p_o = jnp.exp(s_o - m)
        l = jnp.sum(p_sel, axis=0, keepdims=True) + jnp.sum(p_o, axis=0, keepdims=True)
        out = (jnp.sum(p_sel * vg_ref[slot, t], axis=0, keepdims=True)
               + jnp.sum(p_o * v_own, axis=0, keepdims=True))
        rows.append(out / l)
    o_ref[...] = jnp.concatenate(rows, axis=0)


def _moba_sample(q, k_new, v_new, cache_k3, cache_v3, sel_flat, page_table_flat, *, n_seq, n_pages):
    m, d = q.shape
    t = m // n_seq
    n_h = d // HEAD_DIM
    n_sel = MOBA_TOP_K * MOBA_BLOCK
    row_spec = pl.BlockSpec((t, HEAD_DIM), lambda b, h, sel, pt: (b, h))
    grid_spec = pltpu.PrefetchScalarGridSpec(
        num_scalar_prefetch=2,
        grid=(n_seq, n_h),
        in_specs=[row_spec, row_spec, row_spec, pl.BlockSpec(memory_space=pl.ANY), pl.BlockSpec(memory_space=pl.ANY)],
        out_specs=row_spec,
        scratch_shapes=[
            pltpu.VMEM((2, t, n_sel, HEAD_DIM), F32),
            pltpu.VMEM((2, t, n_sel, HEAD_DIM), F32),
            pltpu.SemaphoreType.DMA((2, 2)),
        ],
    )
    return pl.pallas_call(
        functools.partial(_sample_attn_kernel, n_pages=n_pages),
        grid_spec=grid_spec,
        out_shape=jax.ShapeDtypeStruct((m, d), F32),
        compiler_params=_params(2),
        name="moba_sample",
    )(sel_flat, page_table_flat, q, k_new, v_new, cache_k3, cache_v3)


def _rope_tables(pos):
    half = HEAD_DIM // 2
    inv = 1.0 / (ROPE_THETA ** (jnp.arange(half, dtype=F32) * (2.0 / HEAD_DIM)))
    ang = pos.astype(F32)[:, None] * inv[None, :]
    cos = jnp.cos(ang)
    sin = jnp.sin(ang)
    return jnp.concatenate([cos, cos], axis=-1), jnp.concatenate([-sin, sin], axis=-1)


def kernel(x_prompt, x_sample, state_pool, cache_k, cache_v, page_table, g_pool, w_pool, s_pool, g_ffn, w_gate_up,
           w_down, g_kv, w_kv, g_attn, w_q, w_o, g_final):
    n_b, seq, d = x_prompt.shape
    n_db, t_dec, _ = x_sample.shape
    depth = g_ffn.shape[0]
    n_pool = g_pool.shape[0]
    n_h = d // HEAD_DIM
    n_pages = page_table.shape[1]
    n_past_blk = PAST_LEN // MOBA_BLOCK
    m_p = n_b * seq
    m_s = n_db * t_dec
    state_rows = state_pool.shape[2]
    assert n_pages == n_past_blk * PAGES_PER_BLOCK, "own MoBA block must hold only the new tokens"
    assert seq % MOBA_BLOCK == 0 and state_rows == POOL_HALO - 1

    tm_pool, tm_ffn, tf, tm_proj = ROW_TILE, ROW_TILE, FF_CHUNK, ROW_TILE
    assert seq % ROW_TILE == 0 and w_down.shape[1] % FF_CHUNK == 0


    cos_p, sin_p = _rope_tables(jnp.arange(seq))
    cos_s, sin_s = _rope_tables(PAST_LEN + jnp.arange(t_dec))
    cos_s = jnp.tile(cos_s, (n_db, 1))
    sin_s = jnp.tile(sin_s, (n_db, 1))

    pt_flat = page_table.reshape(-1)

    hp = x_prompt
    hs = x_sample
    pool_p, pool_s, bm_parts = [], [], []
    assert n_pool > 0 and (n_db * n_pages) % (n_pool * PAGES_PER_BLOCK) == 0
    for layer in range(depth):
        last = layer == depth - 1
        if layer < n_pool:
            first = (w_gate_up, w_down, 0) if layer == 0 else None
            hp3, zt_p, *first_b = _pool_layer(hp.reshape(n_b, seq, d), hp.reshape(n_b, seq, d), g_pool[layer],
                                              w_pool[layer], s_pool[layer], tm=tm_pool, halo_normalized=False, pos0=0,
                                              round_weights=first)
            if first_b:
                w_gu_b, w_dn_b = first_b
            state = state_pool[layer]
            halo_s = jnp.pad(state, ((0, 0), (POOL_HALO - state_rows, 0), (0, 0)))
            hs3, zt_s = _pool_layer(hs.reshape(n_db, t_dec, d), halo_s, g_pool[layer], w_pool[layer], s_pool[layer],
                                    tm=t_dec, halo_normalized=True, pos0=PAST_LEN)
            pool_p.append(zt_p[:, POOL_HALO - state_rows:])
            pool_s.append(jnp.concatenate([state, zt_s], axis=1)[:, -state_rows:])
            hp = hp3.reshape(m_p, d)
            hs = hs3.reshape(m_s, d)
        else:
            a = layer - n_pool
            if a == 0:
                k_p, k_pb, kmean, k_s = _k_proj(hp, hs, g_kv, w_kv, cos_p, sin_p, cos_s, sin_s, tm=tm_proj)
                v_p, vt_pb, v_s = _v_proj(hp, hs, g_kv, w_kv, tm=tm_proj, seq=seq)
                kmean = kmean.reshape(n_b, seq // MOBA_BLOCK, d)
                bm = jnp.concatenate(bm_parts, axis=0).reshape(n_db, n_past_blk, n_h, HEAD_DIM)
            q_p, q_s = _q_proj(hp, hs, g_attn[a], w_q, a, cos_p, sin_p, cos_s, sin_s, tm=tm_proj)
            att_p = _moba_prompt(q_p, k_pb, vt_pb, kmean, batch=n_b, seq=seq)
            sel = _select_blocks(q_s, bm, n_seq=n_db)
            sel_flat = sel[:, :, :n_h * MOBA_TOP_K].reshape(-1)
            att_s = _moba_sample(q_s, k_s, v_s, cache_k, cache_v, sel_flat, pt_flat, n_seq=n_db, n_pages=n_pages)
            hp, hs = _o_proj(att_p, att_s, w_o, a, hp, hs, tm=tm_proj)
        pages_per_call = n_db * n_pages // n_pool
        cache_means = (cache_k, pt_flat, layer * pages_per_call, pages_per_call) if layer < n_pool else None
        next_weights = None if last else (w_gate_up, w_down, layer + 1)
        hp, hs, bm_part, next_b = _ffn(hp, hs, g_ffn[layer], w_gu_b, w_dn_b, g_final, tm=tm_ffn, tf=tf, final_norm=last,
                                       cache_means=cache_means, next_weights=next_weights)
        if bm_part is not None:
            bm_parts.append(bm_part)
        if next_b is not None:
            w_gu_b, w_dn_b = next_b

    y_prompt = hp.reshape(n_b, seq, d)
    y_sample = hs.reshape(n_db, t_dec, d)
    new_pool_prompt = jnp.stack(pool_p)
    new_pool_sample = jnp.stack(pool_s)
    shape_p = (n_b, seq, n_h, HEAD_DIM)
    shape_s = (n_db, t_dec, n_h, HEAD_DIM)
    return (y_prompt, y_sample, new_pool_prompt, new_pool_sample, k_p.reshape(shape_p), v_p.reshape(shape_p),
            k_s.reshape(shape_s), v_s.reshape(shape_s))
```
